```python
import jax, jax.numpy as jnp
from jax import lax
import numpy as np

D_MODEL = 1024
BATCH = 2
SEQ = 8192
DEPTH = 4

GRID_W = 64
CTX_LEN = 256
N_MIXERS = 4
HEAD_DIM = 64
AT_HEADS = 16
AT_KV_HEADS = 4
AT_GROUP = AT_HEADS // AT_KV_HEADS
AT_WIDTH = AT_HEADS * HEAD_DIM
AT_KV_WIDTH = AT_KV_HEADS * HEAD_DIM
Q_BLOCK = 128
ROPE_THETA = 10000.0
NA_HEADS = 16
NA_WIDTH = NA_HEADS * HEAD_DIM
NA_WIN_ROWS = 8
NA_WIN_COLS = 16
CONV_WIDTH = 31
FT_GROUPS = 4
FT_GROUP_WIDTH = D_MODEL // FT_GROUPS
D_FF = 4 * D_MODEL
N_MOD = 6
EPS = 1e-6
N_AT_LAYERS = (DEPTH + 3) // 4
N_NA_LAYERS = (DEPTH + 2) // 4
N_CV_LAYERS = (DEPTH + 1) // 4
N_FT_LAYERS = DEPTH // 4

kernel_name = 'hybrid_interleaved_diffusion_trunk'


def rms_norm(x, g):
    xf = x.astype(jnp.float32)
    y = xf * lax.rsqrt(jnp.mean(xf * xf, axis=-1, keepdims=True) + EPS)
    return (y * g.astype(jnp.float32)).astype(x.dtype)


def layer_norm(x, g, b):
    xf = x.astype(jnp.float32)
    mu = jnp.mean(xf, axis=-1, keepdims=True)
    var = jnp.mean(jnp.square(xf - mu), axis=-1, keepdims=True)
    y = (xf - mu) * lax.rsqrt(var + EPS)
    return (y * g.astype(jnp.float32) + b.astype(jnp.float32)).astype(x.dtype)


def ada_mods(vec, w, b):
    m = jax.nn.silu(vec) @ w + b
    return [mi[..., None, :] for mi in jnp.split(m, N_MOD, axis=-1)]


def modulate(h, shift, scale):
    return h * (1.0 + scale) + shift


def rope_angles(n_tok):
    t = jnp.arange(n_tok)
    row = (t // GRID_W).astype(jnp.float32)
    col = (t % GRID_W).astype(jnp.float32)
    n_axis = HEAD_DIM // 4
    inv = ROPE_THETA ** (-jnp.arange(n_axis, dtype=jnp.float32) / n_axis)
    return jnp.concatenate([row[:, None] * inv, col[:, None] * inv], axis=-1)


def apply_rope(x, ang):
    xf = x.astype(jnp.float32).reshape(x.shape[:-1] + (HEAD_DIM // 2, 2))
    cos = jnp.cos(ang)[None, :, None, :]
    sin = jnp.sin(ang)[None, :, None, :]
    x1, x2 = xf[..., 0], xf[..., 1]
    out = jnp.stack([x1 * cos - x2 * sin, x1 * sin + x2 * cos], axis=-1)
    return out.reshape(x.shape).astype(x.dtype)


def gqa_attend(q, k, v):
    s = jnp.einsum('bqkgd,bnkd->bkgqn', q, k, preferred_element_type=jnp.float32)
    p = jax.nn.softmax(s, axis=-1).astype(v.dtype)
    return jnp.einsum('bkgqn,bnkd->bqkgd', p, v)


def gqa_mixer(hl, hc, ang, w_qkv, q_g, k_g, w_o, ctx_out):
    B, S, _ = hl.shape
    C = hc.shape[1]
    scale = HEAD_DIM ** -0.5
    ql, kl, vl = jnp.split(hl @ w_qkv, [AT_WIDTH, AT_WIDTH + AT_KV_WIDTH], axis=-1)
    ql = apply_rope(rms_norm(ql.reshape(B, S, AT_HEADS, HEAD_DIM), q_g), ang) * scale
    kl = apply_rope(rms_norm(kl.reshape(B, S, AT_KV_HEADS, HEAD_DIM), k_g), ang)
    vl = vl.reshape(B, S, AT_KV_HEADS, HEAD_DIM)
    kc, vc = jnp.split(hc @ w_qkv[:, AT_WIDTH:], 2, axis=-1)
    kc = rms_norm(kc.reshape(B, C, AT_KV_HEADS, HEAD_DIM), k_g)
    vc = vc.reshape(B, C, AT_KV_HEADS, HEAD_DIM)
    k_all = jnp.concatenate([kc, kl], axis=1)
    v_all = jnp.concatenate([vc, vl], axis=1)
    nb = S // Q_BLOCK
    qb = ql.reshape(B, nb, Q_BLOCK, AT_KV_HEADS, AT_GROUP, HEAD_DIM).transpose(1, 0, 2, 3, 4, 5)
    ob = lax.map(lambda q: gqa_attend(q, k_all, v_all), qb)
    yl = ob.transpose(1, 0, 2, 3, 4, 5).reshape(B, S, AT_WIDTH) @ w_o
    yc = None
    if ctx_out:
        qc = rms_norm((hc @ w_qkv[:, :AT_WIDTH]).reshape(B, C, AT_HEADS, HEAD_DIM), q_g) * scale
        oc = gqa_attend(qc.reshape(B, C, AT_KV_HEADS, AT_GROUP, HEAD_DIM), kc, vc)
        yc = oc.reshape(B, C, AT_WIDTH) @ w_o
    return yc, yl


def na_mixer(hl, hc, n_rows, w_qkv, rpb, w_o, ctx_out):
    B, S, _ = hl.shape
    C = hc.shape[1]
    scale = HEAD_DIM ** -0.5
    ql, kl, vl = jnp.split(hl @ w_qkv, 3, axis=-1)
    kc, vc = jnp.split(hc @ w_qkv[:, NA_WIDTH:], 2, axis=-1)
    kc = kc.reshape(B, C, NA_HEADS, HEAD_DIM)
    vc = vc.reshape(B, C, NA_HEADS, HEAD_DIM)
    wh = min(NA_WIN_ROWS, n_rows)
    ww = NA_WIN_COLS
    row_start = jnp.clip(jnp.arange(n_rows) - wh // 2, 0, n_rows - wh)
    cols = jnp.arange(GRID_W)
    col_idx = jnp.clip(cols - ww // 2, 0, GRID_W - ww)[:, None] + jnp.arange(ww)
    col_bias_idx = col_idx - cols[:, None] + (NA_WIN_COLS - 1)
    rpb_cols = rpb[:, :, col_bias_idx]
    qg = (ql * scale).reshape(B, n_rows, GRID_W, NA_HEADS, HEAD_DIM).transpose(1, 0, 2, 3, 4)
    kg = kl.reshape(B, n_rows, GRID_W, NA_HEADS, HEAD_DIM)
    vg = vl.reshape(B, n_rows, GRID_W, NA_HEADS, HEAD_DIM)

    def row_fn(args):
        r, q = args
        rs = row_start[r]
        k_nb = lax.dynamic_slice_in_dim(kg, rs, wh, axis=1)[:, :, col_idx]
        v_nb = lax.dynamic_slice_in_dim(vg, rs, wh, axis=1)[:, :, col_idx]
        s_nb = jnp.einsum('bqhd,biqjhd->bhqij', q, k_nb, preferred_element_type=jnp.float32)
        row_bias_idx = rs + jnp.arange(wh) - r + (NA_WIN_ROWS - 1)
        bias = rpb_cols[:, row_bias_idx].transpose(0, 2, 1, 3).astype(jnp.float32)
        s_nb = s_nb + bias[None]
        s_ctx = jnp.einsum('bqhd,bnhd->bhqn', q, kc, preferred_element_type=jnp.float32)
        s = jnp.concatenate([s_nb.reshape(B, NA_HEADS, GRID_W, wh * ww), s_ctx], axis=-1)
        p = jax.nn.softmax(s, axis=-1).astype(vg.dtype)
        p_nb = p[..., :wh * ww].reshape(B, NA_HEADS, GRID_W, wh, ww)
        p_ctx = p[..., wh * ww:]
        return (jnp.einsum('bhqij,biqjhd->bqhd', p_nb, v_nb)
                + jnp.einsum('bhqn,bnhd->bqhd', p_ctx, vc))

    o = lax.map(row_fn, (jnp.arange(n_rows), qg))
    yl = o.transpose(1, 0, 2, 3, 4).reshape(B, S, NA_WIDTH) @ w_o
    yc = None
    if ctx_out:
        qc = (hc @ w_qkv[:, :NA_WIDTH]).reshape(B, C, NA_HEADS, 1, HEAD_DIM) * scale
        yc = gqa_attend(qc, kc, vc).reshape(B, C, NA_WIDTH) @ w_o
    return yc, yl


def conv_module(h, w_pw1, b_pw1, w_dw, b_dw, ln_g, ln_b, w_pw2, b_pw2):
    a, g = jnp.split(h @ w_pw1 + b_pw1, 2, axis=-1)
    u = a * jax.nn.sigmoid(g)
    pad = CONV_WIDTH // 2
    u = lax.conv_general_dilated(u, w_dw[:, None, :], window_strides=(1,), padding=[(pad, pad)],
                                 dimension_numbers=('NWC', 'WIO', 'NWC'),
                                 feature_group_count=u.shape[-1]) + b_dw
    u = jax.nn.silu(layer_norm(u, ln_g, ln_b))
    return u @ w_pw2 + b_pw2


def fourier_mixer(h, w, b):
    B, n, D = h.shape
    hf = h.astype(jnp.float32).reshape(B, n, FT_GROUPS, FT_GROUP_WIDTH)
    z = jnp.fft.fft2(hf, axes=(1, 3), norm='ortho').real.astype(h.dtype).reshape(B, n, D)
    return z @ w + b


def sq_relu_mlp(h, w1, w2):
    return jnp.square(jax.nn.relu(h @ w1)) @ w2


def setup_inputs(seed: int = 0) -> dict:
    key = jax.random.key(seed)
    ks = jax.random.split(key, 32)
    f32 = jnp.float32
    D = D_MODEL

    def nrm(k, shape, s):
        return jax.random.normal(k, shape, f32) * s

    def gain(k, shape):
        return 1.0 + 0.02 * jax.random.normal(k, shape, f32)

    return {
        'x': nrm(ks[0], (BATCH, SEQ, D), 1.0),
        'c': nrm(ks[1], (BATCH, D), 1.0),
        'ctx': nrm(ks[2], (BATCH, CTX_LEN, D), 1.0),
        'c_ctx': nrm(ks[3], (D,), 1.0),
        'ada_w': nrm(ks[4], (DEPTH, D, N_MOD * D), 0.5 * D ** -0.5),
        'ada_b': nrm(ks[5], (DEPTH, N_MOD * D), 0.02),
        'norm1_g': gain(ks[6], (DEPTH, D)),
        'norm2_g': gain(ks[7], (DEPTH, D)),
        'mlp_w1': nrm(ks[8], (DEPTH, D, D_FF), D ** -0.5),
        'mlp_w2': nrm(ks[9], (DEPTH, D_FF, D), D_FF ** -0.5),
        'final_g': gain(ks[10], (D,)),
        'at_w_qkv': nrm(ks[11], (N_AT_LAYERS, D, AT_WIDTH + 2 * AT_KV_WIDTH), D ** -0.5),
        'at_q_g': gain(ks[12], (N_AT_LAYERS, HEAD_DIM)),
        'at_k_g': gain(ks[13], (N_AT_LAYERS, HEAD_DIM)),
        'at_w_o': nrm(ks[14], (N_AT_LAYERS, AT_WIDTH, D), AT_WIDTH ** -0.5),
        'na_w_qkv': nrm(ks[15], (N_NA_LAYERS, D, 3 * NA_WIDTH), D ** -0.5),
        'na_rpb': nrm(ks[16], (N_NA_LAYERS, NA_HEADS, 2 * NA_WIN_ROWS - 1, 2 * NA_WIN_COLS - 1), 0.1),
        'na_w_o': nrm(ks[17], (N_NA_LAYERS, NA_WIDTH, D), NA_WIDTH ** -0.5),
        'cv_w_pw1': nrm(ks[18], (N_CV_LAYERS, D, 2 * D), D ** -0.5),
        'cv_b_pw1': nrm(ks[19], (N_CV_LAYERS, 2 * D), 0.02),
        'cv_w_dw': nrm(ks[20], (N_CV_LAYERS, CONV_WIDTH, D), CONV_WIDTH ** -0.5),
        'cv_b_dw': nrm(ks[21], (N_CV_LAYERS, D), 0.02),
        'cv_ln_g': gain(ks[22], (N_CV_LAYERS, D)),
        'cv_ln_b': nrm(ks[23], (N_CV_LAYERS, D), 0.02),
        'cv_w_pw2': nrm(ks[24], (N_CV_LAYERS, D, D), D ** -0.5),
        'cv_b_pw2': nrm(ks[25], (N_CV_LAYERS, D), 0.02),
        'ft_w': nrm(ks[26], (N_FT_LAYERS, D, D), D ** -0.5),
        'ft_b': nrm(ks[27], (N_FT_LAYERS, D), 0.02),
    }


def reference(x, c, ctx, c_ctx, ada_w, ada_b, norm1_g, norm2_g, mlp_w1, mlp_w2, final_g,
              at_w_qkv, at_q_g, at_k_g, at_w_o, na_w_qkv, na_rpb, na_w_o,
              cv_w_pw1, cv_b_pw1, cv_w_dw, cv_b_dw, cv_ln_g, cv_ln_b, cv_w_pw2, cv_b_pw2,
              ft_w, ft_b):
    n_lat = x.shape[1]
    n_rows = n_lat // GRID_W
    ang = rope_angles(n_lat)
    h_ctx = ctx
    for i in range(DEPTH):
        kind = i % N_MIXERS
        occ = i // N_MIXERS
        ctx_later = any((j % N_MIXERS) in (0, 1) for j in range(i + 1, DEPTH))
        ctx_read = kind in (0, 1)
        ml = ada_mods(c, ada_w[i], ada_b[i])
        hl = modulate(rms_norm(x, norm1_g[i]), ml[0], ml[1])
        hc = None
        mc = None
        if ctx_read or ctx_later:
            mc = ada_mods(c_ctx, ada_w[i], ada_b[i])
            hc = modulate(rms_norm(h_ctx, norm1_g[i]), mc[0], mc[1])
        if kind == 0:
            yc, yl = gqa_mixer(hl, hc, ang, at_w_qkv[occ], at_q_g[occ], at_k_g[occ], at_w_o[occ], ctx_later)
        elif kind == 1:
            yc, yl = na_mixer(hl, hc, n_rows, na_w_qkv[occ], na_rpb[occ], na_w_o[occ], ctx_later)
        elif kind == 2:
            cv = (cv_w_pw1[occ], cv_b_pw1[occ], cv_w_dw[occ], cv_b_dw[occ], cv_ln_g[occ], cv_ln_b[occ],
                  cv_w_pw2[occ], cv_b_pw2[occ])
            yl = conv_module(hl, *cv)
            yc = conv_module(hc, *cv) if ctx_later else None
        else:
            yl = fourier_mixer(hl, ft_w[occ], ft_b[occ])
            yc = fourier_mixer(hc, ft_w[occ], ft_b[occ]) if ctx_later else None
        x = x + ml[2] * yl
        x = x + ml[5] * sq_relu_mlp(modulate(rms_norm(x, norm2_g[i]), ml[3], ml[4]), mlp_w1[i], mlp_w2[i])
        if ctx_later:
            h_ctx = h_ctx + mc[2] * yc
            h_ctx = h_ctx + mc[5] * sq_relu_mlp(modulate(rms_norm(h_ctx, norm2_g[i]), mc[3], mc[4]),
                                                mlp_w1[i], mlp_w2[i])
    return rms_norm(x, final_g)
```

```python
import functools

import jax
import jax.numpy as jnp
import numpy as np
from jax import lax
from jax.experimental import pallas as pl
from jax.experimental.pallas import tpu as pltpu

GRID_W = 64
N_MIXERS = 4
HEAD_DIM = 64
AT_HEADS = 16
AT_KV_HEADS = 4
AT_WIDTH = AT_HEADS * HEAD_DIM
AT_KV_WIDTH = AT_KV_HEADS * HEAD_DIM
ROPE_THETA = 10000.0
NA_HEADS = 16
NA_WIDTH = NA_HEADS * HEAD_DIM
NA_WIN_ROWS = 8
NA_WIN_COLS = 16
CONV_WIDTH = 31
FT_GROUPS = 4
N_MOD = 6
EPS = 1e-6
QK_SCALE = HEAD_DIM ** -0.5
LOG2E = float(np.log2(np.e))

LANES = 128
BF16_ROWS = 16
NA_QROWS = 4
NA_KROWS = 12
CONV_HALO = 16
CONV_ROWS = 64
CONV_LANES = 256
SUBLANES = 8
FT_NA = 64
NEG_BIG = -1e30

BF = jnp.bfloat16
F32 = jnp.float32


def _cparams(sem, vmem_mib):
    return pltpu.CompilerParams(dimension_semantics=sem, vmem_limit_bytes=vmem_mib << 20)


def _norm_mod(x, g, shift, scale):
    ms = jnp.mean(x * x, axis=-1, keepdims=True)
    return (x * lax.rsqrt(ms + EPS) * g) * (1.0 + scale) + shift


def _ada_kernel(v_ref, w_ref, b_ref, o_ref):
    v = v_ref[...]
    sv = v * jax.nn.sigmoid(v)
    o_ref[0] = jnp.dot(sv, w_ref[0], preferred_element_type=F32, precision=lax.Precision.HIGHEST) + b_ref[0]


def _ada_mods(vec8, ada_w, ada_b):
    depth, d, n = ada_w.shape
    tn = n // 4
    return pl.pallas_call(
        _ada_kernel,
        grid=(depth, n // tn),
        in_specs=[
            pl.BlockSpec((8, d), lambda l, j: (0, 0)),
            pl.BlockSpec((1, d, tn), lambda l, j: (l, 0, j)),
            pl.BlockSpec((1, 1, tn), lambda l, j: (l, 0, j)),
        ],
        out_specs=pl.BlockSpec((1, 8, tn), lambda l, j: (l, 0, j)),
        out_shape=jax.ShapeDtypeStruct((depth, 8, n), F32),
        compiler_params=_cparams(("arbitrary", "arbitrary"), 40),
        name="ada_mods",
    )(vec8, ada_w, ada_b.reshape(depth, 1, n))


def _mod_map(n_mod_rows):
    if n_mod_rows == 1:
        return lambda b, *_: (0, 0, 0)
    return lambda b, *_: (b, 0, 0)


def _gqa_proj_kernel(*refs, rope, tm):
    if rope:
        x_ref, mod_ref, g1_ref, w_ref, hm_ref, qg_ref, kg_ref, cos_ref, sin_ref, qt_ref, k_ref, vt_ref = refs
    else:
        x_ref, mod_ref, g1_ref, w_ref, hm_ref, qg_ref, kg_ref, qt_ref, k_ref, vt_ref = refs
    h = _norm_mod(x_ref[0], g1_ref[...], mod_ref[0, 0:1, :], mod_ref[0, 1:2, :]).astype(BF)
    qkv = jnp.dot(h, w_ref[...], preferred_element_type=F32)
    q = qkv[:, :AT_WIDTH]
    k = qkv[:, AT_WIDTH:AT_WIDTH + AT_KV_WIDTH]
    v = qkv[:, AT_WIDTH + AT_KV_WIDTH:]
    hm = hm_ref[...]
    q = q * lax.rsqrt(jnp.dot((q * q).astype(BF), hm, preferred_element_type=F32) + EPS) * qg_ref[...]
    hk = hm[:AT_KV_WIDTH, :AT_KV_WIDTH]
    k = k * lax.rsqrt(jnp.dot((k * k).astype(BF), hk, preferred_element_type=F32) + EPS) * kg_ref[...]
    if rope:
        cos = cos_ref[...]
        sin = sin_ref[...]
        even = (lax.broadcasted_iota(jnp.int32, (tm, LANES), 1) % 2) == 0

        def rot(t):
            partner = jnp.where(even, pltpu.roll(t, LANES - 1, 1), pltpu.roll(t, 1, 1))
            return t * cos + partner * sin
    else:
        def rot(t):
            return t
    for c in range(AT_WIDTH // LANES):
        qc = rot(q[:, c * LANES:(c + 1) * LANES]) * (QK_SCALE * LOG2E)
        qt_ref[0, c * LANES:(c + 1) * LANES, :] = qc.T.astype(BF)
    for c in range(AT_KV_WIDTH // LANES):
        kc = rot(k[:, c * LANES:(c + 1) * LANES]).astype(BF)
        k_ref[0, 2 * c] = kc[:, :HEAD_DIM]
        k_ref[0, 2 * c + 1] = kc[:, HEAD_DIM:]
    vt_ref[0] = v.T.astype(BF)


def _gqa_proj(x, mod, g1, w, hm, qg, kg, cos, sin, *, tm=256):
    bx, t, d = x.shape
    rope = cos is not None
    n = w.shape[1]
    const = lambda b, i: (0, 0)
    in_specs = [
        pl.BlockSpec((1, tm, d), lambda b, i: (b, i, 0)),
        pl.BlockSpec((1, N_MOD, d), _mod_map(mod.shape[0])),
        pl.BlockSpec((1, d), const),
        pl.BlockSpec((d, n), const),
        pl.BlockSpec(hm.shape, const),
        pl.BlockSpec((1, AT_WIDTH), const),
        pl.BlockSpec((1, AT_KV_WIDTH), const),
    ]
    args = [x, mod, g1, w, hm, qg, kg]
    if rope:
        in_specs += [pl.BlockSpec((tm, LANES), lambda b, i: (i, 0))] * 2
        args += [cos, sin]
    return pl.pallas_call(
        functools.partial(_gqa_proj_kernel, rope=rope, tm=tm),
        grid=(bx, t // tm),
        in_specs=in_specs,
        out_specs=[
            pl.BlockSpec((1, AT_WIDTH, tm), lambda b, i: (b, 0, i)),
            pl.BlockSpec((1, AT_KV_HEADS, tm, HEAD_DIM), lambda b, i: (b, 0, i, 0)),
            pl.BlockSpec((1, AT_KV_WIDTH, tm), lambda b, i: (b, 0, i)),
        ],
        out_shape=[
            jax.ShapeDtypeStruct((bx, AT_WIDTH, t), BF),
            jax.ShapeDtypeStruct((bx, AT_KV_HEADS, t, HEAD_DIM), BF),
            jax.ShapeDtypeStruct((bx, AT_KV_WIDTH, t), BF),
        ],
        compiler_params=_cparams(("parallel", "parallel"), 48),
        name="gqa_proj_rope" if rope else "gqa_proj_ctx",
    )(*args)


def _flash_kernel(qt_ref, k_ref, vt_ref, ot_ref, m_sc, acc_sc, *, tk, n_chunks):
    q = qt_ref[0]

    def scores(c):
        return jnp.dot(k_ref[0, 0, c * tk:(c + 1) * tk, :], q, preferred_element_type=F32)

    def weighted(c, s, m):
        return jnp.dot(vt_ref[0, 0, :, c * tk:(c + 1) * tk], jnp.exp2(s - m).astype(BF), preferred_element_type=F32)

    s0 = scores(0)
    m0 = jnp.max(s0, axis=0, keepdims=True)
    acc = weighted(0, s0, m0)
    for c in range(1, n_chunks):
        acc = acc + weighted(c, scores(c), m0)
    denom = acc[HEAD_DIM:HEAD_DIM + 1]
    overflowed = jnp.max(jnp.where(jnp.isfinite(denom), 0.0, 1.0)) > 0.0

    @pl.when(jnp.logical_not(overflowed))
    def _():
        ot_ref[0] = (acc[:HEAD_DIM] * (1.0 / denom)).astype(BF)

    @pl.when(overflowed)
    def _():
        m_sc[...] = jnp.full(m_sc.shape, NEG_BIG, F32)
        acc_sc[...] = jnp.zeros(acc_sc.shape, F32)

        def body(c, carry):
            off = pl.multiple_of(c * tk, tk)
            s = jnp.dot(k_ref[0, 0, pl.ds(off, tk), :], q, preferred_element_type=F32)
            m_prev = m_sc[...]
            m_new = jnp.maximum(m_prev, jnp.max(s, axis=0, keepdims=True))
            pv = jnp.dot(vt_ref[0, 0, :, pl.ds(off, tk)], jnp.exp2(s - m_new).astype(BF), preferred_element_type=F32)
            acc_sc[...] = jnp.exp2(m_prev - m_new) * acc_sc[...] + pv
            m_sc[...] = m_new
            return carry

        lax.fori_loop(0, n_chunks, body, 0)
        a = acc_sc[...]
        ot_ref[0] = (a[:HEAD_DIM] * (1.0 / a[HEAD_DIM:HEAD_DIM + 1])).astype(BF)


def _flash(qt, k, vt, *, tq, tk):
    b, w, t = qt.shape
    kv, tkv = k.shape[1], k.shape[2]
    heads = w // HEAD_DIM
    group = heads // kv
    rows = HEAD_DIM + BF16_ROWS
    ones = jnp.zeros((b, kv, BF16_ROWS, tkv), BF).at[:, :, 0].set(1.0)
    vt_ext = jnp.concatenate([vt.reshape(b, kv, HEAD_DIM, tkv), ones], axis=2)
    return pl.pallas_call(
        functools.partial(_flash_kernel, tk=tk, n_chunks=tkv // tk),
        grid=(b, heads, t // tq),
        in_specs=[
            pl.BlockSpec((1, HEAD_DIM, tq), lambda bi, h, i: (bi, h, i)),
            pl.BlockSpec((1, 1, tkv, HEAD_DIM), lambda bi, h, i: (bi, h // group, 0, 0)),
            pl.BlockSpec((1, 1, rows, tkv), lambda bi, h, i: (bi, h // group, 0, 0)),
        ],
        out_specs=pl.BlockSpec((1, HEAD_DIM, tq), lambda bi, h, i: (bi, h, i)),
        out_shape=jax.ShapeDtypeStruct((b, w, t), BF),
        scratch_shapes=[pltpu.VMEM((1, tq), F32), pltpu.VMEM((rows, tq), F32)],
        compiler_params=_cparams(("parallel", "parallel", "parallel"), 48),
        name="gqa_flash",
    )(qt, k, vt_ext)


def _oproj_kernel(ot_ref, w_ref, x_ref, mod_ref, o_ref):
    y = lax.dot_general(ot_ref[0], w_ref[...], (((0,), (0,)), ((), ())), preferred_element_type=F32)
    o_ref[0] = x_ref[0] + mod_ref[0, 2:3, :] * y


def _oproj(ot, w, x, mod, *, tm):
    b, t, d = x.shape
    kw = ot.shape[1]
    return pl.pallas_call(
        _oproj_kernel,
        grid=(b, t // tm),
        in_specs=[
            pl.BlockSpec((1, kw, tm), lambda bi, i: (bi, 0, i)),
            pl.BlockSpec((kw, d), lambda bi, i: (0, 0)),
            pl.BlockSpec((1, tm, d), lambda bi, i: (bi, i, 0)),
            pl.BlockSpec((1, N_MOD, d), _mod_map(mod.shape[0])),
        ],
        out_specs=pl.BlockSpec((1, tm, d), lambda bi, i: (bi, i, 0)),
        out_shape=jax.ShapeDtypeStruct((b, t, d), F32),
        compiler_params=_cparams(("parallel", "parallel"), 48),
        name="attn_oproj",
    )(ot, w, x, mod)


def _mlp_kernel(*refs, final):
    if final:
        x_ref, mod_ref, g2_ref, w1_ref, w2_ref, fg_ref, o_ref, h_sc, acc_sc = refs
    else:
        x_ref, mod_ref, g2_ref, w1_ref, w2_ref, o_ref, h_sc, acc_sc = refs
    j = pl.program_id(2)

    @pl.when(j == 0)
    def _():
        h_sc[...] = _norm_mod(x_ref[0], g2_ref[...], mod_ref[0, 3:4, :], mod_ref[0, 4:5, :]).astype(BF)
        acc_sc[...] = jnp.zeros(acc_sc.shape, F32)

    a = jnp.dot(h_sc[...], w1_ref[...], preferred_element_type=F32)
    a = jnp.square(jnp.maximum(a, 0.0)).astype(BF)
    acc_sc[...] += jnp.dot(a, w2_ref[...], preferred_element_type=F32)

    @pl.when(j == pl.num_programs(2) - 1)
    def _():
        y = x_ref[0] + mod_ref[0, 5:6, :] * acc_sc[...]
        if final:
            ms = jnp.mean(y * y, axis=-1, keepdims=True)
            y = y * lax.rsqrt(ms + EPS) * fg_ref[...]
        o_ref[0] = y


def _mlp(x, mod, g2, w1, w2, final_g=None, *, tm, tf=512):
    b, t, d = x.shape
    f = w1.shape[1]
    final = final_g is not None
    in_specs = [
        pl.BlockSpec((1, tm, d), lambda bi, i, j: (bi, i, 0)),
        pl.BlockSpec((1, N_MOD, d), _mod_map(mod.shape[0])),
        pl.BlockSpec((1, d), lambda bi, i, j: (0, 0)),
        pl.BlockSpec((d, tf), lambda bi, i, j: (0, j)),
        pl.BlockSpec((tf, d), lambda bi, i, j: (j, 0)),
    ]
    args = [x, mod, g2, w1, w2]
    if final:
        in_specs.append(pl.BlockSpec((1, d), lambda bi, i, j: (0, 0)))
        args.append(final_g)
    return pl.pallas_call(
        functools.partial(_mlp_kernel, final=final),
        grid=(b, t // tm, f // tf),
        in_specs=in_specs,
        out_specs=pl.BlockSpec((1, tm, d), lambda bi, i, j: (bi, i, 0)),
        out_shape=jax.ShapeDtypeStruct((b, t, d), F32),
        scratch_shapes=[pltpu.VMEM((tm, d), BF), pltpu.VMEM((tm, d), F32)],
        compiler_params=_cparams(("parallel", "parallel", "arbitrary"), 52),
        name="mlp_final" if final else "mlp",
    )(*args)


def _na_proj_kernel(x_ref, mod_ref, g1_ref, w_ref, qt_ref, k_ref, v_ref):
    h = _norm_mod(x_ref[0], g1_ref[...], mod_ref[0, 0:1, :], mod_ref[0, 1:2, :]).astype(BF)
    qkv = jnp.dot(h, w_ref[...], preferred_element_type=F32)
    for c in range(NA_WIDTH // LANES):
        qt_ref[0, c * LANES:(c + 1) * LANES, :] = (qkv[:, c * LANES:(c + 1) * LANES] * (QK_SCALE * LOG2E)).T.astype(BF)
    k_ref[0] = qkv[:, NA_WIDTH:2 * NA_WIDTH].astype(BF)
    v_ref[0] = qkv[:, 2 * NA_WIDTH:].astype(BF)


def _na_proj(x, mod, g1, w, *, tm=256):
    bx, t, d = x.shape
    return pl.pallas_call(
        _na_proj_kernel,
        grid=(bx, t // tm),
        in_specs=[
            pl.BlockSpec((1, tm, d), lambda b, i: (b, i, 0)),
            pl.BlockSpec((1, N_MOD, d), _mod_map(mod.shape[0])),
            pl.BlockSpec((1, d), lambda b, i: (0, 0)),
            pl.BlockSpec(w.shape, lambda b, i: (0, 0)),
        ],
        out_specs=[
            pl.BlockSpec((1, NA_WIDTH, tm), lambda b, i: (b, 0, i)),
            pl.BlockSpec((1, tm, NA_WIDTH), lambda b, i: (b, i, 0)),
            pl.BlockSpec((1, tm, NA_WIDTH), lambda b, i: (b, i, 0)),
        ],
        out_shape=[
            jax.ShapeDtypeStruct((bx, NA_WIDTH, t), BF),
            jax.ShapeDtypeStruct((bx, t, NA_WIDTH), BF),
            jax.ShapeDtypeStruct((bx, t, NA_WIDTH), BF),
        ],
        compiler_params=_cparams(("parallel", "parallel"), 48),
        name="na_proj",
    )(x, mod, g1, w)


def _na_key_base(g, n_groups):
    return jnp.clip(NA_QROWS * g - NA_WIN_ROWS // 2, 0, NA_QROWS * n_groups - NA_KROWS)


def _na_kernel(qt_ref, k_ref, v_ref, kc_ref, vc_ref, bias_ref, ot_ref, *, n_groups):
    g = pl.program_id(2)
    kb = pl.multiple_of(_na_key_base(g, n_groups) * GRID_W, GRID_W)
    kw = k_ref[0, pl.ds(kb, NA_KROWS * GRID_W), :]
    vw = v_ref[0, pl.ds(kb, NA_KROWS * GRID_W), :]
    kc = kc_ref[0]
    vc = vc_ref[0]
    q2 = qt_ref[0]
    upper = lax.broadcasted_iota(jnp.int32, q2.shape, 0) < HEAD_DIM
    tn = (((0,), (0,)), ((), ()))
    scores = []
    for half in range(2):
        qh = jnp.where(upper if half == 0 else jnp.logical_not(upper), q2, jnp.zeros_like(q2))
        scores.append((jnp.dot(kw, qh, preferred_element_type=F32) + bias_ref[0, half],
                       jnp.dot(kc, qh, preferred_element_type=F32)))
    outs = []
    for s_nb, s_cx in scores:
        m = jnp.maximum(jnp.max(s_nb, axis=0, keepdims=True), jnp.max(s_cx, axis=0, keepdims=True))
        p_nb = jnp.exp2(s_nb - m)
        p_cx = jnp.exp2(s_cx - m)
        l = jnp.sum(p_nb, axis=0, keepdims=True) + jnp.sum(p_cx, axis=0, keepdims=True)
        o = (lax.dot_general(vw, p_nb.astype(BF), tn, preferred_element_type=F32)
             + lax.dot_general(vc, p_cx.astype(BF), tn, preferred_element_type=F32))
        outs.append(o * (1.0 / l))
    ot_ref[0] = jnp.where(upper, outs[0], outs[1]).astype(BF)


def _na_bias_tables(rpb, n_rows):
    n_groups = n_rows // NA_QROWS
    n_h = rpb.shape[0]
    rp = jnp.pad(rpb, ((0, 0), (NA_KROWS, NA_KROWS), (GRID_W, GRID_W)))
    c0 = GRID_W + NA_WIN_COLS - 1
    tc = jnp.stack([rp[:, :, c0 - qc:c0 - qc + GRID_W] for qc in range(GRID_W)], axis=-1)
    tabs = []
    for g in (0, 1, n_groups - 1):
        r0 = NA_QROWS * g
        kb = int(np.clip(NA_QROWS * g - NA_WIN_ROWS // 2, 0, n_rows - NA_KROWS))
        a = np.arange(NA_QROWS)[None, None, :, None]
        qc = np.arange(GRID_W)[None, None, None, :]
        i = np.arange(NA_KROWS)[:, None, None, None]
        kc = np.arange(GRID_W)[None, :, None, None]
        r = r0 + a
        rs = np.clip(r - NA_WIN_ROWS // 2, 0, n_rows - NA_WIN_ROWS)
        krow = kb + i
        cs = np.clip(qc - NA_WIN_COLS // 2, 0, GRID_W - NA_WIN_COLS)
        ok = (krow >= rs) & (krow < rs + NA_WIN_ROWS) & (kc >= cs) & (kc < cs + NA_WIN_COLS)
        ok = np.broadcast_to(ok, (NA_KROWS, GRID_W, NA_QROWS, GRID_W))
        r_base = NA_KROWS + kb - r0 + NA_WIN_ROWS - 1
        per_a = [tc[:, r_base - qa:r_base - qa + NA_KROWS] for qa in range(NA_QROWS)]
        tab = jnp.stack(per_a, axis=3)
        tab = jnp.where(ok[None], tab * LOG2E, NEG_BIG)
        tabs.append(tab.reshape(n_h, NA_KROWS * GRID_W, NA_QROWS * GRID_W))
    return jnp.stack(tabs).astype(F32)


def _na_attend(qt, k, v, kc, vc, bias, n_rows):
    b, w, s = qt.shape
    c = kc.shape[1]
    n_groups = n_rows // NA_QROWS
    tq = NA_QROWS * GRID_W
    pair = 2 * HEAD_DIM

    def bias_map(bi, hp, g):
        return (jnp.where(g == 0, 0, jnp.where(g == n_groups - 1, 2, 1)), hp, 0, 0)

    return pl.pallas_call(
        functools.partial(_na_kernel, n_groups=n_groups),
        grid=(b, w // pair, n_groups),
        in_specs=[
            pl.BlockSpec((1, pair, tq), lambda bi, hp, g: (bi, hp, g)),
            pl.BlockSpec((1, s, pair), lambda bi, hp, g: (bi, 0, hp)),
            pl.BlockSpec((1, s, pair), lambda bi, hp, g: (bi, 0, hp)),
            pl.BlockSpec((1, c, pair), lambda bi, hp, g: (bi, 0, hp)),
            pl.BlockSpec((1, c, pair), lambda bi, hp, g: (bi, 0, hp)),
            pl.BlockSpec((1, 2, NA_KROWS * GRID_W, tq), bias_map),
        ],
        out_specs=pl.BlockSpec((1, pair, tq), lambda bi, hp, g: (bi, hp, g)),
        out_shape=jax.ShapeDtypeStruct((b, w, s), BF),
        compiler_params=_cparams(("parallel", "parallel", "arbitrary"), 48),
        name="na_attend",
    )(qt, k, v, kc, vc, bias)


def _conv_pw1_kernel(x_ref, mod_ref, g1_ref, w_ref, b_ref, u_ref):
    h = _norm_mod(x_ref[0], g1_ref[...], mod_ref[0, 0:1, :], mod_ref[0, 1:2, :]).astype(BF)
    ag = jnp.dot(h, w_ref[...], preferred_element_type=F32) + b_ref[...]
    d = u_ref.shape[2]
    u_ref[0] = ag[:, :d] * jax.nn.sigmoid(ag[:, d:])


def _conv_pw1(x, mod, g1, w, b, *, tm=512):
    bx, t, d = x.shape
    return pl.pallas_call(
        _conv_pw1_kernel,
        grid=(bx, t // tm),
        in_specs=[
            pl.BlockSpec((1, tm, d), lambda bi, i: (bi, i, 0)),
            pl.BlockSpec((1, N_MOD, d), _mod_map(mod.shape[0])),
            pl.BlockSpec((1, d), lambda bi, i: (0, 0)),
            pl.BlockSpec(w.shape, lambda bi, i: (0, 0)),
            pl.BlockSpec((1, 2 * d), lambda bi, i: (0, 0)),
        ],
        out_specs=pl.BlockSpec((1, tm, d), lambda bi, i: (bi, i, 0)),
        out_shape=jax.ShapeDtypeStruct((bx, t, d), F32),
        compiler_params=_cparams(("parallel", "parallel"), 48),
        name="conv_pw1_glu",
    )(x, mod, g1, w, b)


def _conv_tail_kernel(u_ref, up_ref, un_ref, wdw_ref, bdw_ref, lg_ref, lb_ref, w2_ref, b2_ref, x_ref, mod_ref,
                      o_ref, buf_sc, cv_sc, *, tm):
    i = pl.program_id(1)
    last = pl.num_programs(1) - 1
    buf_sc[0:CONV_HALO, :] = jnp.where(i > 0, up_ref[0], 0.0)
    buf_sc[CONV_HALO:CONV_HALO + tm, :] = u_ref[0]
    buf_sc[CONV_HALO + tm:, :] = jnp.where(i < last, un_ref[0], 0.0)
    assert CONV_HALO - CONV_WIDTH // 2 == 1
    d = buf_sc.shape[1]
    for lb in range(d // CONV_LANES):
        ls = slice(lb * CONV_LANES, (lb + 1) * CONV_LANES)
        for rc in range(tm // CONV_ROWS):
            r0 = rc * CONV_ROWS
            acc = None
            for r in range(SUBLANES):
                part = None
                for j in range(r, CONV_WIDTH + 1, SUBLANES):
                    if j == 0:
                        continue
                    rows = slice(r0 + j - r, r0 + j - r + CONV_ROWS + SUBLANES)
                    term = buf_sc[rows, ls] * wdw_ref[j - 1:j, ls]
                    part = term if part is None else part + term
                part = part[r:r + CONV_ROWS]
                acc = part if acc is None else acc + part
            cv_sc[r0:r0 + CONV_ROWS, ls] = acc
    u = cv_sc[...] + bdw_ref[...]
    mu = jnp.mean(u, axis=-1, keepdims=True)
    uc = u - mu
    var = jnp.mean(uc * uc, axis=-1, keepdims=True)
    y = uc * lax.rsqrt(var + EPS) * lg_ref[...] + lb_ref[...]
    y = (y * jax.nn.sigmoid(y)).astype(BF)
    z = jnp.dot(y, w2_ref[...], preferred_element_type=F32) + b2_ref[...]
    o_ref[0] = x_ref[0] + mod_ref[0, 2:3, :] * z


def _conv_tail(u, wdw, bdw, lg, lb, w2, b2, x, mod, *, tm=256):
    bx, t, d = x.shape
    hb = tm // CONV_HALO
    n_halo = t // CONV_HALO
    vec = lambda: pl.BlockSpec((1, d), lambda bi, i: (0, 0))
    return pl.pallas_call(
        functools.partial(_conv_tail_kernel, tm=tm),
        grid=(bx, t // tm),
        in_specs=[
            pl.BlockSpec((1, tm, d), lambda bi, i: (bi, i, 0)),
            pl.BlockSpec((1, CONV_HALO, d), lambda bi, i: (bi, jnp.maximum(i * hb - 1, 0), 0)),
            pl.BlockSpec((1, CONV_HALO, d), lambda bi, i: (bi, jnp.minimum((i + 1) * hb, n_halo - 1), 0)),
            pl.BlockSpec(wdw.shape, lambda bi, i: (0, 0)),
            vec(), vec(), vec(),
            pl.BlockSpec(w2.shape, lambda bi, i: (0, 0)),
            vec(),
            pl.BlockSpec((1, tm, d), lambda bi, i: (bi, i, 0)),
            pl.BlockSpec((1, N_MOD, d), _mod_map(mod.shape[0])),
        ],
        out_specs=pl.BlockSpec((1, tm, d), lambda bi, i: (bi, i, 0)),
        out_shape=jax.ShapeDtypeStruct((bx, t, d), F32),
        scratch_shapes=[pltpu.VMEM((tm + 2 * CONV_HALO, d), F32), pltpu.VMEM((tm, d), F32)],
        compiler_params=_cparams(("parallel", "parallel"), 48),
        name="conv_tail",
    )(u, u, u, wdw, bdw, lg, lb, w2, b2, x, mod)


def _ft_tables(n_seq, gw):
    nb = n_seq // FT_NA
    m = np.arange(gw)
    ph = 2 * np.pi * ((m[:, None] * m[None, :]) % gw) / gw
    t0 = np.concatenate([np.cos(ph), -np.sin(ph)], axis=1) / np.sqrt(gw)
    ka = np.arange(FT_NA)[None, :, None]
    na = np.arange(FT_NA)[None, None, :]
    jb = np.arange(nb)[:, None, None]
    ph = 2 * np.pi * ((ka * (nb * na + jb)) % n_seq) / n_seq
    cw, sw = np.cos(ph), np.sin(ph)
    t1 = np.concatenate([np.concatenate([cw, sw], axis=2), np.concatenate([-sw, cw], axis=2)], axis=1)
    kb = np.arange(nb)
    ph = 2 * np.pi * ((kb[:, None] * kb[None, :]) % nb) / nb
    t3 = np.concatenate([np.cos(ph), np.sin(ph)], axis=1) / np.sqrt(n_seq)
    return tuple(jnp.asarray(t, F32).astype(BF) for t in (t0, t1, t3))


def _ft_width_kernel(x_ref, mod_ref, g1_ref, t0_ref, w_ref):
    h = _norm_mod(x_ref[0], g1_ref[...], mod_ref[0, 0:1, :], mod_ref[0, 1:2, :]).astype(BF)
    d = h.shape[1]
    gw = d // FT_GROUPS
    for g in range(FT_GROUPS):
        wg = jnp.dot(h[:, g * gw:(g + 1) * gw], t0_ref[...], preferred_element_type=F32).astype(BF)
        w_ref[0, :, g * gw:(g + 1) * gw] = wg[:, :gw]
        w_ref[0, :, d + g * gw:d + (g + 1) * gw] = wg[:, gw:]


def _ft_width(x, mod, g1, t0, *, tm=512):
    bx, t, d = x.shape
    return pl.pallas_call(
        _ft_width_kernel,
        grid=(bx, t // tm),
        in_specs=[
            pl.BlockSpec((1, tm, d), lambda bi, i: (bi, i, 0)),
            pl.BlockSpec((1, N_MOD, d), _mod_map(mod.shape[0])),
            pl.BlockSpec((1, d), lambda bi, i: (0, 0)),
            pl.BlockSpec(t0.shape, lambda bi, i: (0, 0)),
        ],
        out_specs=pl.BlockSpec((1, tm, 2 * d), lambda bi, i: (bi, i, 0)),
        out_shape=jax.ShapeDtypeStruct((bx, t, 2 * d), BF),
        compiler_params=_cparams(("parallel", "parallel"), 48),
        name="ft_width",
    )(x, mod, g1, t0)


def _ft_seq1_kernel(w_ref, t1_ref, y_ref, *, tb, d):
    for j in range(tb):
        blk = w_ref[0, :, j, :]
        rhs = jnp.concatenate([blk[:, :d], blk[:, d:]], axis=0)
        y = jnp.dot(t1_ref[j], rhs, preferred_element_type=F32).astype(BF)
        y_ref[0, :, j, :d] = y[:FT_NA]
        y_ref[0, :, j, d:] = y[FT_NA:]


def _ft_seq1(ww, t1, *, tb=16):
    bx, t, d2 = ww.shape
    nb = t // FT_NA
    wv = ww.reshape(bx, FT_NA, nb, d2)
    return pl.pallas_call(
        functools.partial(_ft_seq1_kernel, tb=tb, d=d2 // 2),
        grid=(bx, nb // tb),
        in_specs=[
            pl.BlockSpec((1, FT_NA, tb, d2), lambda bi, i: (bi, 0, i, 0)),
            pl.BlockSpec((tb, 2 * FT_NA, 2 * FT_NA), lambda bi, i: (i, 0, 0)),
        ],
        out_specs=pl.BlockSpec((1, FT_NA, tb, d2), lambda bi, i: (bi, 0, i, 0)),
        out_shape=jax.ShapeDtypeStruct((bx, FT_NA, nb, d2), BF),
        compiler_params=_cparams(("parallel", "parallel"), 48),
        name="ft_seq1",
    )(wv, t1)


def _ft_seq2_kernel(y_ref, t3_ref, w_ref, b_ref, x_ref, mod_ref, o_ref, *, ta):
    d = w_ref.shape[0]
    for a in range(ta):
        yb = y_ref[0, a]
        rhs = jnp.concatenate([yb[:, :d], yb[:, d:]], axis=0)
        z = jnp.dot(t3_ref[...], rhs, preferred_element_type=F32).astype(BF)
        yl = jnp.dot(z, w_ref[...], preferred_element_type=F32) + b_ref[...]
        o_ref[0, :, a, :] = x_ref[0, :, a, :] + mod_ref[0, 2:3, :] * yl


def _ft_seq2(y, t3, w, b, x, mod, *, ta=8):
    bx, t, d = x.shape
    nb = t // FT_NA
    xv = x.reshape(bx, nb, FT_NA, d)
    out = pl.pallas_call(
        functools.partial(_ft_seq2_kernel, ta=ta),
        grid=(bx, FT_NA // ta),
        in_specs=[
            pl.BlockSpec((1, ta, nb, 2 * d), lambda bi, i: (bi, i, 0, 0)),
            pl.BlockSpec(t3.shape, lambda bi, i: (0, 0)),
            pl.BlockSpec(w.shape, lambda bi, i: (0, 0)),
            pl.BlockSpec((1, d), lambda bi, i: (0, 0)),
            pl.BlockSpec((1, nb, ta, d), lambda bi, i: (bi, 0, i, 0)),
            pl.BlockSpec((1, N_MOD, d), _mod_map(mod.shape[0])),
        ],
        out_specs=pl.BlockSpec((1, nb, ta, d), lambda bi, i: (bi, 0, i, 0)),
        out_shape=jax.ShapeDtypeStruct((bx, nb, FT_NA, d), F32),
        compiler_params=_cparams(("parallel", "parallel"), 48),
        name="ft_seq2_mix",
    )(y, t3, w, b, xv, mod)
    return out.reshape(bx, t, d)


def _rope_tables(n_tok):
    t = jnp.arange(n_tok)
    row = (t // GRID_W).astype(F32)
    col = (t % GRID_W).astype(F32)
    n_axis = HEAD_DIM // 4
    inv = ROPE_THETA ** (-jnp.arange(n_axis, dtype=F32) / n_axis)
    ang = jnp.concatenate([row[:, None] * inv, col[:, None] * inv], axis=-1)
    ang = jnp.tile(jnp.repeat(ang, 2, axis=-1), (1, LANES // HEAD_DIM))
    sign = jnp.where(jnp.arange(LANES) % 2 == 0, -1.0, 1.0).astype(F32)
    return jnp.cos(ang), jnp.sin(ang) * sign


def _row(v):
    return v.reshape(1, -1)


def kernel(x, c, ctx, c_ctx, ada_w, ada_b, norm1_g, norm2_g, mlp_w1, mlp_w2, final_g, at_w_qkv, at_q_g, at_k_g, at_w_o, na_w_qkv, na_rpb, na_w_o, cv_w_pw1, cv_b_pw1, cv_w_dw, cv_b_dw, cv_ln_g, cv_ln_b, cv_w_pw2, cv_b_pw2, ft_w, ft_b):
    bsz, n_lat, d = x.shape
    depth = ada_w.shape[0]
    n_rows = n_lat // GRID_W
    n_ctx = ctx.shape[1]

    vec8 = jnp.zeros((8, d), F32).at[:bsz].set(c).at[bsz].set(c_ctx)
    mods = _ada_mods(vec8, ada_w, ada_b)

    hm = jnp.asarray(np.kron(np.eye(AT_HEADS), np.full((HEAD_DIM, HEAD_DIM), 1.0 / HEAD_DIM)), BF)
    h_ctx = ctx
    for i in range(depth):
        kind = i % N_MIXERS
        occ = i // N_MIXERS
        ctx_later = any((j % N_MIXERS) in (0, 1) for j in range(i + 1, depth))
        ml = mods[i, :bsz].reshape(bsz, N_MOD, d)
        mc = mods[i, bsz:bsz + 1].reshape(1, N_MOD, d)
        g1 = _row(norm1_g[i])
        if kind == 0:
            w = at_w_qkv[occ].astype(BF)
            wo = at_w_o[occ].astype(BF)
            qg = _row(jnp.tile(at_q_g[occ], AT_HEADS))
            kg = _row(jnp.tile(at_k_g[occ], AT_KV_HEADS))
            cos, sin = _rope_tables(n_lat)
            qt_l, k_l, vt_l = _gqa_proj(x, ml, g1, w, hm, qg, kg, cos, sin)
            qt_c, k_c, vt_c = _gqa_proj(h_ctx, mc, g1, w, hm, qg, kg, None, None)
            k_all = jnp.concatenate([k_c, k_l], axis=2)
            vt_all = jnp.concatenate([vt_c, vt_l], axis=2)
            ot_l = _flash(qt_l, k_all, vt_all, tq=512, tk=768)
            x = _oproj(ot_l, wo, x, ml, tm=512)
            if ctx_later:
                ot_c = _flash(qt_c, k_c, vt_c, tq=n_ctx, tk=n_ctx)
                h_ctx = _oproj(ot_c, wo, h_ctx, mc, tm=n_ctx)
        elif kind == 1:
            w = na_w_qkv[occ].astype(BF)
            wo = na_w_o[occ].astype(BF)
            qt_l, k_l, v_l = _na_proj(x, ml, g1, w)
            qt_c, k_c, v_c = _na_proj(h_ctx, mc, g1, w)
            bias = _na_bias_tables(na_rpb[occ], n_rows)
            ot_l = _na_attend(qt_l, k_l, v_l, k_c, v_c, bias, n_rows)
            x = _oproj(ot_l, wo, x, ml, tm=512)
            if ctx_later:
                raise NotImplementedError("context output of a neighbourhood layer is not needed at this depth")
        elif kind == 2:
            w1 = cv_w_pw1[occ].astype(BF)
            w2 = cv_w_pw2[occ].astype(BF)
            wdw = jnp.zeros((CONV_WIDTH + 1, d), F32).at[:CONV_WIDTH].set(cv_w_dw[occ])
            cv = (wdw, _row(cv_b_dw[occ]), _row(cv_ln_g[occ]), _row(cv_ln_b[occ]), w2, _row(cv_b_pw2[occ]))
            u = _conv_pw1(x, ml, g1, w1, _row(cv_b_pw1[occ]))
            x = _conv_tail(u, *cv, x, ml)
            if ctx_later:
                raise NotImplementedError("context output of a convolution layer is not needed at this depth")
        else:
            t0, t1, t3 = _ft_tables(n_lat, d // FT_GROUPS)
            fw = ft_w[occ].astype(BF)
            ww = _ft_width(x, ml, g1, t0)
            x = _ft_seq2(_ft_seq1(ww, t1), t3, fw, _row(ft_b[occ]), x, ml)
            if ctx_later:
                raise NotImplementedError("context output of a Fourier layer is not needed at this depth")
        g2 = _row(norm2_g[i])
        w1m = mlp_w1[i].astype(BF)
        w2m = mlp_w2[i].astype(BF)
        is_last = i == depth - 1
        x = _mlp(x, ml, g2, w1m, w2m, _row(final_g) if is_last else None, tm=1024)
        if ctx_later:
            h_ctx = _mlp(h_ctx, mc, g2, w1m, w2m, tm=n_ctx)
    return x
```

```python
import functools

import jax
import jax.numpy as jnp
import numpy as np
from jax import lax
from jax.experimental import pallas as pl
from jax.experimental.pallas import tpu as pltpu

GRID_W = 64
N_MIXERS = 4
HEAD_DIM = 64
AT_HEADS = 16
AT_KV_HEADS = 4
AT_WIDTH = AT_HEADS * HEAD_DIM
AT_KV_WIDTH = AT_KV_HEADS * HEAD_DIM
ROPE_THETA = 10000.0
NA_HEADS = 16
NA_WIDTH = NA_HEADS * HEAD_DIM
NA_WIN_ROWS = 8
NA_WIN_COLS = 16
CONV_WIDTH = 31
FT_GROUPS = 4
N_MOD = 6
EPS = 1e-6
QK_SCALE = HEAD_DIM ** -0.5
LOG2E = float(np.log2(np.e))

LANES = 128
BF16_ROWS = 16
V_ROWS = HEAD_DIM + BF16_ROWS
NA_QROWS = 4
NA_KROWS = 12
NA_EROWS = NA_KROWS + 2 * NA_QROWS
CONV_HALO = 16
CONV_ROWS = 64
CONV_LANES = 256
SUBLANES = 8
FT_NA = 64
NEG_BIG = -1e30

BF = jnp.bfloat16
F32 = jnp.float32


def _cparams(sem, vmem_mib):
    return pltpu.CompilerParams(dimension_semantics=sem, vmem_limit_bytes=vmem_mib << 20)


def _norm_mod(x, g, shift, scale):
    ms = jnp.mean(x * x, axis=-1, keepdims=True)
    return (x * lax.rsqrt(ms + EPS) * g) * (1.0 + scale) + shift


def _ada_kernel(v_ref, w_ref, b_ref, o_ref):
    v = v_ref[...]
    sv = v * jax.nn.sigmoid(v)
    o_ref[0] = jnp.dot(sv, w_ref[0], preferred_element_type=F32, precision=lax.Precision.HIGHEST) + b_ref[0]


def _ada_mods(vec8, ada_w, ada_b):
    depth, d, n = ada_w.shape
    tn = n // 4
    return pl.pallas_call(
        _ada_kernel,
        grid=(depth, n // tn),
        in_specs=[
            pl.BlockSpec((8, d), lambda l, j: (0, 0)),
            pl.BlockSpec((1, d, tn), lambda l, j: (l, 0, j)),
            pl.BlockSpec((1, 1, tn), lambda l, j: (l, 0, j)),
        ],
        out_specs=pl.BlockSpec((1, 8, tn), lambda l, j: (l, 0, j)),
        out_shape=jax.ShapeDtypeStruct((depth, 8, n), F32),
        compiler_params=_cparams(("arbitrary", "arbitrary"), 40),
        name="ada_mods",
    )(vec8, ada_w, ada_b.reshape(depth, 1, n))


def _mod_map(n_mod_rows):
    if n_mod_rows == 1:
        return lambda b, *_: (0, 0, 0)
    return lambda b, *_: (b, 0, 0)


def _gqa_proj_kernel(*refs, rope, tm):
    if rope:
        x_ref, mod_ref, g1_ref, w_ref, hm_ref, qg_ref, kg_ref, cos_ref, sin_ref, qt_ref, k_ref, vt_ref = refs
    else:
        x_ref, mod_ref, g1_ref, w_ref, hm_ref, qg_ref, kg_ref, qt_ref, k_ref, vt_ref = refs
    h = _norm_mod(x_ref[0], g1_ref[...], mod_ref[0, 0:1, :], mod_ref[0, 1:2, :]).astype(BF)
    qkv = jnp.dot(h, w_ref[...], preferred_element_type=F32)
    q = qkv[:, :AT_WIDTH]
    k = qkv[:, AT_WIDTH:AT_WIDTH + AT_KV_WIDTH]
    v = qkv[:, AT_WIDTH + AT_KV_WIDTH:]
    hm = hm_ref[...]
    q = q * lax.rsqrt(jnp.dot((q * q).astype(BF), hm, preferred_element_type=F32) + EPS) * qg_ref[...]
    hk = hm[:AT_KV_WIDTH, :AT_KV_WIDTH]
    k = k * lax.rsqrt(jnp.dot((k * k).astype(BF), hk, preferred_element_type=F32) + EPS) * kg_ref[...]
    if rope:
        cos = cos_ref[...]
        sin = sin_ref[...]
        even = (lax.broadcasted_iota(jnp.int32, (tm, LANES), 1) % 2) == 0

        def rot(t):
            partner = jnp.where(even, pltpu.roll(t, LANES - 1, 1), pltpu.roll(t, 1, 1))
            return t * cos + partner * sin
    else:
        def rot(t):
            return t
    for c in range(AT_WIDTH // LANES):
        qc = rot(q[:, c * LANES:(c + 1) * LANES]) * (QK_SCALE * LOG2E)
        qt_ref[0, c * LANES:(c + 1) * LANES, :] = qc.T.astype(BF)
    for c in range(AT_KV_WIDTH // LANES):
        kc = rot(k[:, c * LANES:(c + 1) * LANES]).astype(BF)
        k_ref[0, 2 * c] = kc[:, :HEAD_DIM]
        k_ref[0, 2 * c + 1] = kc[:, HEAD_DIM:]
    vt = v.T.astype(BF)
    ones = (lax.broadcasted_iota(jnp.int32, (BF16_ROWS, tm), 0) == 0).astype(BF)
    for h in range(AT_KV_HEADS):
        vt_ref[0, h, :HEAD_DIM, :] = vt[h * HEAD_DIM:(h + 1) * HEAD_DIM]
        vt_ref[0, h, HEAD_DIM:, :] = ones


def _gqa_proj(x, mod, g1, w, hm, qg, kg, cos, sin, *, tm=256):
    bx, t, d = x.shape
    rope = cos is not None
    n = w.shape[1]
    const = lambda b, i: (0, 0)
    in_specs = [
        pl.BlockSpec((1, tm, d), lambda b, i: (b, i, 0)),
        pl.BlockSpec((1, N_MOD, d), _mod_map(mod.shape[0])),
        pl.BlockSpec((1, d), const),
        pl.BlockSpec((d, n), const),
        pl.BlockSpec(hm.shape, const),
        pl.BlockSpec((1, AT_WIDTH), const),
        pl.BlockSpec((1, AT_KV_WIDTH), const),
    ]
    args = [x, mod, g1, w, hm, qg, kg]
    if rope:
        in_specs += [pl.BlockSpec((tm, LANES), lambda b, i: (i, 0))] * 2
        args += [cos, sin]
    return pl.pallas_call(
        functools.partial(_gqa_proj_kernel, rope=rope, tm=tm),
        grid=(bx, t // tm),
        in_specs=in_specs,
        out_specs=[
            pl.BlockSpec((1, AT_WIDTH, tm), lambda b, i: (b, 0, i)),
            pl.BlockSpec((1, AT_KV_HEADS, tm, HEAD_DIM), lambda b, i: (b, 0, i, 0)),
            pl.BlockSpec((1, AT_KV_HEADS, V_ROWS, tm), lambda b, i: (b, 0, 0, i)),
        ],
        out_shape=[
            jax.ShapeDtypeStruct((bx, AT_WIDTH, t), BF),
            jax.ShapeDtypeStruct((bx, AT_KV_HEADS, t, HEAD_DIM), BF),
            jax.ShapeDtypeStruct((bx, AT_KV_HEADS, V_ROWS, t), BF),
        ],
        compiler_params=_cparams(("parallel", "parallel"), 48),
        name="gqa_proj_rope" if rope else "gqa_proj_ctx",
    )(*args)


def _flash_kernel(*refs, tk, lookahead, with_latents):
    if with_latents:
        qt_ref, kc_ref, vtc_ref, kl_ref, vtl_ref, ot_ref, m_sc, acc_sc = refs
        n_lat = kl_ref.shape[2] // tk
    else:
        qt_ref, kc_ref, vtc_ref, ot_ref, m_sc, acc_sc = refs
        n_lat = 0
    q = qt_ref[0]
    chunks = [(kc_ref.at[0, 0], vtc_ref.at[0, 0])]
    chunks += [(kl_ref.at[0, 0, c * tk:(c + 1) * tk, :], vtl_ref.at[0, 0, :, c * tk:(c + 1) * tk]) for c in range(n_lat)]

    def scores(c):
        return jnp.dot(chunks[c][0][...], q, preferred_element_type=F32)

    def weighted(c, s, m):
        return jnp.dot(chunks[c][1][...], jnp.exp2(s - m).astype(BF), preferred_element_type=F32)

    pending = [scores(c) for c in range(min(lookahead, len(chunks)))]
    m0 = jnp.max(pending[0], axis=0, keepdims=True)
    acc = None
    for c in range(len(chunks)):
        if c + lookahead < len(chunks):
            pending.append(scores(c + lookahead))
        pv = weighted(c, pending.pop(0), m0)
        acc = pv if acc is None else acc + pv
    denom = acc[HEAD_DIM:HEAD_DIM + 1]
    overflowed = jnp.max(jnp.where(jnp.isfinite(denom), 0.0, 1.0)) > 0.0

    @pl.when(jnp.logical_not(overflowed))
    def _():
        ot_ref[0] = (acc[:HEAD_DIM] * (1.0 / denom)).astype(BF)

    @pl.when(overflowed)
    def _():
        m_sc[...] = jnp.full(m_sc.shape, NEG_BIG, F32)
        acc_sc[...] = jnp.zeros(acc_sc.shape, F32)

        def update(k, vt):
            s = jnp.dot(k, q, preferred_element_type=F32)
            m_prev = m_sc[...]
            m_new = jnp.maximum(m_prev, jnp.max(s, axis=0, keepdims=True))
            pv = jnp.dot(vt, jnp.exp2(s - m_new).astype(BF), preferred_element_type=F32)
            acc_sc[...] = jnp.exp2(m_prev - m_new) * acc_sc[...] + pv
            m_sc[...] = m_new

        update(kc_ref[0, 0], vtc_ref[0, 0])

        def body(c, carry):
            off = pl.multiple_of(c * tk, tk)
            update(kl_ref[0, 0, pl.ds(off, tk), :], vtl_ref[0, 0, :, pl.ds(off, tk)])
            return carry

        if with_latents:
            lax.fori_loop(0, n_lat, body, 0)
        a = acc_sc[...]
        ot_ref[0] = (a[:HEAD_DIM] * (1.0 / a[HEAD_DIM:HEAD_DIM + 1])).astype(BF)


def _flash(qt, kc, vtc, kl=None, vtl=None, *, tq, tk=256, lookahead=2):
    b, w, t = qt.shape
    kv = kc.shape[1]
    heads = w // HEAD_DIM
    group = heads // kv
    kv_map = lambda bi, h, i: (bi, h // group, 0, 0)
    in_specs = [
        pl.BlockSpec((1, HEAD_DIM, tq), lambda bi, h, i: (bi, h, i)),
        pl.BlockSpec((1, 1) + kc.shape[2:], kv_map),
        pl.BlockSpec((1, 1) + vtc.shape[2:], kv_map),
    ]
    args = [qt, kc, vtc]
    if kl is not None:
        in_specs += [pl.BlockSpec((1, 1) + kl.shape[2:], kv_map), pl.BlockSpec((1, 1) + vtl.shape[2:], kv_map)]
        args += [kl, vtl]
    return pl.pallas_call(
        functools.partial(_flash_kernel, tk=tk, lookahead=lookahead, with_latents=kl is not None),
        grid=(b, heads, t // tq),
        in_specs=in_specs,
        out_specs=pl.BlockSpec((1, HEAD_DIM, tq), lambda bi, h, i: (bi, h, i)),
        out_shape=jax.ShapeDtypeStruct((b, w, t), BF),
        scratch_shapes=[pltpu.VMEM((1, tq), F32), pltpu.VMEM((V_ROWS, tq), F32)],
        compiler_params=_cparams(("parallel", "parallel", "parallel"), 48),
        name="gqa_flash" if kl is not None else "gqa_flash_ctx",
    )(*args)


def _oproj_kernel(ot_ref, w_ref, x_ref, mod_ref, o_ref):
    y = lax.dot_general(ot_ref[0], w_ref[...], (((0,), (0,)), ((), ())), preferred_element_type=F32)
    o_ref[0] = x_ref[0] + mod_ref[0, 2:3, :] * y


def _oproj(ot, w, x, mod, *, tm):
    b, t, d = x.shape
    kw = ot.shape[1]
    return pl.pallas_call(
        _oproj_kernel,
        grid=(b, t // tm),
        in_specs=[
            pl.BlockSpec((1, kw, tm), lambda bi, i: (bi, 0, i)),
            pl.BlockSpec((kw, d), lambda bi, i: (0, 0)),
            pl.BlockSpec((1, tm, d), lambda bi, i: (bi, i, 0)),
            pl.BlockSpec((1, N_MOD, d), _mod_map(mod.shape[0])),
        ],
        out_specs=pl.BlockSpec((1, tm, d), lambda bi, i: (bi, i, 0)),
        out_shape=jax.ShapeDtypeStruct((b, t, d), F32),
        compiler_params=_cparams(("parallel", "parallel"), 48),
        name="attn_oproj",
    )(ot, w, x, mod)


def _mlp_kernel(*refs, final):
    if final:
        x_ref, mod_ref, g2_ref, w1_ref, w2_ref, fg_ref, o_ref, h_sc, acc_sc = refs
    else:
        x_ref, mod_ref, g2_ref, w1_ref, w2_ref, o_ref, h_sc, acc_sc = refs
    j = pl.program_id(2)

    @pl.when(j == 0)
    def _():
        h_sc[...] = _norm_mod(x_ref[0], g2_ref[...], mod_ref[0, 3:4, :], mod_ref[0, 4:5, :]).astype(BF)
        acc_sc[...] = jnp.zeros(acc_sc.shape, F32)

    a = jnp.dot(h_sc[...], w1_ref[...], preferred_element_type=F32)
    a = jnp.square(jnp.maximum(a, 0.0)).astype(BF)
    acc_sc[...] += jnp.dot(a, w2_ref[...], preferred_element_type=F32)

    @pl.when(j == pl.num_programs(2) - 1)
    def _():
        y = x_ref[0] + mod_ref[0, 5:6, :] * acc_sc[...]
        if final:
            ms = jnp.mean(y * y, axis=-1, keepdims=True)
            y = y * lax.rsqrt(ms + EPS) * fg_ref[...]
        o_ref[0] = y


def _mlp(x, mod, g2, w1, w2, final_g=None, *, tm, tf=512):
    b, t, d = x.shape
    f = w1.shape[1]
    final = final_g is not None
    in_specs = [
        pl.BlockSpec((1, tm, d), lambda bi, i, j: (bi, i, 0)),
        pl.BlockSpec((1, N_MOD, d), _mod_map(mod.shape[0])),
        pl.BlockSpec((1, d), lambda bi, i, j: (0, 0)),
        pl.BlockSpec((d, tf), lambda bi, i, j: (0, j)),
        pl.BlockSpec((tf, d), lambda bi, i, j: (j, 0)),
    ]
    args = [x, mod, g2, w1, w2]
    if final:
        in_specs.append(pl.BlockSpec((1, d), lambda bi, i, j: (0, 0)))
        args.append(final_g)
    return pl.pallas_call(
        functools.partial(_mlp_kernel, final=final),
        grid=(b, t // tm, f // tf),
        in_specs=in_specs,
        out_specs=pl.BlockSpec((1, tm, d), lambda bi, i, j: (bi, i, 0)),
        out_shape=jax.ShapeDtypeStruct((b, t, d), F32),
        scratch_shapes=[pltpu.VMEM((tm, d), BF), pltpu.VMEM((tm, d), F32)],
        compiler_params=_cparams(("parallel", "parallel", "arbitrary"), 52),
        name="mlp_final" if final else "mlp",
    )(*args)


def _na_proj_kernel(x_ref, mod_ref, g1_ref, w_ref, qt_ref, k_ref, v_ref):
    h = _norm_mod(x_ref[0], g1_ref[...], mod_ref[0, 0:1, :], mod_ref[0, 1:2, :]).astype(BF)
    qkv = jnp.dot(h, w_ref[...], preferred_element_type=F32)
    for c in range(NA_WIDTH // LANES):
        qt_ref[0, c * LANES:(c + 1) * LANES, :] = (qkv[:, c * LANES:(c + 1) * LANES] * (QK_SCALE * LOG2E)).T.astype(BF)
    k_ref[0] = qkv[:, NA_WIDTH:2 * NA_WIDTH].astype(BF)
    v_ref[0] = qkv[:, 2 * NA_WIDTH:].astype(BF)


def _na_proj(x, mod, g1, w, *, tm=256):
    bx, t, d = x.shape
    return pl.pallas_call(
        _na_proj_kernel,
        grid=(bx, t // tm),
        in_specs=[
            pl.BlockSpec((1, tm, d), lambda b, i: (b, i, 0)),
            pl.BlockSpec((1, N_MOD, d), _mod_map(mod.shape[0])),
            pl.BlockSpec((1, d), lambda b, i: (0, 0)),
            pl.BlockSpec(w.shape, lambda b, i: (0, 0)),
        ],
        out_specs=[
            pl.BlockSpec((1, NA_WIDTH, tm), lambda b, i: (b, 0, i)),
            pl.BlockSpec((1, tm, NA_WIDTH), lambda b, i: (b, i, 0)),
            pl.BlockSpec((1, tm, NA_WIDTH), lambda b, i: (b, i, 0)),
        ],
        out_shape=[
            jax.ShapeDtypeStruct((bx, NA_WIDTH, t), BF),
            jax.ShapeDtypeStruct((bx, t, NA_WIDTH), BF),
            jax.ShapeDtypeStruct((bx, t, NA_WIDTH), BF),
        ],
        compiler_params=_cparams(("parallel", "parallel"), 48),
        name="na_proj",
    )(x, mod, g1, w)


def _na_key_base(g, n_groups):
    return jnp.clip(NA_QROWS * g - NA_WIN_ROWS // 2, 0, NA_QROWS * n_groups - NA_KROWS)


def _na_kernel(qt_ref, k_ref, v_ref, kc_ref, vc_ref, quad_ref, mask_ref, ot_ref, *, n_groups):
    g = pl.program_id(2)
    kb_row = _na_key_base(g, n_groups)
    e0 = kb_row - NA_QROWS * g + NA_WIN_ROWS
    kb = pl.multiple_of(kb_row * GRID_W, GRID_W)
    kw = k_ref[0, pl.ds(kb, NA_KROWS * GRID_W), :]
    vw = v_ref[0, pl.ds(kb, NA_KROWS * GRID_W), :]
    kc = kc_ref[0]
    vc = vc_ref[0]
    q2 = qt_ref[0]
    upper = lax.broadcasted_iota(jnp.int32, q2.shape, 0) < HEAD_DIM
    tn = (((0,), (0,)), ((), ()))
    scores = []
    for half in range(2):
        qh = jnp.where(upper if half == 0 else jnp.logical_not(upper), q2, jnp.zeros_like(q2))
        bias = jnp.concatenate([quad_ref[half, e0 + i] for i in range(NA_KROWS)], axis=0) + mask_ref[0]
        scores.append((jnp.dot(kw, qh, preferred_element_type=F32) + bias,
                       jnp.dot(kc, qh, preferred_element_type=F32)))
    outs = []
    for s_nb, s_cx in scores:
        m = jnp.maximum(jnp.max(s_nb, axis=0, keepdims=True), jnp.max(s_cx, axis=0, keepdims=True))
        p_nb = jnp.exp2(s_nb - m)
        p_cx = jnp.exp2(s_cx - m)
        l = jnp.sum(p_nb, axis=0, keepdims=True) + jnp.sum(p_cx, axis=0, keepdims=True)
        o = (lax.dot_general(vw, p_nb.astype(BF), tn, preferred_element_type=F32)
             + lax.dot_general(vc, p_cx.astype(BF), tn, preferred_element_type=F32))
        outs.append(o * (1.0 / l))
    ot_ref[0] = jnp.where(upper, outs[0], outs[1]).astype(BF)


def _na_bias_tables(rpb, n_rows):
    n_groups = n_rows // NA_QROWS
    n_h, n_dr, n_dc = rpb.shape
    m = n_dc + 2 * GRID_W
    c0 = GRID_W + NA_WIN_COLS - 1
    flat = jnp.tile(jnp.pad(rpb, ((0, 0), (0, 0), (GRID_W, GRID_W))), (1, 1, GRID_W + 1))
    t = flat[:, :, c0:c0 + GRID_W * (m - 1)].reshape(n_h, n_dr, GRID_W, m - 1)[..., :GRID_W]
    qc = np.arange(GRID_W)[None, :]
    kc = np.arange(GRID_W)[:, None]
    cs = np.clip(qc - NA_WIN_COLS // 2, 0, GRID_W - NA_WIN_COLS)
    col_ok = (kc >= cs) & (kc < cs + NA_WIN_COLS)
    tc = jnp.where(col_ok, jnp.swapaxes(t, -1, -2) * LOG2E, NEG_BIG)
    back = NA_EROWS - 1 - n_dr
    tcp = jnp.pad(tc, ((0, 0), (NA_QROWS, back), (0, 0), (0, 0)))
    quad = jnp.concatenate([tcp[:, NA_QROWS - 1 - a:NA_QROWS - 1 - a + NA_EROWS] for a in range(NA_QROWS)], axis=-1)
    masks = []
    for g in (0, 1, n_groups - 1):
        r = NA_QROWS * g + np.arange(NA_QROWS)[None, None, :, None]
        kb = int(np.clip(NA_QROWS * g - NA_WIN_ROWS // 2, 0, n_rows - NA_KROWS))
        krow = kb + np.arange(NA_KROWS)[:, None, None, None]
        rs = np.clip(r - NA_WIN_ROWS // 2, 0, n_rows - NA_WIN_ROWS)
        ok = np.broadcast_to((krow >= rs) & (krow < rs + NA_WIN_ROWS), (NA_KROWS, GRID_W, NA_QROWS, GRID_W))
        masks.append(np.where(ok, 0.0, NEG_BIG).reshape(NA_KROWS * GRID_W, NA_QROWS * GRID_W))
    return quad.astype(F32), jnp.asarray(np.stack(masks), F32)


def _na_attend(qt, k, v, kc, vc, quad, rowmask, n_rows):
    b, w, s = qt.shape
    c = kc.shape[1]
    n_groups = n_rows // NA_QROWS
    tq = NA_QROWS * GRID_W
    pair = 2 * HEAD_DIM

    def mask_map(bi, hp, g):
        return (jnp.where(g == 0, 0, jnp.where(g == n_groups - 1, 2, 1)), 0, 0)

    return pl.pallas_call(
        functools.partial(_na_kernel, n_groups=n_groups),
        grid=(b, w // pair, n_groups),
        in_specs=[
            pl.BlockSpec((1, pair, tq), lambda bi, hp, g: (bi, hp, g)),
            pl.BlockSpec((1, s, pair), lambda bi, hp, g: (bi, 0, hp)),
            pl.BlockSpec((1, s, pair), lambda bi, hp, g: (bi, 0, hp)),
            pl.BlockSpec((1, c, pair), lambda bi, hp, g: (bi, 0, hp)),
            pl.BlockSpec((1, c, pair), lambda bi, hp, g: (bi, 0, hp)),
            pl.BlockSpec((2, NA_EROWS, GRID_W, tq), lambda bi, hp, g: (hp, 0, 0, 0)),
            pl.BlockSpec((1, NA_KROWS * GRID_W, tq), mask_map),
        ],
        out_specs=pl.BlockSpec((1, pair, tq), lambda bi, hp, g: (bi, hp, g)),
        out_shape=jax.ShapeDtypeStruct((b, w, s), BF),
        compiler_params=_cparams(("parallel", "parallel", "arbitrary"), 48),
        name="na_attend",
    )(qt, k, v, kc, vc, quad, rowmask)


def _conv_pw1_kernel(x_ref, mod_ref, g1_ref, w_ref, b_ref, u_ref):
    h = _norm_mod(x_ref[0], g1_ref[...], mod_ref[0, 0:1, :], mod_ref[0, 1:2, :]).astype(BF)
    ag = jnp.dot(h, w_ref[...], preferred_element_type=F32) + b_ref[...]
    d = u_ref.shape[2]
    u_ref[0] = ag[:, :d] * jax.nn.sigmoid(ag[:, d:])


def _conv_pw1(x, mod, g1, w, b, *, tm=512):
    bx, t, d = x.shape
    return pl.pallas_call(
        _conv_pw1_kernel,
        grid=(bx, t // tm),
        in_specs=[
            pl.BlockSpec((1, tm, d), lambda bi, i: (bi, i, 0)),
            pl.BlockSpec((1, N_MOD, d), _mod_map(mod.shape[0])),
            pl.BlockSpec((1, d), lambda bi, i: (0, 0)),
            pl.BlockSpec(w.shape, lambda bi, i: (0, 0)),
            pl.BlockSpec((1, 2 * d), lambda bi, i: (0, 0)),
        ],
        out_specs=pl.BlockSpec((1, tm, d), lambda bi, i: (bi, i, 0)),
        out_shape=jax.ShapeDtypeStruct((bx, t, d), F32),
        compiler_params=_cparams(("parallel", "parallel"), 48),
        name="conv_pw1_glu",
    )(x, mod, g1, w, b)


def _conv_tail_kernel(u_ref, up_ref, un_ref, wdw_ref, bdw_ref, lg_ref, lb_ref, w2_ref, b2_ref, x_ref, mod_ref,
                      o_ref, buf_sc, cv_sc, *, tm):
    i = pl.program_id(1)
    last = pl.num_programs(1) - 1
    buf_sc[0:CONV_HALO, :] = jnp.where(i > 0, up_ref[0], 0.0)
    buf_sc[CONV_HALO:CONV_HALO + tm, :] = u_ref[0]
    buf_sc[CONV_HALO + tm:, :] = jnp.where(i < last, un_ref[0], 0.0)
    assert CONV_HALO - CONV_WIDTH // 2 == 1
    d = buf_sc.shape[1]
    for lb in range(d // CONV_LANES):
        ls = slice(lb * CONV_LANES, (lb + 1) * CONV_LANES)
        for rc in range(tm // CONV_ROWS):
            r0 = rc * CONV_ROWS
            acc = None
            for r in range(SUBLANES):
                part = None
                for j in range(r, CONV_WIDTH + 1, SUBLANES):
                    if j == 0:
                        continue
                    rows = slice(r0 + j - r, r0 + j - r + CONV_ROWS + SUBLANES)
                    term = buf_sc[rows, ls] * wdw_ref[j - 1:j, ls]
                    part = term if part is None else part + term
                part = part[r:r + CONV_ROWS]
                acc = part if acc is None else acc + part
            cv_sc[r0:r0 + CONV_ROWS, ls] = acc
    u = cv_sc[...] + bdw_ref[...]
    mu = jnp.mean(u, axis=-1, keepdims=True)
    uc = u - mu
    var = jnp.mean(uc * uc, axis=-1, keepdims=True)
    y = uc * lax.rsqrt(var + EPS) * lg_ref[...] + lb_ref[...]
    y = (y * jax.nn.sigmoid(y)).astype(BF)
    z = jnp.dot(y, w2_ref[...], preferred_element_type=F32) + b2_ref[...]
    o_ref[0] = x_ref[0] + mod_ref[0, 2:3, :] * z


def _conv_tail(u, wdw, bdw, lg, lb, w2, b2, x, mod, *, tm=256):
    bx, t, d = x.shape
    hb = tm // CONV_HALO
    n_halo = t // CONV_HALO
    vec = lambda: pl.BlockSpec((1, d), lambda bi, i: (0, 0))
    return pl.pallas_call(
        functools.partial(_conv_tail_kernel, tm=tm),
        grid=(bx, t // tm),
        in_specs=[
            pl.BlockSpec((1, tm, d), lambda bi, i: (bi, i, 0)),
            pl.BlockSpec((1, CONV_HALO, d), lambda bi, i: (bi, jnp.maximum(i * hb - 1, 0), 0)),
            pl.BlockSpec((1, CONV_HALO, d), lambda bi, i: (bi, jnp.minimum((i + 1) * hb, n_halo - 1), 0)),
            pl.BlockSpec(wdw.shape, lambda bi, i: (0, 0)),
            vec(), vec(), vec(),
            pl.BlockSpec(w2.shape, lambda bi, i: (0, 0)),
            vec(),
            pl.BlockSpec((1, tm, d), lambda bi, i: (bi, i, 0)),
            pl.BlockSpec((1, N_MOD, d), _mod_map(mod.shape[0])),
        ],
        out_specs=pl.BlockSpec((1, tm, d), lambda bi, i: (bi, i, 0)),
        out_shape=jax.ShapeDtypeStruct((bx, t, d), F32),
        scratch_shapes=[pltpu.VMEM((tm + 2 * CONV_HALO, d), F32), pltpu.VMEM((tm, d), F32)],
        compiler_params=_cparams(("parallel", "parallel"), 48),
        name="conv_tail",
    )(u, u, u, wdw, bdw, lg, lb, w2, b2, x, mod)


def _ft_tables(n_seq, gw):
    nb = n_seq // FT_NA
    m = np.arange(gw)
    ph = 2 * np.pi * ((m[:, None] * m[None, :]) % gw) / gw
    t0 = np.concatenate([np.cos(ph), -np.sin(ph)], axis=1) / np.sqrt(gw)
    ka = np.arange(FT_NA)[None, :, None]
    na = np.arange(FT_NA)[None, None, :]
    jb = np.arange(nb)[:, None, None]
    ph = 2 * np.pi * ((ka * (nb * na + jb)) % n_seq) / n_seq
    cw, sw = np.cos(ph), np.sin(ph)
    t1 = np.concatenate([np.concatenate([cw, sw], axis=2), np.concatenate([-sw, cw], axis=2)], axis=1)
    kb = np.arange(nb)
    ph = 2 * np.pi * ((kb[:, None] * kb[None, :]) % nb) / nb
    t3 = np.concatenate([np.cos(ph), np.sin(ph)], axis=1) / np.sqrt(n_seq)
    return tuple(jnp.asarray(t, F32).astype(BF) for t in (t0, t1, t3))


def _ft_width_kernel(x_ref, mod_ref, g1_ref, t0_ref, w_ref):
    h = _norm_mod(x_ref[0], g1_ref[...], mod_ref[0, 0:1, :], mod_ref[0, 1:2, :]).astype(BF)
    d = h.shape[1]
    gw = d // FT_GROUPS
    for g in range(FT_GROUPS):
        wg = jnp.dot(h[:, g * gw:(g + 1) * gw], t0_ref[...], preferred_element_type=F32).astype(BF)
        w_ref[0, :, g * gw:(g + 1) * gw] = wg[:, :gw]
        w_ref[0, :, d + g * gw:d + (g + 1) * gw] = wg[:, gw:]


def _ft_width(x, mod, g1, t0, *, tm=512):
    bx, t, d = x.shape
    return pl.pallas_call(
        _ft_width_kernel,
        grid=(bx, t // tm),
        in_specs=[
            pl.BlockSpec((1, tm, d), lambda bi, i: (bi, i, 0)),
            pl.BlockSpec((1, N_MOD, d), _mod_map(mod.shape[0])),
            pl.BlockSpec((1, d), lambda bi, i: (0, 0)),
            pl.BlockSpec(t0.shape, lambda bi, i: (0, 0)),
        ],
        out_specs=pl.BlockSpec((1, tm, 2 * d), lambda bi, i: (bi, i, 0)),
        out_shape=jax.ShapeDtypeStruct((bx, t, 2 * d), BF),
        compiler_params=_cparams(("parallel", "parallel"), 48),
        name="ft_width",
    )(x, mod, g1, t0)


def _ft_seq1_kernel(w_ref, t1_ref, y_ref, *, tb, d):
    for j in range(tb):
        blk = w_ref[0, :, j, :]
        rhs = jnp.concatenate([blk[:, :d], blk[:, d:]], axis=0)
        y = jnp.dot(t1_ref[j], rhs, preferred_element_type=F32).astype(BF)
        y_ref[0, :, j, :d] = y[:FT_NA]
        y_ref[0, :, j, d:] = y[FT_NA:]


def _ft_seq1(ww, t1, *, tb=16):
    bx, t, d2 = ww.shape
    nb = t // FT_NA
    wv = ww.reshape(bx, FT_NA, nb, d2)
    return pl.pallas_call(
        functools.partial(_ft_seq1_kernel, tb=tb, d=d2 // 2),
        grid=(bx, nb // tb),
        in_specs=[
            pl.BlockSpec((1, FT_NA, tb, d2), lambda bi, i: (bi, 0, i, 0)),
            pl.BlockSpec((tb, 2 * FT_NA, 2 * FT_NA), lambda bi, i: (i, 0, 0)),
        ],
        out_specs=pl.BlockSpec((1, FT_NA, tb, d2), lambda bi, i: (bi, 0, i, 0)),
        out_shape=jax.ShapeDtypeStruct((bx, FT_NA, nb, d2), BF),
        compiler_params=_cparams(("parallel", "parallel"), 48),
        name="ft_seq1",
    )(wv, t1)


def _ft_seq2_kernel(y_ref, t3_ref, w_ref, b_ref, x_ref, mod_ref, o_ref, *, ta):
    d = w_ref.shape[0]
    for a in range(ta):
        yb = y_ref[0, a]
        rhs = jnp.concatenate([yb[:, :d], yb[:, d:]], axis=0)
        z = jnp.dot(t3_ref[...], rhs, preferred_element_type=F32).astype(BF)
        yl = jnp.dot(z, w_ref[...], preferred_element_type=F32) + b_ref[...]
        o_ref[0, :, a, :] = x_ref[0, :, a, :] + mod_ref[0, 2:3, :] * yl


def _ft_seq2(y, t3, w, b, x, mod, *, ta=8):
    bx, t, d = x.shape
    nb = t // FT_NA
    xv = x.reshape(bx, nb, FT_NA, d)
    out = pl.pallas_call(
        functools.partial(_ft_seq2_kernel, ta=ta),
        grid=(bx, FT_NA // ta),
        in_specs=[
            pl.BlockSpec((1, ta, nb, 2 * d), lambda bi, i: (bi, i, 0, 0)),
            pl.BlockSpec(t3.shape, lambda bi, i: (0, 0)),
            pl.BlockSpec(w.shape, lambda bi, i: (0, 0)),
            pl.BlockSpec((1, d), lambda bi, i: (0, 0)),
            pl.BlockSpec((1, nb, ta, d), lambda bi, i: (bi, 0, i, 0)),
            pl.BlockSpec((1, N_MOD, d), _mod_map(mod.shape[0])),
        ],
        out_specs=pl.BlockSpec((1, nb, ta, d), lambda bi, i: (bi, 0, i, 0)),
        out_shape=jax.ShapeDtypeStruct((bx, nb, FT_NA, d), F32),
        compiler_params=_cparams(("parallel", "parallel"), 48),
        name="ft_seq2_mix",
    )(y, t3, w, b, xv, mod)
    return out.reshape(bx, t, d)


def _rope_tables(n_tok):
    t = jnp.arange(n_tok)
    row = (t // GRID_W).astype(F32)
    col = (t % GRID_W).astype(F32)
    n_axis = HEAD_DIM // 4
    inv = ROPE_THETA ** (-jnp.arange(n_axis, dtype=F32) / n_axis)
    ang = jnp.concatenate([row[:, None] * inv, col[:, None] * inv], axis=-1)
    ang = jnp.tile(jnp.repeat(ang, 2, axis=-1), (1, LANES // HEAD_DIM))
    sign = jnp.where(jnp.arange(LANES) % 2 == 0, -1.0, 1.0).astype(F32)
    return jnp.cos(ang), jnp.sin(ang) * sign


def _row(v):
    return v.reshape(1, -1)


def kernel(x, c, ctx, c_ctx, ada_w, ada_b, norm1_g, norm2_g, mlp_w1, mlp_w2, final_g, at_w_qkv, at_q_g, at_k_g, at_w_o, na_w_qkv, na_rpb, na_w_o, cv_w_pw1, cv_b_pw1, cv_w_dw, cv_b_dw, cv_ln_g, cv_ln_b, cv_w_pw2, cv_b_pw2, ft_w, ft_b):
    bsz, n_lat, d = x.shape
    depth = ada_w.shape[0]
    n_rows = n_lat // GRID_W
    n_ctx = ctx.shape[1]

    vec8 = jnp.zeros((8, d), F32).at[:bsz].set(c).at[bsz].set(c_ctx)
    mods = _ada_mods(vec8, ada_w, ada_b)

    hm = jnp.asarray(np.kron(np.eye(AT_HEADS), np.full((HEAD_DIM, HEAD_DIM), 1.0 / HEAD_DIM)), BF)
    h_ctx = ctx
    for i in range(depth):
        kind = i % N_MIXERS
        occ = i // N_MIXERS
        ctx_later = any((j % N_MIXERS) in (0, 1) for j in range(i + 1, depth))
        ml = mods[i, :bsz].reshape(bsz, N_MOD, d)
        mc = mods[i, bsz:bsz + 1].reshape(1, N_MOD, d)
        g1 = _row(norm1_g[i])
        if kind == 0:
            w = at_w_qkv[occ].astype(BF)
            wo = at_w_o[occ].astype(BF)
            qg = _row(jnp.tile(at_q_g[occ], AT_HEADS))
            kg = _row(jnp.tile(at_k_g[occ], AT_KV_HEADS))
            cos, sin = _rope_tables(n_lat)
            qt_l, k_l, vt_l = _gqa_proj(x, ml, g1, w, hm, qg, kg, cos, sin)
            qt_c, k_c, vt_c = _gqa_proj(h_ctx, mc, g1, w, hm, qg, kg, None, None)
            ot_l = _flash(qt_l, k_c, vt_c, k_l, vt_l, tq=512)
            x = _oproj(ot_l, wo, x, ml, tm=512)
            if ctx_later:
                ot_c = _flash(qt_c, k_c, vt_c, tq=n_ctx)
                h_ctx = _oproj(ot_c, wo, h_ctx, mc, tm=n_ctx)
        elif kind == 1:
            w = na_w_qkv[occ].astype(BF)
            wo = na_w_o[occ].astype(BF)
            qt_l, k_l, v_l = _na_proj(x, ml, g1, w)
            qt_c, k_c, v_c = _na_proj(h_ctx, mc, g1, w)
            quad, rowmask = _na_bias_tables(na_rpb[occ], n_rows)
            ot_l = _na_attend(qt_l, k_l, v_l, k_c, v_c, quad, rowmask, n_rows)
            x = _oproj(ot_l, wo, x, ml, tm=512)
            if ctx_later:
                raise NotImplementedError("context output of a neighbourhood layer is not needed at this depth")
        elif kind == 2:
            w1 = cv_w_pw1[occ].astype(BF)
            w2 = cv_w_pw2[occ].astype(BF)
            wdw = jnp.zeros((CONV_WIDTH + 1, d), F32).at[:CONV_WIDTH].set(cv_w_dw[occ])
            cv = (wdw, _row(cv_b_dw[occ]), _row(cv_ln_g[occ]), _row(cv_ln_b[occ]), w2, _row(cv_b_pw2[occ]))
            u = _conv_pw1(x, ml, g1, w1, _row(cv_b_pw1[occ]))
            x = _conv_tail(u, *cv, x, ml)
            if ctx_later:
                raise NotImplementedError("context output of a convolution layer is not needed at this depth")
        else:
            t0, t1, t3 = _ft_tables(n_lat, d // FT_GROUPS)
            fw = ft_w[occ].astype(BF)
            ww = _ft_width(x, ml, g1, t0)
            x = _ft_seq2(_ft_seq1(ww, t1), t3, fw, _row(ft_b[occ]), x, ml)
            if ctx_later:
                raise NotImplementedError("context output of a Fourier layer is not needed at this depth")
        g2 = _row(norm2_g[i])
        w1m = mlp_w1[i].astype(BF)
        w2m = mlp_w2[i].astype(BF)
        is_last = i == depth - 1
        x = _mlp(x, ml, g2, w1m, w2m, _row(final_g) if is_last else None, tm=1024)
        if ctx_later:
            h_ctx = _mlp(h_ctx, mc, g2, w1m, w2m, tm=n_ctx)
    return x
```

```python
import functools

import jax
import jax.numpy as jnp
import numpy as np
from jax import lax
from jax.experimental import pallas as pl
from jax.experimental.pallas import tpu as pltpu

GRID_W = 64
N_MIXERS = 4
HEAD_DIM = 64
AT_HEADS = 16
AT_KV_HEADS = 4
AT_WIDTH = AT_HEADS * HEAD_DIM
AT_KV_WIDTH = AT_KV_HEADS * HEAD_DIM
ROPE_THETA = 10000.0
NA_HEADS = 16
NA_WIDTH = NA_HEADS * HEAD_DIM
NA_WIN_ROWS = 8
NA_WIN_COLS = 16
CONV_WIDTH = 31
FT_GROUPS = 4
N_MOD = 6
EPS = 1e-6
QK_SCALE = HEAD_DIM ** -0.5
LOG2E = float(np.log2(np.e))

LANES = 128
BF16_ROWS = 16
V_ROWS = HEAD_DIM + BF16_ROWS
NA_QROWS = 4
NA_KROWS = 12
NA_CHUNK_ROWS = 4
NA_EROWS = NA_KROWS + 2 * NA_QROWS
CONV_HALO = 16
CONV_ROWS = 64
CONV_LANES = 256
SUBLANES = 8
FT_NA = 64
NEG_BIG = -1e30

BF = jnp.bfloat16
F32 = jnp.float32


def _cparams(sem, vmem_mib):
    return pltpu.CompilerParams(dimension_semantics=sem, vmem_limit_bytes=vmem_mib << 20)


def _norm_mod(x, g, shift, scale):
    ms = jnp.mean(x * x, axis=-1, keepdims=True)
    return (x * lax.rsqrt(ms + EPS) * g) * (1.0 + scale) + shift


def _ada_kernel(v_ref, w_ref, b_ref, o_ref):
    v = v_ref[...]
    sv = v * jax.nn.sigmoid(v)
    o_ref[0] = jnp.dot(sv, w_ref[0], preferred_element_type=F32, precision=lax.Precision.HIGHEST) + b_ref[0]


def _ada_mods(vec8, ada_w, ada_b):
    depth, d, n = ada_w.shape
    tn = n // 4
    return pl.pallas_call(
        _ada_kernel,
        grid=(depth, n // tn),
        in_specs=[
            pl.BlockSpec((8, d), lambda l, j: (0, 0)),
            pl.BlockSpec((1, d, tn), lambda l, j: (l, 0, j)),
            pl.BlockSpec((1, 1, tn), lambda l, j: (l, 0, j)),
        ],
        out_specs=pl.BlockSpec((1, 8, tn), lambda l, j: (l, 0, j)),
        out_shape=jax.ShapeDtypeStruct((depth, 8, n), F32),
        compiler_params=_cparams(("arbitrary", "arbitrary"), 40),
        name="ada_mods",
    )(vec8, ada_w, ada_b.reshape(depth, 1, n))


def _mod_map(n_mod_rows):
    if n_mod_rows == 1:
        return lambda b, *_: (0, 0, 0)
    return lambda b, *_: (b, 0, 0)


def _gqa_proj_kernel(*refs, rope, tm):
    if rope:
        x_ref, mod_ref, g1_ref, w_ref, hm_ref, qg_ref, kg_ref, cos_ref, sin_ref, qt_ref, k_ref, vt_ref = refs
    else:
        x_ref, mod_ref, g1_ref, w_ref, hm_ref, qg_ref, kg_ref, qt_ref, k_ref, vt_ref = refs
    h = _norm_mod(x_ref[0], g1_ref[...], mod_ref[0, 0:1, :], mod_ref[0, 1:2, :]).astype(BF)
    qkv = jnp.dot(h, w_ref[...], preferred_element_type=F32)
    q = qkv[:, :AT_WIDTH]
    k = qkv[:, AT_WIDTH:AT_WIDTH + AT_KV_WIDTH]
    v = qkv[:, AT_WIDTH + AT_KV_WIDTH:]
    hm = hm_ref[...]
    q = q * lax.rsqrt(jnp.dot((q * q).astype(BF), hm, preferred_element_type=F32) + EPS) * qg_ref[...]
    hk = hm[:AT_KV_WIDTH, :AT_KV_WIDTH]
    k = k * lax.rsqrt(jnp.dot((k * k).astype(BF), hk, preferred_element_type=F32) + EPS) * kg_ref[...]
    if rope:
        cos = cos_ref[...]
        sin = sin_ref[...]
        even = (lax.broadcasted_iota(jnp.int32, (tm, LANES), 1) % 2) == 0

        def rot(t):
            partner = jnp.where(even, pltpu.roll(t, LANES - 1, 1), pltpu.roll(t, 1, 1))
            return t * cos + partner * sin
    else:
        def rot(t):
            return t
    for c in range(AT_WIDTH // LANES):
        qc = rot(q[:, c * LANES:(c + 1) * LANES]) * (QK_SCALE * LOG2E)
        qt_ref[0, c * LANES:(c + 1) * LANES, :] = qc.T.astype(BF)
    for c in range(AT_KV_WIDTH // LANES):
        kc = rot(k[:, c * LANES:(c + 1) * LANES]).astype(BF)
        k_ref[0, 2 * c] = kc[:, :HEAD_DIM]
        k_ref[0, 2 * c + 1] = kc[:, HEAD_DIM:]
    vt = v.T.astype(BF)
    ones = (lax.broadcasted_iota(jnp.int32, (BF16_ROWS, tm), 0) == 0).astype(BF)
    for h in range(AT_KV_HEADS):
        vt_ref[0, h, :HEAD_DIM, :] = vt[h * HEAD_DIM:(h + 1) * HEAD_DIM]
        vt_ref[0, h, HEAD_DIM:, :] = ones


def _gqa_proj(x, mod, g1, w, hm, qg, kg, cos, sin, *, tm=256):
    bx, t, d = x.shape
    rope = cos is not None
    n = w.shape[1]
    const = lambda b, i: (0, 0)
    in_specs = [
        pl.BlockSpec((1, tm, d), lambda b, i: (b, i, 0)),
        pl.BlockSpec((1, N_MOD, d), _mod_map(mod.shape[0])),
        pl.BlockSpec((1, d), const),
        pl.BlockSpec((d, n), const),
        pl.BlockSpec(hm.shape, const),
        pl.BlockSpec((1, AT_WIDTH), const),
        pl.BlockSpec((1, AT_KV_WIDTH), const),
    ]
    args = [x, mod, g1, w, hm, qg, kg]
    if rope:
        in_specs += [pl.BlockSpec((tm, LANES), lambda b, i: (i, 0))] * 2
        args += [cos, sin]
    return pl.pallas_call(
        functools.partial(_gqa_proj_kernel, rope=rope, tm=tm),
        grid=(bx, t // tm),
        in_specs=in_specs,
        out_specs=[
            pl.BlockSpec((1, AT_WIDTH, tm), lambda b, i: (b, 0, i)),
            pl.BlockSpec((1, AT_KV_HEADS, tm, HEAD_DIM), lambda b, i: (b, 0, i, 0)),
            pl.BlockSpec((1, AT_KV_HEADS, V_ROWS, tm), lambda b, i: (b, 0, 0, i)),
        ],
        out_shape=[
            jax.ShapeDtypeStruct((bx, AT_WIDTH, t), BF),
            jax.ShapeDtypeStruct((bx, AT_KV_HEADS, t, HEAD_DIM), BF),
            jax.ShapeDtypeStruct((bx, AT_KV_HEADS, V_ROWS, t), BF),
        ],
        compiler_params=_cparams(("parallel", "parallel"), 48),
        name="gqa_proj_rope" if rope else "gqa_proj_ctx",
    )(*args)


def _flash_kernel(*refs, tk, lookahead, with_latents):
    if with_latents:
        qt_ref, kc_ref, vtc_ref, kl_ref, vtl_ref, ot_ref, m_sc, acc_sc = refs
        n_lat = kl_ref.shape[2] // tk
    else:
        qt_ref, kc_ref, vtc_ref, ot_ref, m_sc, acc_sc = refs
        n_lat = 0
    q = qt_ref[0]
    chunks = [(kc_ref.at[0, 0], vtc_ref.at[0, 0])]
    chunks += [(kl_ref.at[0, 0, c * tk:(c + 1) * tk, :], vtl_ref.at[0, 0, :, c * tk:(c + 1) * tk]) for c in range(n_lat)]

    def scores(c):
        return jnp.dot(chunks[c][0][...], q, preferred_element_type=F32)

    def weighted(c, s, m):
        return jnp.dot(chunks[c][1][...], jnp.exp2(s - m).astype(BF), preferred_element_type=F32)

    pending = [scores(c) for c in range(min(lookahead, len(chunks)))]
    m0 = jnp.max(pending[0], axis=0, keepdims=True)
    acc = None
    for c in range(len(chunks)):
        if c + lookahead < len(chunks):
            pending.append(scores(c + lookahead))
        pv = weighted(c, pending.pop(0), m0)
        acc = pv if acc is None else acc + pv
    denom = acc[HEAD_DIM:HEAD_DIM + 1]
    overflowed = jnp.max(jnp.where(jnp.isfinite(denom), 0.0, 1.0)) > 0.0

    @pl.when(jnp.logical_not(overflowed))
    def _():
        ot_ref[0] = (acc[:HEAD_DIM] * (1.0 / denom)).astype(BF)

    @pl.when(overflowed)
    def _():
        m_sc[...] = jnp.full(m_sc.shape, NEG_BIG, F32)
        acc_sc[...] = jnp.zeros(acc_sc.shape, F32)

        def update(k, vt):
            s = jnp.dot(k, q, preferred_element_type=F32)
            m_prev = m_sc[...]
            m_new = jnp.maximum(m_prev, jnp.max(s, axis=0, keepdims=True))
            pv = jnp.dot(vt, jnp.exp2(s - m_new).astype(BF), preferred_element_type=F32)
            acc_sc[...] = jnp.exp2(m_prev - m_new) * acc_sc[...] + pv
            m_sc[...] = m_new

        update(kc_ref[0, 0], vtc_ref[0, 0])

        def body(c, carry):
            off = pl.multiple_of(c * tk, tk)
            update(kl_ref[0, 0, pl.ds(off, tk), :], vtl_ref[0, 0, :, pl.ds(off, tk)])
            return carry

        if with_latents:
            lax.fori_loop(0, n_lat, body, 0)
        a = acc_sc[...]
        ot_ref[0] = (a[:HEAD_DIM] * (1.0 / a[HEAD_DIM:HEAD_DIM + 1])).astype(BF)


def _flash(qt, kc, vtc, kl=None, vtl=None, *, tq, tk=256, lookahead=2):
    b, w, t = qt.shape
    kv = kc.shape[1]
    heads = w // HEAD_DIM
    group = heads // kv
    kv_map = lambda bi, h, i: (bi, h // group, 0, 0)
    in_specs = [
        pl.BlockSpec((1, HEAD_DIM, tq), lambda bi, h, i: (bi, h, i)),
        pl.BlockSpec((1, 1) + kc.shape[2:], kv_map),
        pl.BlockSpec((1, 1) + vtc.shape[2:], kv_map),
    ]
    args = [qt, kc, vtc]
    if kl is not None:
        in_specs += [pl.BlockSpec((1, 1) + kl.shape[2:], kv_map), pl.BlockSpec((1, 1) + vtl.shape[2:], kv_map)]
        args += [kl, vtl]
    return pl.pallas_call(
        functools.partial(_flash_kernel, tk=tk, lookahead=lookahead, with_latents=kl is not None),
        grid=(b, heads, t // tq),
        in_specs=in_specs,
        out_specs=pl.BlockSpec((1, HEAD_DIM, tq), lambda bi, h, i: (bi, h, i)),
        out_shape=jax.ShapeDtypeStruct((b, w, t), BF),
        scratch_shapes=[pltpu.VMEM((1, tq), F32), pltpu.VMEM((V_ROWS, tq), F32)],
        compiler_params=_cparams(("parallel", "parallel", "parallel"), 48),
        name="gqa_flash" if kl is not None else "gqa_flash_ctx",
    )(*args)


def _oproj_kernel(ot_ref, w_ref, x_ref, mod_ref, o_ref):
    y = lax.dot_general(ot_ref[0], w_ref[...], (((0,), (0,)), ((), ())), preferred_element_type=F32)
    o_ref[0] = x_ref[0] + mod_ref[0, 2:3, :] * y


def _oproj(ot, w, x, mod, *, tm):
    b, t, d = x.shape
    kw = ot.shape[1]
    return pl.pallas_call(
        _oproj_kernel,
        grid=(b, t // tm),
        in_specs=[
            pl.BlockSpec((1, kw, tm), lambda bi, i: (bi, 0, i)),
            pl.BlockSpec((kw, d), lambda bi, i: (0, 0)),
            pl.BlockSpec((1, tm, d), lambda bi, i: (bi, i, 0)),
            pl.BlockSpec((1, N_MOD, d), _mod_map(mod.shape[0])),
        ],
        out_specs=pl.BlockSpec((1, tm, d), lambda bi, i: (bi, i, 0)),
        out_shape=jax.ShapeDtypeStruct((b, t, d), F32),
        compiler_params=_cparams(("parallel", "parallel"), 48),
        name="attn_oproj",
    )(ot, w, x, mod)


def _mlp_kernel(*refs, final):
    if final:
        x_ref, mod_ref, g2_ref, w1_ref, w2_ref, fg_ref, o_ref, h_sc, acc_sc = refs
    else:
        x_ref, mod_ref, g2_ref, w1_ref, w2_ref, o_ref, h_sc, acc_sc = refs
    j = pl.program_id(2)

    @pl.when(j == 0)
    def _():
        h_sc[...] = _norm_mod(x_ref[0], g2_ref[...], mod_ref[0, 3:4, :], mod_ref[0, 4:5, :]).astype(BF)
        acc_sc[...] = jnp.zeros(acc_sc.shape, F32)

    a = jnp.dot(h_sc[...], w1_ref[...], preferred_element_type=F32)
    a = jnp.square(jnp.maximum(a, 0.0)).astype(BF)
    acc_sc[...] += jnp.dot(a, w2_ref[...], preferred_element_type=F32)

    @pl.when(j == pl.num_programs(2) - 1)
    def _():
        y = x_ref[0] + mod_ref[0, 5:6, :] * acc_sc[...]
        if final:
            ms = jnp.mean(y * y, axis=-1, keepdims=True)
            y = y * lax.rsqrt(ms + EPS) * fg_ref[...]
        o_ref[0] = y


def _mlp(x, mod, g2, w1, w2, layer, final_g=None, *, tm, tf=512):
    b, t, d = x.shape
    f = w1.shape[2]
    final = final_g is not None
    in_specs = [
        pl.BlockSpec((1, tm, d), lambda bi, i, j: (bi, i, 0)),
        pl.BlockSpec((1, N_MOD, d), _mod_map(mod.shape[0])),
        pl.BlockSpec((1, d), lambda bi, i, j: (0, 0)),
        pl.BlockSpec((None, d, tf), lambda bi, i, j: (layer, 0, j)),
        pl.BlockSpec((None, tf, d), lambda bi, i, j: (layer, j, 0)),
    ]
    args = [x, mod, g2, w1, w2]
    if final:
        in_specs.append(pl.BlockSpec((1, d), lambda bi, i, j: (0, 0)))
        args.append(final_g)
    return pl.pallas_call(
        functools.partial(_mlp_kernel, final=final),
        grid=(b, t // tm, f // tf),
        in_specs=in_specs,
        out_specs=pl.BlockSpec((1, tm, d), lambda bi, i, j: (bi, i, 0)),
        out_shape=jax.ShapeDtypeStruct((b, t, d), F32),
        scratch_shapes=[pltpu.VMEM((tm, d), BF), pltpu.VMEM((tm, d), F32)],
        compiler_params=_cparams(("parallel", "parallel", "arbitrary"), 52),
        name="mlp_final" if final else "mlp",
    )(*args)


def _na_proj_kernel(x_ref, mod_ref, g1_ref, w_ref, qt_ref, k_ref, v_ref):
    h = _norm_mod(x_ref[0], g1_ref[...], mod_ref[0, 0:1, :], mod_ref[0, 1:2, :]).astype(BF)
    qkv = jnp.dot(h, w_ref[...], preferred_element_type=F32)
    for c in range(NA_WIDTH // LANES):
        qt_ref[0, c * LANES:(c + 1) * LANES, :] = (qkv[:, c * LANES:(c + 1) * LANES] * (QK_SCALE * LOG2E)).T.astype(BF)
    k_ref[0] = qkv[:, NA_WIDTH:2 * NA_WIDTH].astype(BF)
    v_ref[0] = qkv[:, 2 * NA_WIDTH:].astype(BF)


def _na_proj(x, mod, g1, w, *, tm=256):
    bx, t, d = x.shape
    return pl.pallas_call(
        _na_proj_kernel,
        grid=(bx, t // tm),
        in_specs=[
            pl.BlockSpec((1, tm, d), lambda b, i: (b, i, 0)),
            pl.BlockSpec((1, N_MOD, d), _mod_map(mod.shape[0])),
            pl.BlockSpec((1, d), lambda b, i: (0, 0)),
            pl.BlockSpec(w.shape, lambda b, i: (0, 0)),
        ],
        out_specs=[
            pl.BlockSpec((1, NA_WIDTH, tm), lambda b, i: (b, 0, i)),
            pl.BlockSpec((1, tm, NA_WIDTH), lambda b, i: (b, i, 0)),
            pl.BlockSpec((1, tm, NA_WIDTH), lambda b, i: (b, i, 0)),
        ],
        out_shape=[
            jax.ShapeDtypeStruct((bx, NA_WIDTH, t), BF),
            jax.ShapeDtypeStruct((bx, t, NA_WIDTH), BF),
            jax.ShapeDtypeStruct((bx, t, NA_WIDTH), BF),
        ],
        compiler_params=_cparams(("parallel", "parallel"), 48),
        name="na_proj",
    )(x, mod, g1, w)


def _na_key_base(g, n_groups):
    return jnp.clip(NA_QROWS * g - NA_WIN_ROWS // 2, 0, NA_QROWS * n_groups - NA_KROWS)


def _na_kernel(qt_ref, k_ref, v_ref, kc_ref, vc_ref, quad_ref, mask_ref, ot_ref, *, n_groups):
    g = pl.program_id(2)
    kb_row = _na_key_base(g, n_groups)
    e0 = kb_row - NA_QROWS * g + NA_WIN_ROWS
    kb = pl.multiple_of(kb_row * GRID_W, GRID_W)
    q2 = qt_ref[0]
    upper = lax.broadcasted_iota(jnp.int32, q2.shape, 0) < HEAD_DIM
    qh = [jnp.where(upper, q2, jnp.zeros_like(q2)), jnp.where(upper, jnp.zeros_like(q2), q2)]
    tn = (((0,), (0,)), ((), ()))
    ck = NA_CHUNK_ROWS * GRID_W
    n_nb = NA_KROWS // NA_CHUNK_ROWS

    def scores(half, j):
        if j < 0:
            return jnp.dot(kc_ref[0], qh[half], preferred_element_type=F32)
        bias = jnp.concatenate([quad_ref[half, e0 + NA_CHUNK_ROWS * j + i] for i in range(NA_CHUNK_ROWS)], axis=0)
        bias = bias + mask_ref[0, j * ck:(j + 1) * ck, :]
        return jnp.dot(k_ref[0, pl.ds(kb + j * ck, ck), :], qh[half], preferred_element_type=F32) + bias

    def values(j):
        return vc_ref[0] if j < 0 else v_ref[0, pl.ds(kb + j * ck, ck), :]

    units = [(half, j) for j in range(-1, n_nb) for half in range(2)]
    lookahead = 6
    pending = [scores(*u) for u in units[:lookahead]]
    m0 = [jnp.max(s, axis=0, keepdims=True) for s in pending[:2]]
    acc = [None, None]
    den = [None, None]
    for n, (half, j) in enumerate(units):
        if n + lookahead < len(units):
            pending.append(scores(*units[n + lookahead]))
        p = jnp.exp2(pending.pop(0) - m0[half])
        pv = lax.dot_general(values(j), p.astype(BF), tn, preferred_element_type=F32)
        ps = jnp.sum(p, axis=0, keepdims=True)
        acc[half] = pv if acc[half] is None else acc[half] + pv
        den[half] = ps if den[half] is None else den[half] + ps
    bad = jnp.where(jnp.isfinite(den[0]) & jnp.isfinite(den[1]), 0.0, 1.0)
    overflowed = jnp.max(bad) > 0.0

    @pl.when(jnp.logical_not(overflowed))
    def _():
        ot_ref[0] = jnp.where(upper, acc[0] * (1.0 / den[0]), acc[1] * (1.0 / den[1])).astype(BF)

    @pl.when(overflowed)
    def _():
        outs = []
        for half in range(2):
            s_all = [scores(half, j) for j in range(-1, n_nb)]
            m = s_all[0].max(axis=0, keepdims=True)
            for s in s_all[1:]:
                m = jnp.maximum(m, jnp.max(s, axis=0, keepdims=True))
            o = None
            l = None
            for j, s in zip(range(-1, n_nb), s_all):
                p = jnp.exp2(s - m)
                pv = lax.dot_general(values(j), p.astype(BF), tn, preferred_element_type=F32)
                ps = jnp.sum(p, axis=0, keepdims=True)
                o = pv if o is None else o + pv
                l = ps if l is None else l + ps
            outs.append(o * (1.0 / l))
        ot_ref[0] = jnp.where(upper, outs[0], outs[1]).astype(BF)


def _na_bias_tables(rpb, n_rows):
    n_groups = n_rows // NA_QROWS
    n_h, n_dr, n_dc = rpb.shape
    m = n_dc + 2 * GRID_W
    c0 = GRID_W + NA_WIN_COLS - 1
    flat = jnp.tile(jnp.pad(rpb, ((0, 0), (0, 0), (GRID_W, GRID_W))), (1, 1, GRID_W + 1))
    t = flat[:, :, c0:c0 + GRID_W * (m - 1)].reshape(n_h, n_dr, GRID_W, m - 1)[..., :GRID_W]
    qc = np.arange(GRID_W)[None, :]
    kc = np.arange(GRID_W)[:, None]
    cs = np.clip(qc - NA_WIN_COLS // 2, 0, GRID_W - NA_WIN_COLS)
    col_ok = (kc >= cs) & (kc < cs + NA_WIN_COLS)
    tc = jnp.where(col_ok, jnp.swapaxes(t, -1, -2) * LOG2E, NEG_BIG)
    back = NA_EROWS - 1 - n_dr
    tcp = jnp.pad(tc, ((0, 0), (NA_QROWS, back), (0, 0), (0, 0)))
    quad = jnp.concatenate([tcp[:, NA_QROWS - 1 - a:NA_QROWS - 1 - a + NA_EROWS] for a in range(NA_QROWS)], axis=-1)
    masks = []
    for g in (0, 1, n_groups - 1):
        r = NA_QROWS * g + np.arange(NA_QROWS)[None, None, :, None]
        kb = int(np.clip(NA_QROWS * g - NA_WIN_ROWS // 2, 0, n_rows - NA_KROWS))
        krow = kb + np.arange(NA_KROWS)[:, None, None, None]
        rs = np.clip(r - NA_WIN_ROWS // 2, 0, n_rows - NA_WIN_ROWS)
        ok = np.broadcast_to((krow >= rs) & (krow < rs + NA_WIN_ROWS), (NA_KROWS, GRID_W, NA_QROWS, GRID_W))
        masks.append(np.where(ok, 0.0, NEG_BIG).reshape(NA_KROWS * GRID_W, NA_QROWS * GRID_W))
    return quad.astype(F32), jnp.asarray(np.stack(masks), F32)


def _na_attend(qt, k, v, kc, vc, quad, rowmask, n_rows):
    b, w, s = qt.shape
    c = kc.shape[1]
    n_groups = n_rows // NA_QROWS
    tq = NA_QROWS * GRID_W
    pair = 2 * HEAD_DIM

    def mask_map(bi, hp, g):
        return (jnp.where(g == 0, 0, jnp.where(g == n_groups - 1, 2, 1)), 0, 0)

    return pl.pallas_call(
        functools.partial(_na_kernel, n_groups=n_groups),
        grid=(b, w // pair, n_groups),
        in_specs=[
            pl.BlockSpec((1, pair, tq), lambda bi, hp, g: (bi, hp, g)),
            pl.BlockSpec((1, s, pair), lambda bi, hp, g: (bi, 0, hp)),
            pl.BlockSpec((1, s, pair), lambda bi, hp, g: (bi, 0, hp)),
            pl.BlockSpec((1, c, pair), lambda bi, hp, g: (bi, 0, hp)),
            pl.BlockSpec((1, c, pair), lambda bi, hp, g: (bi, 0, hp)),
            pl.BlockSpec((2, NA_EROWS, GRID_W, tq), lambda bi, hp, g: (hp, 0, 0, 0)),
            pl.BlockSpec((1, NA_KROWS * GRID_W, tq), mask_map),
        ],
        out_specs=pl.BlockSpec((1, pair, tq), lambda bi, hp, g: (bi, hp, g)),
        out_shape=jax.ShapeDtypeStruct((b, w, s), BF),
        compiler_params=_cparams(("parallel", "parallel", "arbitrary"), 48),
        name="na_attend",
    )(qt, k, v, kc, vc, quad, rowmask)


def _conv_pw1_kernel(x_ref, mod_ref, g1_ref, w_ref, b_ref, u_ref):
    h = _norm_mod(x_ref[0], g1_ref[...], mod_ref[0, 0:1, :], mod_ref[0, 1:2, :]).astype(BF)
    ag = jnp.dot(h, w_ref[...], preferred_element_type=F32) + b_ref[...]
    d = u_ref.shape[2]
    u_ref[0] = ag[:, :d] * jax.nn.sigmoid(ag[:, d:])


def _conv_pw1(x, mod, g1, w, b, *, tm=512):
    bx, t, d = x.shape
    return pl.pallas_call(
        _conv_pw1_kernel,
        grid=(bx, t // tm),
        in_specs=[
            pl.BlockSpec((1, tm, d), lambda bi, i: (bi, i, 0)),
            pl.BlockSpec((1, N_MOD, d), _mod_map(mod.shape[0])),
            pl.BlockSpec((1, d), lambda bi, i: (0, 0)),
            pl.BlockSpec(w.shape, lambda bi, i: (0, 0)),
            pl.BlockSpec((1, 2 * d), lambda bi, i: (0, 0)),
        ],
        out_specs=pl.BlockSpec((1, tm, d), lambda bi, i: (bi, i, 0)),
        out_shape=jax.ShapeDtypeStruct((bx, t, d), F32),
        compiler_params=_cparams(("parallel", "parallel"), 48),
        name="conv_pw1_glu",
    )(x, mod, g1, w, b)


def _conv_tail_kernel(u_ref, up_ref, un_ref, wdw_ref, bdw_ref, lg_ref, lb_ref, w2_ref, b2_ref, x_ref, mod_ref,
                      o_ref, buf_sc, cv_sc, *, tm):
    i = pl.program_id(1)
    last = pl.num_programs(1) - 1
    buf_sc[0:CONV_HALO, :] = jnp.where(i > 0, up_ref[0], 0.0)
    buf_sc[CONV_HALO:CONV_HALO + tm, :] = u_ref[0]
    buf_sc[CONV_HALO + tm:, :] = jnp.where(i < last, un_ref[0], 0.0)
    assert CONV_HALO - CONV_WIDTH // 2 == 1
    d = buf_sc.shape[1]
    for lb in range(d // CONV_LANES):
        ls = slice(lb * CONV_LANES, (lb + 1) * CONV_LANES)
        for rc in range(tm // CONV_ROWS):
            r0 = rc * CONV_ROWS
            acc = None
            for r in range(SUBLANES):
                part = None
                for j in range(r, CONV_WIDTH + 1, SUBLANES):
                    if j == 0:
                        continue
                    rows = slice(r0 + j - r, r0 + j - r + CONV_ROWS + SUBLANES)
                    term = buf_sc[rows, ls] * wdw_ref[j - 1:j, ls]
                    part = term if part is None else part + term
                part = part[r:r + CONV_ROWS]
                acc = part if acc is None else acc + part
            cv_sc[r0:r0 + CONV_ROWS, ls] = acc
    u = cv_sc[...] + bdw_ref[...]
    mu = jnp.mean(u, axis=-1, keepdims=True)
    uc = u - mu
    var = jnp.mean(uc * uc, axis=-1, keepdims=True)
    y = uc * lax.rsqrt(var + EPS) * lg_ref[...] + lb_ref[...]
    y = (y * jax.nn.sigmoid(y)).astype(BF)
    z = jnp.dot(y, w2_ref[...], preferred_element_type=F32) + b2_ref[...]
    o_ref[0] = x_ref[0] + mod_ref[0, 2:3, :] * z


def _conv_tail(u, wdw, bdw, lg, lb, w2, b2, x, mod, *, tm=256):
    bx, t, d = x.shape
    hb = tm // CONV_HALO
    n_halo = t // CONV_HALO
    vec = lambda: pl.BlockSpec((1, d), lambda bi, i: (0, 0))
    return pl.pallas_call(
        functools.partial(_conv_tail_kernel, tm=tm),
        grid=(bx, t // tm),
        in_specs=[
            pl.BlockSpec((1, tm, d), lambda bi, i: (bi, i, 0)),
            pl.BlockSpec((1, CONV_HALO, d), lambda bi, i: (bi, jnp.maximum(i * hb - 1, 0), 0)),
            pl.BlockSpec((1, CONV_HALO, d), lambda bi, i: (bi, jnp.minimum((i + 1) * hb, n_halo - 1), 0)),
            pl.BlockSpec(wdw.shape, lambda bi, i: (0, 0)),
            vec(), vec(), vec(),
            pl.BlockSpec(w2.shape, lambda bi, i: (0, 0)),
            vec(),
            pl.BlockSpec((1, tm, d), lambda bi, i: (bi, i, 0)),
            pl.BlockSpec((1, N_MOD, d), _mod_map(mod.shape[0])),
        ],
        out_specs=pl.BlockSpec((1, tm, d), lambda bi, i: (bi, i, 0)),
        out_shape=jax.ShapeDtypeStruct((bx, t, d), F32),
        scratch_shapes=[pltpu.VMEM((tm + 2 * CONV_HALO, d), F32), pltpu.VMEM((tm, d), F32)],
        compiler_params=_cparams(("parallel", "parallel"), 48),
        name="conv_tail",
    )(u, u, u, wdw, bdw, lg, lb, w2, b2, x, mod)


def _ft_tables(n_seq, gw):
    nb = n_seq // FT_NA
    m = np.arange(gw)
    ph = 2 * np.pi * ((m[:, None] * m[None, :]) % gw) / gw
    t0 = np.concatenate([np.cos(ph), -np.sin(ph)], axis=1) / np.sqrt(gw)
    ka = np.arange(FT_NA)[None, :, None]
    na = np.arange(FT_NA)[None, None, :]
    jb = np.arange(nb)[:, None, None]
    ph = 2 * np.pi * ((ka * (nb * na + jb)) % n_seq) / n_seq
    cw, sw = np.cos(ph), np.sin(ph)
    t1 = np.concatenate([np.concatenate([cw, sw], axis=2), np.concatenate([-sw, cw], axis=2)], axis=1)
    kb = np.arange(nb)
    ph = 2 * np.pi * ((kb[:, None] * kb[None, :]) % nb) / nb
    t3 = np.concatenate([np.cos(ph), np.sin(ph)], axis=1) / np.sqrt(n_seq)
    return tuple(jnp.asarray(t, F32).astype(BF) for t in (t0, t1, t3))


def _ft_width_kernel(x_ref, mod_ref, g1_ref, t0_ref, w_ref):
    h = _norm_mod(x_ref[0], g1_ref[...], mod_ref[0, 0:1, :], mod_ref[0, 1:2, :]).astype(BF)
    d = h.shape[1]
    gw = d // FT_GROUPS
    for g in range(FT_GROUPS):
        wg = jnp.dot(h[:, g * gw:(g + 1) * gw], t0_ref[...], preferred_element_type=F32).astype(BF)
        w_ref[0, :, g * gw:(g + 1) * gw] = wg[:, :gw]
        w_ref[0, :, d + g * gw:d + (g + 1) * gw] = wg[:, gw:]


def _ft_width(x, mod, g1, t0, *, tm=512):
    bx, t, d = x.shape
    return pl.pallas_call(
        _ft_width_kernel,
        grid=(bx, t // tm),
        in_specs=[
            pl.BlockSpec((1, tm, d), lambda bi, i: (bi, i, 0)),
            pl.BlockSpec((1, N_MOD, d), _mod_map(mod.shape[0])),
            pl.BlockSpec((1, d), lambda bi, i: (0, 0)),
            pl.BlockSpec(t0.shape, lambda bi, i: (0, 0)),
        ],
        out_specs=pl.BlockSpec((1, tm, 2 * d), lambda bi, i: (bi, i, 0)),
        out_shape=jax.ShapeDtypeStruct((bx, t, 2 * d), BF),
        compiler_params=_cparams(("parallel", "parallel"), 48),
        name="ft_width",
    )(x, mod, g1, t0)


def _ft_seq1_kernel(w_ref, t1_ref, y_ref, *, tb, d):
    for j in range(tb):
        blk = w_ref[0, :, j, :]
        rhs = jnp.concatenate([blk[:, :d], blk[:, d:]], axis=0)
        y = jnp.dot(t1_ref[j], rhs, preferred_element_type=F32).astype(BF)
        y_ref[0, :, j, :d] = y[:FT_NA]
        y_ref[0, :, j, d:] = y[FT_NA:]


def _ft_seq1(ww, t1, *, tb=16):
    bx, t, d2 = ww.shape
    nb = t // FT_NA
    wv = ww.reshape(bx, FT_NA, nb, d2)
    return pl.pallas_call(
        functools.partial(_ft_seq1_kernel, tb=tb, d=d2 // 2),
        grid=(bx, nb // tb),
        in_specs=[
            pl.BlockSpec((1, FT_NA, tb, d2), lambda bi, i: (bi, 0, i, 0)),
            pl.BlockSpec((tb, 2 * FT_NA, 2 * FT_NA), lambda bi, i: (i, 0, 0)),
        ],
        out_specs=pl.BlockSpec((1, FT_NA, tb, d2), lambda bi, i: (bi, 0, i, 0)),
        out_shape=jax.ShapeDtypeStruct((bx, FT_NA, nb, d2), BF),
        compiler_params=_cparams(("parallel", "parallel"), 48),
        name="ft_seq1",
    )(wv, t1)


def _ft_seq2_kernel(y_ref, t3_ref, w_ref, b_ref, x_ref, mod_ref, o_ref, *, ta):
    d = w_ref.shape[0]
    for a in range(ta):
        yb = y_ref[0, a]
        rhs = jnp.concatenate([yb[:, :d], yb[:, d:]], axis=0)
        z = jnp.dot(t3_ref[...], rhs, preferred_element_type=F32).astype(BF)
        yl = jnp.dot(z, w_ref[...], preferred_element_type=F32) + b_ref[...]
        o_ref[0, :, a, :] = x_ref[0, :, a, :] + mod_ref[0, 2:3, :] * yl


def _ft_seq2(y, t3, w, b, x, mod, *, ta=8):
    bx, t, d = x.shape
    nb = t // FT_NA
    xv = x.reshape(bx, nb, FT_NA, d)
    out = pl.pallas_call(
        functools.partial(_ft_seq2_kernel, ta=ta),
        grid=(bx, FT_NA // ta),
        in_specs=[
            pl.BlockSpec((1, ta, nb, 2 * d), lambda bi, i: (bi, i, 0, 0)),
            pl.BlockSpec(t3.shape, lambda bi, i: (0, 0)),
            pl.BlockSpec(w.shape, lambda bi, i: (0, 0)),
            pl.BlockSpec((1, d), lambda bi, i: (0, 0)),
            pl.BlockSpec((1, nb, ta, d), lambda bi, i: (bi, 0, i, 0)),
            pl.BlockSpec((1, N_MOD, d), _mod_map(mod.shape[0])),
        ],
        out_specs=pl.BlockSpec((1, nb, ta, d), lambda bi, i: (bi, 0, i, 0)),
        out_shape=jax.ShapeDtypeStruct((bx, nb, FT_NA, d), F32),
        compiler_params=_cparams(("parallel", "parallel"), 48),
        name="ft_seq2_mix",
    )(y, t3, w, b, xv, mod)
    return out.reshape(bx, t, d)


def _rope_tables(n_tok):
    t = jnp.arange(n_tok)
    row = (t // GRID_W).astype(F32)
    col = (t % GRID_W).astype(F32)
    n_axis = HEAD_DIM // 4
    inv = ROPE_THETA ** (-jnp.arange(n_axis, dtype=F32) / n_axis)
    ang = jnp.concatenate([row[:, None] * inv, col[:, None] * inv], axis=-1)
    ang = jnp.tile(jnp.repeat(ang, 2, axis=-1), (1, LANES // HEAD_DIM))
    sign = jnp.where(jnp.arange(LANES) % 2 == 0, -1.0, 1.0).astype(F32)
    return jnp.cos(ang), jnp.sin(ang) * sign


def _row(v):
    return v.reshape(1, -1)


def kernel(x, c, ctx, c_ctx, ada_w, ada_b, norm1_g, norm2_g, mlp_w1, mlp_w2, final_g, at_w_qkv, at_q_g, at_k_g, at_w_o, na_w_qkv, na_rpb, na_w_o, cv_w_pw1, cv_b_pw1, cv_w_dw, cv_b_dw, cv_ln_g, cv_ln_b, cv_w_pw2, cv_b_pw2, ft_w, ft_b):
    bsz, n_lat, d = x.shape
    depth = ada_w.shape[0]
    n_rows = n_lat // GRID_W
    n_ctx = ctx.shape[1]

    vec8 = jnp.zeros((8, d), F32).at[:bsz].set(c).at[bsz].set(c_ctx)
    mods = _ada_mods(vec8, ada_w, ada_b)

    w1m = mlp_w1.astype(BF)
    w2m = mlp_w2.astype(BF)
    hm = jnp.asarray(np.kron(np.eye(AT_HEADS), np.full((HEAD_DIM, HEAD_DIM), 1.0 / HEAD_DIM)), BF)
    h_ctx = ctx
    for i in range(depth):
        kind = i % N_MIXERS
        occ = i // N_MIXERS
        ctx_later = any((j % N_MIXERS) in (0, 1) for j in range(i + 1, depth))
        ml = mods[i, :bsz].reshape(bsz, N_MOD, d)
        mc = mods[i, bsz:bsz + 1].reshape(1, N_MOD, d)
        g1 = _row(norm1_g[i])
        if kind == 0:
            w = at_w_qkv[occ].astype(BF)
            wo = at_w_o[occ].astype(BF)
            qg = _row(jnp.tile(at_q_g[occ], AT_HEADS))
            kg = _row(jnp.tile(at_k_g[occ], AT_KV_HEADS))
            cos, sin = _rope_tables(n_lat)
            qt_l, k_l, vt_l = _gqa_proj(x, ml, g1, w, hm, qg, kg, cos, sin)
            qt_c, k_c, vt_c = _gqa_proj(h_ctx, mc, g1, w, hm, qg, kg, None, None)
            ot_l = _flash(qt_l, k_c, vt_c, k_l, vt_l, tq=512)
            x = _oproj(ot_l, wo, x, ml, tm=512)
            if ctx_later:
                ot_c = _flash(qt_c, k_c, vt_c, tq=n_ctx)
                h_ctx = _oproj(ot_c, wo, h_ctx, mc, tm=n_ctx)
        elif kind == 1:
            w = na_w_qkv[occ].astype(BF)
            wo = na_w_o[occ].astype(BF)
            qt_l, k_l, v_l = _na_proj(x, ml, g1, w)
            qt_c, k_c, v_c = _na_proj(h_ctx, mc, g1, w)
            quad, rowmask = _na_bias_tables(na_rpb[occ], n_rows)
            ot_l = _na_attend(qt_l, k_l, v_l, k_c, v_c, quad, rowmask, n_rows)
            x = _oproj(ot_l, wo, x, ml, tm=512)
            if ctx_later:
                raise NotImplementedError("context output of a neighbourhood layer is not needed at this depth")
        elif kind == 2:
            w1 = cv_w_pw1[occ].astype(BF)
            w2 = cv_w_pw2[occ].astype(BF)
            wdw = jnp.zeros((CONV_WIDTH + 1, d), F32).at[:CONV_WIDTH].set(cv_w_dw[occ])
            cv = (wdw, _row(cv_b_dw[occ]), _row(cv_ln_g[occ]), _row(cv_ln_b[occ]), w2, _row(cv_b_pw2[occ]))
            u = _conv_pw1(x, ml, g1, w1, _row(cv_b_pw1[occ]))
            x = _conv_tail(u, *cv, x, ml)
            if ctx_later:
                raise NotImplementedError("context output of a convolution layer is not needed at this depth")
        else:
            t0, t1, t3 = _ft_tables(n_lat, d // FT_GROUPS)
            fw = ft_w[occ].astype(BF)
            ww = _ft_width(x, ml, g1, t0)
            x = _ft_seq2(_ft_seq1(ww, t1), t3, fw, _row(ft_b[occ]), x, ml)
            if ctx_later:
                raise NotImplementedError("context output of a Fourier layer is not needed at this depth")
        g2 = _row(norm2_g[i])
        is_last = i == depth - 1
        x = _mlp(x, ml, g2, w1m, w2m, i, _row(final_g) if is_last else None, tm=1024, tf=1024)
        if ctx_later:
            h_ctx = _mlp(h_ctx, mc, g2, w1m, w2m, i, tm=n_ctx)
    return x
```

```python
import functools

import jax
import jax.numpy as jnp
import numpy as np
from jax import lax
from jax.experimental import pallas as pl
from jax.experimental.pallas import tpu as pltpu

GRID_W = 64
N_MIXERS = 4
HEAD_DIM = 64
AT_HEADS = 16
AT_KV_HEADS = 4
AT_WIDTH = AT_HEADS * HEAD_DIM
AT_KV_WIDTH = AT_KV_HEADS * HEAD_DIM
ROPE_THETA = 10000.0
NA_HEADS = 16
NA_WIDTH = NA_HEADS * HEAD_DIM
NA_WIN_ROWS = 8
NA_WIN_COLS = 16
CONV_WIDTH = 31
FT_GROUPS = 4
N_MOD = 6
EPS = 1e-6
QK_SCALE = HEAD_DIM ** -0.5
LOG2E = float(np.log2(np.e))

LANES = 128
NA_QROWS = 4
NA_KROWS = 12
NA_CHUNK_ROWS = 4
NA_EROWS = NA_KROWS + 2 * NA_QROWS
CONV_HALO = 16
CONV_ROWS = 64
CONV_LANES = 256
SUBLANES = 8
FT_NA = 64
NEG_BIG = -1e30

BF = jnp.bfloat16
F32 = jnp.float32


def _cparams(sem, vmem_mib):
    return pltpu.CompilerParams(dimension_semantics=sem, vmem_limit_bytes=vmem_mib << 20)


def _norm_mod(x, g, shift, scale):
    ms = jnp.mean(x * x, axis=-1, keepdims=True)
    return (x * lax.rsqrt(ms + EPS) * g) * (1.0 + scale) + shift


def _ada_kernel(v_ref, w_ref, b_ref, o_ref):
    v = v_ref[...]
    sv = v * jax.nn.sigmoid(v)
    o_ref[0] = jnp.dot(sv, w_ref[0], preferred_element_type=F32, precision=lax.Precision.HIGHEST) + b_ref[0]


def _ada_mods(vec8, ada_w, ada_b):
    depth, d, n = ada_w.shape
    tn = n // 4
    return pl.pallas_call(
        _ada_kernel,
        grid=(depth, n // tn),
        in_specs=[
            pl.BlockSpec((8, d), lambda l, j: (0, 0)),
            pl.BlockSpec((1, d, tn), lambda l, j: (l, 0, j)),
            pl.BlockSpec((1, 1, tn), lambda l, j: (l, 0, j)),
        ],
        out_specs=pl.BlockSpec((1, 8, tn), lambda l, j: (l, 0, j)),
        out_shape=jax.ShapeDtypeStruct((depth, 8, n), F32),
        compiler_params=_cparams(("arbitrary", "arbitrary"), 40),
        name="ada_mods",
    )(vec8, ada_w, ada_b.reshape(depth, 1, n))


def _mod_map(n_mod_rows):
    if n_mod_rows == 1:
        return lambda b, *_: (0, 0, 0)
    return lambda b, *_: (b, 0, 0)


def _gqa_proj_kernel(*refs, rope, tm):
    if rope:
        x_ref, mod_ref, g1_ref, w_ref, hm_ref, qg_ref, kg_ref, cos_ref, sin_ref, qt_ref, k_ref, vt_ref = refs
    else:
        x_ref, mod_ref, g1_ref, w_ref, hm_ref, qg_ref, kg_ref, qt_ref, k_ref, vt_ref = refs
    h = _norm_mod(x_ref[0], g1_ref[...], mod_ref[0, 0:1, :], mod_ref[0, 1:2, :]).astype(BF)
    qkv = jnp.dot(h, w_ref[...], preferred_element_type=F32)
    q = qkv[:, :AT_WIDTH]
    k = qkv[:, AT_WIDTH:AT_WIDTH + AT_KV_WIDTH]
    v = qkv[:, AT_WIDTH + AT_KV_WIDTH:]
    hm = hm_ref[...]
    q = q * lax.rsqrt(jnp.dot((q * q).astype(BF), hm, preferred_element_type=F32) + EPS) * qg_ref[...]
    hk = hm[:AT_KV_WIDTH, :AT_KV_WIDTH]
    k = k * lax.rsqrt(jnp.dot((k * k).astype(BF), hk, preferred_element_type=F32) + EPS) * kg_ref[...]
    if rope:
        cos = cos_ref[...]
        sin = sin_ref[...]
        even = (lax.broadcasted_iota(jnp.int32, (tm, LANES), 1) % 2) == 0

        def rot(t):
            partner = jnp.where(even, pltpu.roll(t, LANES - 1, 1), pltpu.roll(t, 1, 1))
            return t * cos + partner * sin
    else:
        def rot(t):
            return t
    for c in range(AT_WIDTH // LANES):
        qc = rot(q[:, c * LANES:(c + 1) * LANES]) * (QK_SCALE * LOG2E)
        qt_ref[0, c * LANES:(c + 1) * LANES, :] = qc.T.astype(BF)
    for c in range(AT_KV_WIDTH // LANES):
        kc = rot(k[:, c * LANES:(c + 1) * LANES]).astype(BF)
        k_ref[0, 2 * c] = kc[:, :HEAD_DIM]
        k_ref[0, 2 * c + 1] = kc[:, HEAD_DIM:]
    vt = v.T.astype(BF)
    for h in range(AT_KV_HEADS):
        vt_ref[0, h] = vt[h * HEAD_DIM:(h + 1) * HEAD_DIM]


def _gqa_proj(x, mod, g1, w, hm, qg, kg, cos, sin, *, tm=256):
    bx, t, d = x.shape
    rope = cos is not None
    n = w.shape[1]
    const = lambda b, i: (0, 0)
    in_specs = [
        pl.BlockSpec((1, tm, d), lambda b, i: (b, i, 0)),
        pl.BlockSpec((1, N_MOD, d), _mod_map(mod.shape[0])),
        pl.BlockSpec((1, d), const),
        pl.BlockSpec((d, n), const),
        pl.BlockSpec(hm.shape, const),
        pl.BlockSpec((1, AT_WIDTH), const),
        pl.BlockSpec((1, AT_KV_WIDTH), const),
    ]
    args = [x, mod, g1, w, hm, qg, kg]
    if rope:
        in_specs += [pl.BlockSpec((tm, LANES), lambda b, i: (i, 0))] * 2
        args += [cos, sin]
    return pl.pallas_call(
        functools.partial(_gqa_proj_kernel, rope=rope, tm=tm),
        grid=(bx, t // tm),
        in_specs=in_specs,
        out_specs=[
            pl.BlockSpec((1, AT_WIDTH, tm), lambda b, i: (b, 0, i)),
            pl.BlockSpec((1, AT_KV_HEADS, tm, HEAD_DIM), lambda b, i: (b, 0, i, 0)),
            pl.BlockSpec((1, AT_KV_HEADS, HEAD_DIM, tm), lambda b, i: (b, 0, 0, i)),
        ],
        out_shape=[
            jax.ShapeDtypeStruct((bx, AT_WIDTH, t), BF),
            jax.ShapeDtypeStruct((bx, AT_KV_HEADS, t, HEAD_DIM), BF),
            jax.ShapeDtypeStruct((bx, AT_KV_HEADS, HEAD_DIM, t), BF),
        ],
        compiler_params=_cparams(("parallel", "parallel"), 48),
        name="gqa_proj_rope" if rope else "gqa_proj_ctx",
    )(*args)


def _flash_kernel(*refs, tk, lookahead, with_latents):
    if with_latents:
        qt_ref, kc_ref, vtc_ref, kl_ref, vtl_ref, ot_ref, m_sc, l_sc, acc_sc = refs
        n_lat = kl_ref.shape[2] // tk
    else:
        qt_ref, kc_ref, vtc_ref, ot_ref, m_sc, l_sc, acc_sc = refs
        n_lat = 0
    q = qt_ref[0]
    chunks = [(kc_ref.at[0, 0], vtc_ref.at[0, 0])]
    chunks += [(kl_ref.at[0, 0, c * tk:(c + 1) * tk, :], vtl_ref.at[0, 0, :, c * tk:(c + 1) * tk]) for c in range(n_lat)]

    def scores(c):
        return jnp.dot(chunks[c][0][...], q, preferred_element_type=F32)

    def weighted(c, s, m):
        p = jnp.exp2(s - m)
        psum = p.reshape(p.shape[0] // SUBLANES, SUBLANES, p.shape[1]).sum(axis=0)
        return jnp.dot(chunks[c][1][...], p.astype(BF), preferred_element_type=F32), psum

    pending = [scores(c) for c in range(min(lookahead, len(chunks)))]
    m0 = jnp.max(pending[0], axis=0, keepdims=True)
    acc = None
    den8 = None
    for c in range(len(chunks)):
        if c + lookahead < len(chunks):
            pending.append(scores(c + lookahead))
        pv, psum = weighted(c, pending.pop(0), m0)
        acc = pv if acc is None else acc + pv
        den8 = psum if den8 is None else den8 + psum
    denom = jnp.sum(den8, axis=0, keepdims=True)
    overflowed = jnp.max(jnp.where(jnp.isfinite(denom), 0.0, 1.0)) > 0.0

    @pl.when(jnp.logical_not(overflowed))
    def _():
        ot_ref[0] = (acc * (1.0 / denom)).astype(BF)

    @pl.when(overflowed)
    def _():
        m_sc[...] = jnp.full(m_sc.shape, NEG_BIG, F32)
        l_sc[...] = jnp.zeros(l_sc.shape, F32)
        acc_sc[...] = jnp.zeros(acc_sc.shape, F32)

        def update(k, vt):
            s = jnp.dot(k, q, preferred_element_type=F32)
            m_prev = m_sc[...]
            m_new = jnp.maximum(m_prev, jnp.max(s, axis=0, keepdims=True))
            alpha = jnp.exp2(m_prev - m_new)
            p = jnp.exp2(s - m_new)
            l_sc[...] = alpha * l_sc[...] + jnp.sum(p, axis=0, keepdims=True)
            acc_sc[...] = alpha * acc_sc[...] + jnp.dot(vt, p.astype(BF), preferred_element_type=F32)
            m_sc[...] = m_new

        update(kc_ref[0, 0], vtc_ref[0, 0])

        def body(c, carry):
            off = pl.multiple_of(c * tk, tk)
            update(kl_ref[0, 0, pl.ds(off, tk), :], vtl_ref[0, 0, :, pl.ds(off, tk)])
            return carry

        if with_latents:
            lax.fori_loop(0, n_lat, body, 0)
        ot_ref[0] = (acc_sc[...] * (1.0 / l_sc[...])).astype(BF)


def _flash(qt, kc, vtc, kl=None, vtl=None, *, tq, tk=256, lookahead=3):
    b, w, t = qt.shape
    kv = kc.shape[1]
    heads = w // HEAD_DIM
    group = heads // kv
    kv_map = lambda bi, h, i: (bi, h // group, 0, 0)
    in_specs = [
        pl.BlockSpec((1, HEAD_DIM, tq), lambda bi, h, i: (bi, h, i)),
        pl.BlockSpec((1, 1) + kc.shape[2:], kv_map),
        pl.BlockSpec((1, 1) + vtc.shape[2:], kv_map),
    ]
    args = [qt, kc, vtc]
    if kl is not None:
        in_specs += [pl.BlockSpec((1, 1) + kl.shape[2:], kv_map), pl.BlockSpec((1, 1) + vtl.shape[2:], kv_map)]
        args += [kl, vtl]
    return pl.pallas_call(
        functools.partial(_flash_kernel, tk=tk, lookahead=lookahead, with_latents=kl is not None),
        grid=(b, heads, t // tq),
        in_specs=in_specs,
        out_specs=pl.BlockSpec((1, HEAD_DIM, tq), lambda bi, h, i: (bi, h, i)),
        out_shape=jax.ShapeDtypeStruct((b, w, t), BF),
        scratch_shapes=[pltpu.VMEM((1, tq), F32), pltpu.VMEM((1, tq), F32), pltpu.VMEM((HEAD_DIM, tq), F32)],
        compiler_params=_cparams(("parallel", "parallel", "parallel"), 48),
        name="gqa_flash" if kl is not None else "gqa_flash_ctx",
    )(*args)


def _oproj_kernel(ot_ref, w_ref, x_ref, mod_ref, o_ref):
    y = lax.dot_general(ot_ref[0], w_ref[...], (((0,), (0,)), ((), ())), preferred_element_type=F32)
    o_ref[0] = x_ref[0] + mod_ref[0, 2:3, :] * y


def _oproj(ot, w, x, mod, *, tm):
    b, t, d = x.shape
    kw = ot.shape[1]
    return pl.pallas_call(
        _oproj_kernel,
        grid=(b, t // tm),
        in_specs=[
            pl.BlockSpec((1, kw, tm), lambda bi, i: (bi, 0, i)),
            pl.BlockSpec((kw, d), lambda bi, i: (0, 0)),
            pl.BlockSpec((1, tm, d), lambda bi, i: (bi, i, 0)),
            pl.BlockSpec((1, N_MOD, d), _mod_map(mod.shape[0])),
        ],
        out_specs=pl.BlockSpec((1, tm, d), lambda bi, i: (bi, i, 0)),
        out_shape=jax.ShapeDtypeStruct((b, t, d), F32),
        compiler_params=_cparams(("parallel", "parallel"), 48),
        name="attn_oproj",
    )(ot, w, x, mod)


def _mlp_kernel(*refs, final, has_delta):
    refs = list(refs)
    x_ref = refs.pop(0)
    d_ref = refs.pop(0) if has_delta else None
    mod_ref, g2_ref, w1_ref, w2_ref = refs[:4]
    fg_ref = refs[4] if final else None
    o_ref, h_sc, acc_sc = refs[-3:]
    j = pl.program_id(2)

    def stream():
        return x_ref[0] + d_ref[0] if has_delta else x_ref[0]

    @pl.when(j == 0)
    def _():
        h_sc[...] = _norm_mod(stream(), g2_ref[...], mod_ref[0, 3:4, :], mod_ref[0, 4:5, :]).astype(BF)
        acc_sc[...] = jnp.zeros(acc_sc.shape, F32)

    a = jnp.dot(h_sc[...], w1_ref[...], preferred_element_type=F32)
    a = jnp.square(jnp.maximum(a, 0.0)).astype(BF)
    acc_sc[...] += jnp.dot(a, w2_ref[...], preferred_element_type=F32)

    @pl.when(j == pl.num_programs(2) - 1)
    def _():
        y = stream() + mod_ref[0, 5:6, :] * acc_sc[...]
        if final:
            ms = jnp.mean(y * y, axis=-1, keepdims=True)
            y = y * lax.rsqrt(ms + EPS) * fg_ref[...]
        o_ref[0] = y


def _mlp(x, mod, g2, w1, w2, layer, final_g=None, delta=None, *, tm, tf=512):
    b, t, d = x.shape
    f = w1.shape[2]
    final = final_g is not None
    row_spec = pl.BlockSpec((1, tm, d), lambda bi, i, j: (bi, i, 0))
    in_specs = [row_spec] + ([row_spec] if delta is not None else []) + [
        pl.BlockSpec((1, N_MOD, d), _mod_map(mod.shape[0])),
        pl.BlockSpec((1, d), lambda bi, i, j: (0, 0)),
        pl.BlockSpec((None, d, tf), lambda bi, i, j: (layer, 0, j)),
        pl.BlockSpec((None, tf, d), lambda bi, i, j: (layer, j, 0)),
    ]
    args = [x] + ([delta] if delta is not None else []) + [mod, g2, w1, w2]
    if final:
        in_specs.append(pl.BlockSpec((1, d), lambda bi, i, j: (0, 0)))
        args.append(final_g)
    return pl.pallas_call(
        functools.partial(_mlp_kernel, final=final, has_delta=delta is not None),
        grid=(b, t // tm, f // tf),
        in_specs=in_specs,
        out_specs=pl.BlockSpec((1, tm, d), lambda bi, i, j: (bi, i, 0)),
        out_shape=jax.ShapeDtypeStruct((b, t, d), F32),
        scratch_shapes=[pltpu.VMEM((tm, d), BF), pltpu.VMEM((tm, d), F32)],
        compiler_params=_cparams(("parallel", "parallel", "arbitrary"), 52),
        name="mlp_final" if final else "mlp",
    )(*args)


def _na_proj_kernel(x_ref, mod_ref, g1_ref, w_ref, qt_ref, k_ref, v_ref):
    h = _norm_mod(x_ref[0], g1_ref[...], mod_ref[0, 0:1, :], mod_ref[0, 1:2, :]).astype(BF)
    qkv = jnp.dot(h, w_ref[...], preferred_element_type=F32)
    for c in range(NA_WIDTH // LANES):
        qt_ref[0, c * LANES:(c + 1) * LANES, :] = (qkv[:, c * LANES:(c + 1) * LANES] * (QK_SCALE * LOG2E)).T.astype(BF)
    k_ref[0] = qkv[:, NA_WIDTH:2 * NA_WIDTH].astype(BF)
    v_ref[0] = qkv[:, 2 * NA_WIDTH:].astype(BF)


def _na_proj(x, mod, g1, w, *, tm=256):
    bx, t, d = x.shape
    return pl.pallas_call(
        _na_proj_kernel,
        grid=(bx, t // tm),
        in_specs=[
            pl.BlockSpec((1, tm, d), lambda b, i: (b, i, 0)),
            pl.BlockSpec((1, N_MOD, d), _mod_map(mod.shape[0])),
            pl.BlockSpec((1, d), lambda b, i: (0, 0)),
            pl.BlockSpec(w.shape, lambda b, i: (0, 0)),
        ],
        out_specs=[
            pl.BlockSpec((1, NA_WIDTH, tm), lambda b, i: (b, 0, i)),
            pl.BlockSpec((1, tm, NA_WIDTH), lambda b, i: (b, i, 0)),
            pl.BlockSpec((1, tm, NA_WIDTH), lambda b, i: (b, i, 0)),
        ],
        out_shape=[
            jax.ShapeDtypeStruct((bx, NA_WIDTH, t), BF),
            jax.ShapeDtypeStruct((bx, t, NA_WIDTH), BF),
            jax.ShapeDtypeStruct((bx, t, NA_WIDTH), BF),
        ],
        compiler_params=_cparams(("parallel", "parallel"), 48),
        name="na_proj",
    )(x, mod, g1, w)


def _na_key_base(g, n_groups):
    return jnp.clip(NA_QROWS * g - NA_WIN_ROWS // 2, 0, NA_QROWS * n_groups - NA_KROWS)


def _na_kernel(qt_ref, k_ref, v_ref, kc_ref, vc_ref, quad_ref, mask_ref, ot_ref, *, n_groups):
    g = pl.program_id(2)
    kb_row = _na_key_base(g, n_groups)
    e0 = kb_row - NA_QROWS * g + NA_WIN_ROWS
    kb = pl.multiple_of(kb_row * GRID_W, GRID_W)
    q2 = qt_ref[0]
    upper = lax.broadcasted_iota(jnp.int32, q2.shape, 0) < HEAD_DIM
    qh = [jnp.where(upper, q2, jnp.zeros_like(q2)), jnp.where(upper, jnp.zeros_like(q2), q2)]
    tn = (((0,), (0,)), ((), ()))
    ck = NA_CHUNK_ROWS * GRID_W
    n_nb = NA_KROWS // NA_CHUNK_ROWS

    def scores(half, j):
        if j < 0:
            return jnp.dot(kc_ref[0], qh[half], preferred_element_type=F32)
        bias = jnp.concatenate([quad_ref[half, e0 + NA_CHUNK_ROWS * j + i] for i in range(NA_CHUNK_ROWS)], axis=0)
        bias = bias + mask_ref[0, j * ck:(j + 1) * ck, :]
        return jnp.dot(k_ref[0, pl.ds(kb + j * ck, ck), :], qh[half], preferred_element_type=F32) + bias

    def values(j):
        return vc_ref[0] if j < 0 else v_ref[0, pl.ds(kb + j * ck, ck), :]

    units = [(half, j) for j in range(-1, n_nb) for half in range(2)]
    lookahead = 6
    pending = [scores(*u) for u in units[:lookahead]]
    m0 = [jnp.max(s, axis=0, keepdims=True) for s in pending[:2]]
    acc = [None, None]
    den = [None, None]
    for n, (half, j) in enumerate(units):
        if n + lookahead < len(units):
            pending.append(scores(*units[n + lookahead]))
        p = jnp.exp2(pending.pop(0) - m0[half])
        pv = lax.dot_general(values(j), p.astype(BF), tn, preferred_element_type=F32)
        ps = jnp.sum(p, axis=0, keepdims=True)
        acc[half] = pv if acc[half] is None else acc[half] + pv
        den[half] = ps if den[half] is None else den[half] + ps
    bad = jnp.where(jnp.isfinite(den[0]) & jnp.isfinite(den[1]), 0.0, 1.0)
    overflowed = jnp.max(bad) > 0.0

    @pl.when(jnp.logical_not(overflowed))
    def _():
        ot_ref[0] = jnp.where(upper, acc[0] * (1.0 / den[0]), acc[1] * (1.0 / den[1])).astype(BF)

    @pl.when(overflowed)
    def _():
        outs = []
        for half in range(2):
            s_all = [scores(half, j) for j in range(-1, n_nb)]
            m = s_all[0].max(axis=0, keepdims=True)
            for s in s_all[1:]:
                m = jnp.maximum(m, jnp.max(s, axis=0, keepdims=True))
            o = None
            l = None
            for j, s in zip(range(-1, n_nb), s_all):
                p = jnp.exp2(s - m)
                pv = lax.dot_general(values(j), p.astype(BF), tn, preferred_element_type=F32)
                ps = jnp.sum(p, axis=0, keepdims=True)
                o = pv if o is None else o + pv
                l = ps if l is None else l + ps
            outs.append(o * (1.0 / l))
        ot_ref[0] = jnp.where(upper, outs[0], outs[1]).astype(BF)


def _na_bias_tables(rpb, n_rows):
    n_groups = n_rows // NA_QROWS
    n_h, n_dr, n_dc = rpb.shape
    m = n_dc + 2 * GRID_W
    c0 = GRID_W + NA_WIN_COLS - 1
    flat = jnp.tile(jnp.pad(rpb, ((0, 0), (0, 0), (GRID_W, GRID_W))), (1, 1, GRID_W + 1))
    t = flat[:, :, c0:c0 + GRID_W * (m - 1)].reshape(n_h, n_dr, GRID_W, m - 1)[..., :GRID_W]
    qc = np.arange(GRID_W)[None, :]
    kc = np.arange(GRID_W)[:, None]
    cs = np.clip(qc - NA_WIN_COLS // 2, 0, GRID_W - NA_WIN_COLS)
    col_ok = (kc >= cs) & (kc < cs + NA_WIN_COLS)
    tc = jnp.where(col_ok, jnp.swapaxes(t, -1, -2) * LOG2E, NEG_BIG)
    back = NA_EROWS - 1 - n_dr
    tcp = jnp.pad(tc, ((0, 0), (NA_QROWS, back), (0, 0), (0, 0)))
    quad = jnp.concatenate([tcp[:, NA_QROWS - 1 - a:NA_QROWS - 1 - a + NA_EROWS] for a in range(NA_QROWS)], axis=-1)
    masks = []
    for g in (0, 1, n_groups - 1):
        r = NA_QROWS * g + np.arange(NA_QROWS)[None, None, :, None]
        kb = int(np.clip(NA_QROWS * g - NA_WIN_ROWS // 2, 0, n_rows - NA_KROWS))
        krow = kb + np.arange(NA_KROWS)[:, None, None, None]
        rs = np.clip(r - NA_WIN_ROWS // 2, 0, n_rows - NA_WIN_ROWS)
        ok = np.broadcast_to((krow >= rs) & (krow < rs + NA_WIN_ROWS), (NA_KROWS, GRID_W, NA_QROWS, GRID_W))
        masks.append(np.where(ok, 0.0, NEG_BIG).reshape(NA_KROWS * GRID_W, NA_QROWS * GRID_W))
    return quad.astype(F32), jnp.asarray(np.stack(masks), F32)


def _na_attend(qt, k, v, kc, vc, quad, rowmask, n_rows):
    b, w, s = qt.shape
    c = kc.shape[1]
    n_groups = n_rows // NA_QROWS
    tq = NA_QROWS * GRID_W
    pair = 2 * HEAD_DIM

    def mask_map(bi, hp, g):
        return (jnp.where(g == 0, 0, jnp.where(g == n_groups - 1, 2, 1)), 0, 0)

    return pl.pallas_call(
        functools.partial(_na_kernel, n_groups=n_groups),
        grid=(b, w // pair, n_groups),
        in_specs=[
            pl.BlockSpec((1, pair, tq), lambda bi, hp, g: (bi, hp, g)),
            pl.BlockSpec((1, s, pair), lambda bi, hp, g: (bi, 0, hp)),
            pl.BlockSpec((1, s, pair), lambda bi, hp, g: (bi, 0, hp)),
            pl.BlockSpec((1, c, pair), lambda bi, hp, g: (bi, 0, hp)),
            pl.BlockSpec((1, c, pair), lambda bi, hp, g: (bi, 0, hp)),
            pl.BlockSpec((2, NA_EROWS, GRID_W, tq), lambda bi, hp, g: (hp, 0, 0, 0)),
            pl.BlockSpec((1, NA_KROWS * GRID_W, tq), mask_map),
        ],
        out_specs=pl.BlockSpec((1, pair, tq), lambda bi, hp, g: (bi, hp, g)),
        out_shape=jax.ShapeDtypeStruct((b, w, s), BF),
        compiler_params=_cparams(("parallel", "parallel", "arbitrary"), 48),
        name="na_attend",
    )(qt, k, v, kc, vc, quad, rowmask)


def _conv_pw1_kernel(x_ref, mod_ref, g1_ref, w_ref, b_ref, u_ref):
    h = _norm_mod(x_ref[0], g1_ref[...], mod_ref[0, 0:1, :], mod_ref[0, 1:2, :]).astype(BF)
    ag = jnp.dot(h, w_ref[...], preferred_element_type=F32) + b_ref[...]
    d = u_ref.shape[2]
    u_ref[0] = ag[:, :d] * jax.nn.sigmoid(ag[:, d:])


def _conv_pw1(x, mod, g1, w, b, *, tm=512):
    bx, t, d = x.shape
    return pl.pallas_call(
        _conv_pw1_kernel,
        grid=(bx, t // tm),
        in_specs=[
            pl.BlockSpec((1, tm, d), lambda bi, i: (bi, i, 0)),
            pl.BlockSpec((1, N_MOD, d), _mod_map(mod.shape[0])),
            pl.BlockSpec((1, d), lambda bi, i: (0, 0)),
            pl.BlockSpec(w.shape, lambda bi, i: (0, 0)),
            pl.BlockSpec((1, 2 * d), lambda bi, i: (0, 0)),
        ],
        out_specs=pl.BlockSpec((1, tm, d), lambda bi, i: (bi, i, 0)),
        out_shape=jax.ShapeDtypeStruct((bx, t, d), F32),
        compiler_params=_cparams(("parallel", "parallel"), 48),
        name="conv_pw1_glu",
    )(x, mod, g1, w, b)


def _conv_tail_kernel(u_ref, up_ref, un_ref, wdw_ref, bdw_ref, lg_ref, lb_ref, w2_ref, b2_ref, x_ref, mod_ref,
                      o_ref, buf_sc, cv_sc, *, tm):
    i = pl.program_id(1)
    last = pl.num_programs(1) - 1
    buf_sc[0:CONV_HALO, :] = jnp.where(i > 0, up_ref[0], 0.0)
    buf_sc[CONV_HALO:CONV_HALO + tm, :] = u_ref[0]
    buf_sc[CONV_HALO + tm:, :] = jnp.where(i < last, un_ref[0], 0.0)
    assert CONV_HALO - CONV_WIDTH // 2 == 1
    d = buf_sc.shape[1]
    for lb in range(d // CONV_LANES):
        ls = slice(lb * CONV_LANES, (lb + 1) * CONV_LANES)
        for rc in range(tm // CONV_ROWS):
            r0 = rc * CONV_ROWS
            acc = None
            for r in range(SUBLANES):
                part = None
                for j in range(r, CONV_WIDTH + 1, SUBLANES):
                    if j == 0:
                        continue
                    rows = slice(r0 + j - r, r0 + j - r + CONV_ROWS + SUBLANES)
                    term = buf_sc[rows, ls] * wdw_ref[j - 1:j, ls]
                    part = term if part is None else part + term
                part = part[r:r + CONV_ROWS]
                acc = part if acc is None else acc + part
            cv_sc[r0:r0 + CONV_ROWS, ls] = acc
    u = cv_sc[...] + bdw_ref[...]
    mu = jnp.mean(u, axis=-1, keepdims=True)
    uc = u - mu
    var = jnp.mean(uc * uc, axis=-1, keepdims=True)
    y = uc * lax.rsqrt(var + EPS) * lg_ref[...] + lb_ref[...]
    y = (y * jax.nn.sigmoid(y)).astype(BF)
    z = jnp.dot(y, w2_ref[...], preferred_element_type=F32) + b2_ref[...]
    o_ref[0] = x_ref[0] + mod_ref[0, 2:3, :] * z


def _conv_tail(u, wdw, bdw, lg, lb, w2, b2, x, mod, *, tm=256):
    bx, t, d = x.shape
    hb = tm // CONV_HALO
    n_halo = t // CONV_HALO
    vec = lambda: pl.BlockSpec((1, d), lambda bi, i: (0, 0))
    return pl.pallas_call(
        functools.partial(_conv_tail_kernel, tm=tm),
        grid=(bx, t // tm),
        in_specs=[
            pl.BlockSpec((1, tm, d), lambda bi, i: (bi, i, 0)),
            pl.BlockSpec((1, CONV_HALO, d), lambda bi, i: (bi, jnp.maximum(i * hb - 1, 0), 0)),
            pl.BlockSpec((1, CONV_HALO, d), lambda bi, i: (bi, jnp.minimum((i + 1) * hb, n_halo - 1), 0)),
            pl.BlockSpec(wdw.shape, lambda bi, i: (0, 0)),
            vec(), vec(), vec(),
            pl.BlockSpec(w2.shape, lambda bi, i: (0, 0)),
            vec(),
            pl.BlockSpec((1, tm, d), lambda bi, i: (bi, i, 0)),
            pl.BlockSpec((1, N_MOD, d), _mod_map(mod.shape[0])),
        ],
        out_specs=pl.BlockSpec((1, tm, d), lambda bi, i: (bi, i, 0)),
        out_shape=jax.ShapeDtypeStruct((bx, t, d), F32),
        scratch_shapes=[pltpu.VMEM((tm + 2 * CONV_HALO, d), F32), pltpu.VMEM((tm, d), F32)],
        compiler_params=_cparams(("parallel", "parallel"), 48),
        name="conv_tail",
    )(u, u, u, wdw, bdw, lg, lb, w2, b2, x, mod)


def _ft_tables(n_seq, gw):
    nb = n_seq // FT_NA
    ka = np.arange(FT_NA)[None, :, None]
    na = np.arange(FT_NA)[None, None, :]
    jb = np.arange(nb)[:, None, None]
    ph = 2 * np.pi * ((ka * (nb * na + jb)) % n_seq) / n_seq
    t1 = np.concatenate([np.cos(ph), -np.sin(ph)], axis=1)
    kb = np.arange(nb)
    ph = 2 * np.pi * ((kb[:, None] * kb[None, :]) % nb) / nb
    c3, s3 = np.cos(ph), np.sin(ph)
    t2 = np.block([[c3, s3], [-s3, c3]]) / np.sqrt(n_seq)
    m = np.arange(gw)
    ph = 2 * np.pi * ((m[:, None] * m[None, :]) % gw) / gw
    tw = np.stack([np.cos(ph), np.sin(ph)]) / np.sqrt(gw)
    return tuple(jnp.asarray(t, F32).astype(BF) for t in (t1, t2, tw))


def _ft_prep_kernel(x_ref, mod_ref, g1_ref, h_ref, *, nb):
    h = _norm_mod(x_ref[0], g1_ref[...], mod_ref[0, 0:1, :], mod_ref[0, 1:2, :])
    for a in range(h.shape[0] // nb):
        h_ref[0, :, a, :] = h[a * nb:(a + 1) * nb, :]


def _ft_prep(x, mod, g1, *, tm=1024):
    bx, t, d = x.shape
    nb = t // FT_NA
    return pl.pallas_call(
        functools.partial(_ft_prep_kernel, nb=nb),
        grid=(bx, t // tm),
        in_specs=[
            pl.BlockSpec((1, tm, d), lambda bi, i: (bi, i, 0)),
            pl.BlockSpec((1, N_MOD, d), _mod_map(mod.shape[0])),
            pl.BlockSpec((1, d), lambda bi, i: (0, 0)),
        ],
        out_specs=pl.BlockSpec((1, nb, tm // nb, d), lambda bi, i: (bi, 0, i, 0)),
        out_shape=jax.ShapeDtypeStruct((bx, nb, FT_NA, d), F32),
        compiler_params=_cparams(("parallel", "parallel"), 48),
        name="ft_prep",
    )(x, mod, g1)


def _ft_seq1_kernel(h_ref, t1_ref, y_ref, *, tb, d):
    for j in range(tb):
        y = jnp.dot(t1_ref[j], h_ref[0, j].astype(BF), preferred_element_type=F32)
        y_ref[0, :, j, :d] = y[:FT_NA]
        y_ref[0, :, j, d:] = y[FT_NA:]


def _ft_seq1(hp, t1, *, tb=8):
    bx, nb, _, d = hp.shape
    return pl.pallas_call(
        functools.partial(_ft_seq1_kernel, tb=tb, d=d),
        grid=(bx, nb // tb),
        in_specs=[
            pl.BlockSpec((1, tb, FT_NA, d), lambda bi, i: (bi, i, 0, 0)),
            pl.BlockSpec((tb, 2 * FT_NA, FT_NA), lambda bi, i: (i, 0, 0)),
        ],
        out_specs=pl.BlockSpec((1, FT_NA, tb, 2 * d), lambda bi, i: (bi, 0, i, 0)),
        out_shape=jax.ShapeDtypeStruct((bx, FT_NA, nb, 2 * d), F32),
        compiler_params=_cparams(("parallel", "parallel"), 48),
        name="ft_seq1",
    )(hp, t1)


def _ft_seq2_kernel(y_ref, t2_ref, tw_ref, w_ref, b_ref, mod_ref, o_ref, *, ta):
    d = w_ref.shape[0]
    nb = y_ref.shape[2]
    gw = tw_ref.shape[1]
    for a in range(ta):
        yb = y_ref[0, a]
        rhs = jnp.concatenate([yb[:, :d], yb[:, d:]], axis=0).astype(BF)
        aa = jnp.dot(t2_ref[...], rhs, preferred_element_type=F32).astype(BF)
        z = [jnp.dot(aa[:nb, g * gw:(g + 1) * gw], tw_ref[0], preferred_element_type=F32)
             + jnp.dot(aa[nb:, g * gw:(g + 1) * gw], tw_ref[1], preferred_element_type=F32)
             for g in range(d // gw)]
        z = jnp.concatenate(z, axis=1).astype(BF)
        yl = jnp.dot(z, w_ref[...], preferred_element_type=F32) + b_ref[...]
        o_ref[0, :, a, :] = mod_ref[0, 2:3, :] * yl


def _ft_seq2(y, t2, tw, w, b, mod, *, ta=8):
    bx, _, nb, d2 = y.shape
    d = d2 // 2
    out = pl.pallas_call(
        functools.partial(_ft_seq2_kernel, ta=ta),
        grid=(bx, FT_NA // ta),
        in_specs=[
            pl.BlockSpec((1, ta, nb, d2), lambda bi, i: (bi, i, 0, 0)),
            pl.BlockSpec(t2.shape, lambda bi, i: (0, 0)),
            pl.BlockSpec(tw.shape, lambda bi, i: (0, 0, 0)),
            pl.BlockSpec(w.shape, lambda bi, i: (0, 0)),
            pl.BlockSpec((1, d), lambda bi, i: (0, 0)),
            pl.BlockSpec((1, N_MOD, d), _mod_map(mod.shape[0])),
        ],
        out_specs=pl.BlockSpec((1, nb, ta, d), lambda bi, i: (bi, 0, i, 0)),
        out_shape=jax.ShapeDtypeStruct((bx, nb, FT_NA, d), F32),
        compiler_params=_cparams(("parallel", "parallel"), 48),
        name="ft_seq2_mix",
    )(y, t2, tw, w, b, mod)
    return out.reshape(bx, nb * FT_NA, d)


def _rope_tables(n_tok):
    t = jnp.arange(n_tok)
    row = (t // GRID_W).astype(F32)
    col = (t % GRID_W).astype(F32)
    n_axis = HEAD_DIM // 4
    inv = ROPE_THETA ** (-jnp.arange(n_axis, dtype=F32) / n_axis)
    ang = jnp.concatenate([row[:, None] * inv, col[:, None] * inv], axis=-1)
    ang = jnp.tile(jnp.repeat(ang, 2, axis=-1), (1, LANES // HEAD_DIM))
    sign = jnp.where(jnp.arange(LANES) % 2 == 0, -1.0, 1.0).astype(F32)
    return jnp.cos(ang), jnp.sin(ang) * sign


def _row(v):
    return v.reshape(1, -1)


def kernel(x, c, ctx, c_ctx, ada_w, ada_b, norm1_g, norm2_g, mlp_w1, mlp_w2, final_g, at_w_qkv, at_q_g, at_k_g, at_w_o, na_w_qkv, na_rpb, na_w_o, cv_w_pw1, cv_b_pw1, cv_w_dw, cv_b_dw, cv_ln_g, cv_ln_b, cv_w_pw2, cv_b_pw2, ft_w, ft_b):
    bsz, n_lat, d = x.shape
    depth = ada_w.shape[0]
    n_rows = n_lat // GRID_W
    n_ctx = ctx.shape[1]

    vec8 = jnp.zeros((8, d), F32).at[:bsz].set(c).at[bsz].set(c_ctx)
    mods = _ada_mods(vec8, ada_w, ada_b)

    w1m = mlp_w1.astype(BF)
    w2m = mlp_w2.astype(BF)
    hm = jnp.asarray(np.kron(np.eye(AT_HEADS), np.full((HEAD_DIM, HEAD_DIM), 1.0 / HEAD_DIM)), BF)
    h_ctx = ctx
    for i in range(depth):
        kind = i % N_MIXERS
        occ = i // N_MIXERS
        ctx_later = any((j % N_MIXERS) in (0, 1) for j in range(i + 1, depth))
        ml = mods[i, :bsz].reshape(bsz, N_MOD, d)
        mc = mods[i, bsz:bsz + 1].reshape(1, N_MOD, d)
        g1 = _row(norm1_g[i])
        delta = None
        if kind == 0:
            w = at_w_qkv[occ].astype(BF)
            wo = at_w_o[occ].astype(BF)
            qg = _row(jnp.tile(at_q_g[occ], AT_HEADS))
            kg = _row(jnp.tile(at_k_g[occ], AT_KV_HEADS))
            cos, sin = _rope_tables(n_lat)
            qt_l, k_l, vt_l = _gqa_proj(x, ml, g1, w, hm, qg, kg, cos, sin)
            qt_c, k_c, vt_c = _gqa_proj(h_ctx, mc, g1, w, hm, qg, kg, None, None)
            ot_l = _flash(qt_l, k_c, vt_c, k_l, vt_l, tq=512)
            x = _oproj(ot_l, wo, x, ml, tm=512)
            if ctx_later:
                ot_c = _flash(qt_c, k_c, vt_c, tq=n_ctx)
                h_ctx = _oproj(ot_c, wo, h_ctx, mc, tm=n_ctx)
        elif kind == 1:
            w = na_w_qkv[occ].astype(BF)
            wo = na_w_o[occ].astype(BF)
            qt_l, k_l, v_l = _na_proj(x, ml, g1, w)
            qt_c, k_c, v_c = _na_proj(h_ctx, mc, g1, w)
            quad, rowmask = _na_bias_tables(na_rpb[occ], n_rows)
            ot_l = _na_attend(qt_l, k_l, v_l, k_c, v_c, quad, rowmask, n_rows)
            x = _oproj(ot_l, wo, x, ml, tm=512)
            if ctx_later:
                raise NotImplementedError("context output of a neighbourhood layer is not needed at this depth")
        elif kind == 2:
            w1 = cv_w_pw1[occ].astype(BF)
            w2 = cv_w_pw2[occ].astype(BF)
            wdw = jnp.zeros((CONV_WIDTH + 1, d), F32).at[:CONV_WIDTH].set(cv_w_dw[occ])
            cv = (wdw, _row(cv_b_dw[occ]), _row(cv_ln_g[occ]), _row(cv_ln_b[occ]), w2, _row(cv_b_pw2[occ]))
            u = _conv_pw1(x, ml, g1, w1, _row(cv_b_pw1[occ]))
            x = _conv_tail(u, *cv, x, ml)
            if ctx_later:
                raise NotImplementedError("context output of a convolution layer is not needed at this depth")
        else:
            t1, t2, tw = _ft_tables(n_lat, d // FT_GROUPS)
            y1 = _ft_seq1(_ft_prep(x, ml, g1), t1)
            delta = _ft_seq2(y1, t2, tw, ft_w[occ].astype(BF), _row(ft_b[occ]), ml)
            if ctx_later:
                raise NotImplementedError("context output of a Fourier layer is not needed at this depth")
        g2 = _row(norm2_g[i])
        is_last = i == depth - 1
        x = _mlp(x, ml, g2, w1m, w2m, i, _row(final_g) if is_last else None, delta, tm=1024, tf=1024)
        if ctx_later:
            h_ctx = _mlp(h_ctx, mc, g2, w1m, w2m, i, tm=n_ctx)
    return x
```

```python
import functools

import jax
import jax.numpy as jnp
import numpy as np
from jax import lax
from jax.experimental import pallas as pl
from jax.experimental.pallas import tpu as pltpu

GRID_W = 64
N_MIXERS = 4
HEAD_DIM = 64
AT_HEADS = 16
AT_KV_HEADS = 4
AT_WIDTH = AT_HEADS * HEAD_DIM
AT_KV_WIDTH = AT_KV_HEADS * HEAD_DIM
ROPE_THETA = 10000.0
NA_HEADS = 16
NA_WIDTH = NA_HEADS * HEAD_DIM
NA_WIN_ROWS = 8
NA_WIN_COLS = 16
CONV_WIDTH = 31
FT_GROUPS = 4
N_MOD = 6
EPS = 1e-6
QK_SCALE = HEAD_DIM ** -0.5
LOG2E = float(np.log2(np.e))

LANES = 128
NA_QROWS = 4
NA_KROWS = 12
NA_CHUNK_ROWS = 4
NA_GROUPS_PER_STEP = 4
NA_LOOKAHEAD = 6
NA_EROWS = NA_KROWS + 2 * NA_QROWS
CONV_HALO = 16
CONV_ROWS = 64
CONV_LANES = 256
SUBLANES = 8
FT_NA = 64
NEG_BIG = -1e30

BF = jnp.bfloat16
F32 = jnp.float32


def _cparams(sem, vmem_mib):
    return pltpu.CompilerParams(dimension_semantics=sem, vmem_limit_bytes=vmem_mib << 20)


def _norm_mod(x, g, shift, scale):
    ms = jnp.mean(x * x, axis=-1, keepdims=True)
    return (x * lax.rsqrt(ms + EPS) * g) * (1.0 + scale) + shift


def _ada_kernel(v_ref, w_ref, b_ref, o_ref):
    v = v_ref[...]
    sv = v * jax.nn.sigmoid(v)
    o_ref[0] = jnp.dot(sv, w_ref[0], preferred_element_type=F32, precision=lax.Precision.HIGHEST) + b_ref[0]


def _ada_mods(vec8, ada_w, ada_b):
    depth, d, n = ada_w.shape
    tn = n // 4
    return pl.pallas_call(
        _ada_kernel,
        grid=(depth, n // tn),
        in_specs=[
            pl.BlockSpec((8, d), lambda l, j: (0, 0)),
            pl.BlockSpec((1, d, tn), lambda l, j: (l, 0, j)),
            pl.BlockSpec((1, 1, tn), lambda l, j: (l, 0, j)),
        ],
        out_specs=pl.BlockSpec((1, 8, tn), lambda l, j: (l, 0, j)),
        out_shape=jax.ShapeDtypeStruct((depth, 8, n), F32),
        compiler_params=_cparams(("arbitrary", "arbitrary"), 40),
        name="ada_mods",
    )(vec8, ada_w, ada_b.reshape(depth, 1, n))


def _mod_map(n_mod_rows):
    if n_mod_rows == 1:
        return lambda b, *_: (0, 0, 0)
    return lambda b, *_: (b, 0, 0)


def _gqa_proj_kernel(*refs, rope, tm):
    if rope:
        x_ref, mod_ref, g1_ref, w_ref, hm_ref, qg_ref, kg_ref, cos_ref, sin_ref, qt_ref, k_ref, vt_ref = refs
    else:
        x_ref, mod_ref, g1_ref, w_ref, hm_ref, qg_ref, kg_ref, qt_ref, k_ref, vt_ref = refs
    h = _norm_mod(x_ref[0], g1_ref[...], mod_ref[0, 0:1, :], mod_ref[0, 1:2, :]).astype(BF)
    qkv = jnp.dot(h, w_ref[...], preferred_element_type=F32)
    q = qkv[:, :AT_WIDTH]
    k = qkv[:, AT_WIDTH:AT_WIDTH + AT_KV_WIDTH]
    v = qkv[:, AT_WIDTH + AT_KV_WIDTH:]
    hm = hm_ref[...]
    q = q * lax.rsqrt(jnp.dot((q * q).astype(BF), hm, preferred_element_type=F32) + EPS) * qg_ref[...]
    hk = hm[:AT_KV_WIDTH, :AT_KV_WIDTH]
    k = k * lax.rsqrt(jnp.dot((k * k).astype(BF), hk, preferred_element_type=F32) + EPS) * kg_ref[...]
    if rope:
        cos = cos_ref[...]
        sin = sin_ref[...]
        even = (lax.broadcasted_iota(jnp.int32, (tm, LANES), 1) % 2) == 0

        def rot(t):
            partner = jnp.where(even, pltpu.roll(t, LANES - 1, 1), pltpu.roll(t, 1, 1))
            return t * cos + partner * sin
    else:
        def rot(t):
            return t
    for c in range(AT_WIDTH // LANES):
        qc = rot(q[:, c * LANES:(c + 1) * LANES]) * (QK_SCALE * LOG2E)
        qt_ref[0, c * LANES:(c + 1) * LANES, :] = qc.T.astype(BF)
    for c in range(AT_KV_WIDTH // LANES):
        kc = rot(k[:, c * LANES:(c + 1) * LANES]).astype(BF)
        k_ref[0, 2 * c] = kc[:, :HEAD_DIM]
        k_ref[0, 2 * c + 1] = kc[:, HEAD_DIM:]
    vt = v.T.astype(BF)
    for h in range(AT_KV_HEADS):
        vt_ref[0, h] = vt[h * HEAD_DIM:(h + 1) * HEAD_DIM]


def _gqa_proj(x, mod, g1, w, hm, qg, kg, cos, sin, *, tm=256):
    bx, t, d = x.shape
    rope = cos is not None
    n = w.shape[1]
    const = lambda b, i: (0, 0)
    in_specs = [
        pl.BlockSpec((1, tm, d), lambda b, i: (b, i, 0)),
        pl.BlockSpec((1, N_MOD, d), _mod_map(mod.shape[0])),
        pl.BlockSpec((1, d), const),
        pl.BlockSpec((d, n), const),
        pl.BlockSpec(hm.shape, const),
        pl.BlockSpec((1, AT_WIDTH), const),
        pl.BlockSpec((1, AT_KV_WIDTH), const),
    ]
    args = [x, mod, g1, w, hm, qg, kg]
    if rope:
        in_specs += [pl.BlockSpec((tm, LANES), lambda b, i: (i, 0))] * 2
        args += [cos, sin]
    return pl.pallas_call(
        functools.partial(_gqa_proj_kernel, rope=rope, tm=tm),
        grid=(bx, t // tm),
        in_specs=in_specs,
        out_specs=[
            pl.BlockSpec((1, AT_WIDTH, tm), lambda b, i: (b, 0, i)),
            pl.BlockSpec((1, AT_KV_HEADS, tm, HEAD_DIM), lambda b, i: (b, 0, i, 0)),
            pl.BlockSpec((1, AT_KV_HEADS, HEAD_DIM, tm), lambda b, i: (b, 0, 0, i)),
        ],
        out_shape=[
            jax.ShapeDtypeStruct((bx, AT_WIDTH, t), BF),
            jax.ShapeDtypeStruct((bx, AT_KV_HEADS, t, HEAD_DIM), BF),
            jax.ShapeDtypeStruct((bx, AT_KV_HEADS, HEAD_DIM, t), BF),
        ],
        compiler_params=_cparams(("parallel", "parallel"), 48),
        name="gqa_proj_rope" if rope else "gqa_proj_ctx",
    )(*args)


def _flash_kernel(*refs, tk, lookahead, with_latents):
    if with_latents:
        qt_ref, kc_ref, vtc_ref, kl_ref, vtl_ref, ot_ref, m_sc, l_sc, acc_sc = refs
        n_lat = kl_ref.shape[2] // tk
    else:
        qt_ref, kc_ref, vtc_ref, ot_ref, m_sc, l_sc, acc_sc = refs
        n_lat = 0
    q = qt_ref[0]
    chunks = [(kc_ref.at[0, 0], vtc_ref.at[0, 0])]
    chunks += [(kl_ref.at[0, 0, c * tk:(c + 1) * tk, :], vtl_ref.at[0, 0, :, c * tk:(c + 1) * tk]) for c in range(n_lat)]

    def scores(c):
        return jnp.dot(chunks[c][0][...], q, preferred_element_type=F32)

    def weighted(c, s, m):
        p = jnp.exp2(s - m)
        psum = p.reshape(p.shape[0] // SUBLANES, SUBLANES, p.shape[1]).sum(axis=0)
        return jnp.dot(chunks[c][1][...], p.astype(BF), preferred_element_type=F32), psum

    pending = [scores(c) for c in range(min(lookahead, len(chunks)))]
    m0 = jnp.max(pending[0], axis=0, keepdims=True)
    acc = None
    den8 = None
    for c in range(len(chunks)):
        if c + lookahead < len(chunks):
            pending.append(scores(c + lookahead))
        pv, psum = weighted(c, pending.pop(0), m0)
        acc = pv if acc is None else acc + pv
        den8 = psum if den8 is None else den8 + psum
    denom = jnp.sum(den8, axis=0, keepdims=True)
    overflowed = jnp.max(jnp.where(jnp.isfinite(denom), 0.0, 1.0)) > 0.0

    @pl.when(jnp.logical_not(overflowed))
    def _():
        ot_ref[0] = (acc * (1.0 / denom)).astype(BF)

    @pl.when(overflowed)
    def _():
        m_sc[...] = jnp.full(m_sc.shape, NEG_BIG, F32)
        l_sc[...] = jnp.zeros(l_sc.shape, F32)
        acc_sc[...] = jnp.zeros(acc_sc.shape, F32)

        def update(k, vt):
            s = jnp.dot(k, q, preferred_element_type=F32)
            m_prev = m_sc[...]
            m_new = jnp.maximum(m_prev, jnp.max(s, axis=0, keepdims=True))
            alpha = jnp.exp2(m_prev - m_new)
            p = jnp.exp2(s - m_new)
            l_sc[...] = alpha * l_sc[...] + jnp.sum(p, axis=0, keepdims=True)
            acc_sc[...] = alpha * acc_sc[...] + jnp.dot(vt, p.astype(BF), preferred_element_type=F32)
            m_sc[...] = m_new

        update(kc_ref[0, 0], vtc_ref[0, 0])

        def body(c, carry):
            off = pl.multiple_of(c * tk, tk)
            update(kl_ref[0, 0, pl.ds(off, tk), :], vtl_ref[0, 0, :, pl.ds(off, tk)])
            return carry

        if with_latents:
            lax.fori_loop(0, n_lat, body, 0)
        ot_ref[0] = (acc_sc[...] * (1.0 / l_sc[...])).astype(BF)


def _flash(qt, kc, vtc, kl=None, vtl=None, *, tq, tk=256, lookahead=3):
    b, w, t = qt.shape
    kv = kc.shape[1]
    heads = w // HEAD_DIM
    group = heads // kv
    kv_map = lambda bi, h, i: (bi, h // group, 0, 0)
    in_specs = [
        pl.BlockSpec((1, HEAD_DIM, tq), lambda bi, h, i: (bi, h, i)),
        pl.BlockSpec((1, 1) + kc.shape[2:], kv_map),
        pl.BlockSpec((1, 1) + vtc.shape[2:], kv_map),
    ]
    args = [qt, kc, vtc]
    if kl is not None:
        in_specs += [pl.BlockSpec((1, 1) + kl.shape[2:], kv_map), pl.BlockSpec((1, 1) + vtl.shape[2:], kv_map)]
        args += [kl, vtl]
    return pl.pallas_call(
        functools.partial(_flash_kernel, tk=tk, lookahead=lookahead, with_latents=kl is not None),
        grid=(b, heads, t // tq),
        in_specs=in_specs,
        out_specs=pl.BlockSpec((1, HEAD_DIM, tq), lambda bi, h, i: (bi, h, i)),
        out_shape=jax.ShapeDtypeStruct((b, w, t), BF),
        scratch_shapes=[pltpu.VMEM((1, tq), F32), pltpu.VMEM((1, tq), F32), pltpu.VMEM((HEAD_DIM, tq), F32)],
        compiler_params=_cparams(("parallel", "parallel", "parallel"), 48),
        name="gqa_flash" if kl is not None else "gqa_flash_ctx",
    )(*args)


def _mlp_kernel(*refs, final, pending):
    refs = list(refs)
    x_ref = refs.pop(0)
    d_ref = refs.pop(0) if pending == "delta" else None
    ot_ref, wo_ref = (refs.pop(0), refs.pop(0)) if pending == "attn" else (None, None)
    mod_ref, g2_ref, w1_ref, w2_ref = refs[:4]
    fg_ref = refs[4] if final else None
    o_ref, xs_sc, h_sc, acc_sc = refs[-4:]
    j = pl.program_id(2)

    @pl.when(j == 0)
    def _():
        xs = x_ref[0]
        if pending == "delta":
            xs = xs + d_ref[0]
        elif pending == "attn":
            y = lax.dot_general(ot_ref[0], wo_ref[...], (((0,), (0,)), ((), ())), preferred_element_type=F32)
            xs = xs + mod_ref[0, 2:3, :] * y
        xs_sc[...] = xs
        h_sc[...] = _norm_mod(xs, g2_ref[...], mod_ref[0, 3:4, :], mod_ref[0, 4:5, :]).astype(BF)
        acc_sc[...] = jnp.zeros(acc_sc.shape, F32)

    a = jnp.dot(h_sc[...], w1_ref[...], preferred_element_type=F32)
    a = jnp.square(jnp.maximum(a, 0.0)).astype(BF)
    acc_sc[...] += jnp.dot(a, w2_ref[...], preferred_element_type=F32)

    @pl.when(j == pl.num_programs(2) - 1)
    def _():
        y = xs_sc[...] + mod_ref[0, 5:6, :] * acc_sc[...]
        if final:
            ms = jnp.mean(y * y, axis=-1, keepdims=True)
            y = y * lax.rsqrt(ms + EPS) * fg_ref[...]
        o_ref[0] = y


def _mlp(x, mod, g2, w1, w2, layer, final_g=None, *, delta=None, attn=None, tm, tf=512):
    b, t, d = x.shape
    f = w1.shape[2]
    final = final_g is not None
    row_spec = pl.BlockSpec((1, tm, d), lambda bi, i, j: (bi, i, 0))
    in_specs, args, pending = [row_spec], [x], None
    if delta is not None:
        in_specs, args, pending = in_specs + [row_spec], args + [delta], "delta"
    if attn is not None:
        ot, wo = attn
        in_specs += [pl.BlockSpec((1, ot.shape[1], tm), lambda bi, i, j: (bi, 0, i)),
                     pl.BlockSpec(wo.shape, lambda bi, i, j: (0, 0))]
        args, pending = args + [ot, wo], "attn"
    in_specs += [
        pl.BlockSpec((1, N_MOD, d), _mod_map(mod.shape[0])),
        pl.BlockSpec((1, d), lambda bi, i, j: (0, 0)),
        pl.BlockSpec((None, d, tf), lambda bi, i, j: (layer, 0, j)),
        pl.BlockSpec((None, tf, d), lambda bi, i, j: (layer, j, 0)),
    ]
    args += [mod, g2, w1, w2]
    if final:
        in_specs.append(pl.BlockSpec((1, d), lambda bi, i, j: (0, 0)))
        args.append(final_g)
    return pl.pallas_call(
        functools.partial(_mlp_kernel, final=final, pending=pending),
        grid=(b, t // tm, f // tf),
        in_specs=in_specs,
        out_specs=pl.BlockSpec((1, tm, d), lambda bi, i, j: (bi, i, 0)),
        out_shape=jax.ShapeDtypeStruct((b, t, d), F32),
        scratch_shapes=[pltpu.VMEM((tm, d), F32), pltpu.VMEM((tm, d), BF), pltpu.VMEM((tm, d), F32)],
        compiler_params=_cparams(("parallel", "parallel", "arbitrary"), 56),
        name="mlp_final" if final else "mlp",
    )(*args)


def _na_proj_kernel(x_ref, mod_ref, g1_ref, w_ref, qt_ref, k_ref, v_ref):
    h = _norm_mod(x_ref[0], g1_ref[...], mod_ref[0, 0:1, :], mod_ref[0, 1:2, :]).astype(BF)
    qkv = jnp.dot(h, w_ref[...], preferred_element_type=F32)
    for c in range(NA_WIDTH // LANES):
        qt_ref[0, c * LANES:(c + 1) * LANES, :] = (qkv[:, c * LANES:(c + 1) * LANES] * (QK_SCALE * LOG2E)).T.astype(BF)
    k_ref[0] = qkv[:, NA_WIDTH:2 * NA_WIDTH].astype(BF)
    v_ref[0] = qkv[:, 2 * NA_WIDTH:].astype(BF)


def _na_proj(x, mod, g1, w, *, tm=256):
    bx, t, d = x.shape
    return pl.pallas_call(
        _na_proj_kernel,
        grid=(bx, t // tm),
        in_specs=[
            pl.BlockSpec((1, tm, d), lambda b, i: (b, i, 0)),
            pl.BlockSpec((1, N_MOD, d), _mod_map(mod.shape[0])),
            pl.BlockSpec((1, d), lambda b, i: (0, 0)),
            pl.BlockSpec(w.shape, lambda b, i: (0, 0)),
        ],
        out_specs=[
            pl.BlockSpec((1, NA_WIDTH, tm), lambda b, i: (b, 0, i)),
            pl.BlockSpec((1, tm, NA_WIDTH), lambda b, i: (b, i, 0)),
            pl.BlockSpec((1, tm, NA_WIDTH), lambda b, i: (b, i, 0)),
        ],
        out_shape=[
            jax.ShapeDtypeStruct((bx, NA_WIDTH, t), BF),
            jax.ShapeDtypeStruct((bx, t, NA_WIDTH), BF),
            jax.ShapeDtypeStruct((bx, t, NA_WIDTH), BF),
        ],
        compiler_params=_cparams(("parallel", "parallel"), 48),
        name="na_proj",
    )(x, mod, g1, w)


def _na_key_base(g, n_groups):
    return jnp.clip(NA_QROWS * g - NA_WIN_ROWS // 2, 0, NA_QROWS * n_groups - NA_KROWS)


def _na_kernel(*refs, n_groups, per_step):
    qt_ref, k_ref, v_ref, kc_ref, vc_ref, quad_ref = refs[:6]
    mask_refs = refs[6:6 + per_step]
    ot_ref = refs[6 + per_step]
    tq = NA_QROWS * GRID_W
    ck = NA_CHUNK_ROWS * GRID_W
    n_nb = NA_KROWS // NA_CHUNK_ROWS
    tn = (((0,), (0,)), ((), ()))
    upper = lax.broadcasted_iota(jnp.int32, (2 * HEAD_DIM, tq), 0) < HEAD_DIM
    geo = []
    for gi in range(per_step):
        g = pl.program_id(2) * per_step + gi
        kb_row = _na_key_base(g, n_groups)
        q2 = qt_ref[0, :, gi * tq:(gi + 1) * tq]
        qh = (jnp.where(upper, q2, jnp.zeros_like(q2)), jnp.where(upper, jnp.zeros_like(q2), q2))
        geo.append((pl.multiple_of(kb_row * GRID_W, GRID_W), kb_row - NA_QROWS * g + NA_WIN_ROWS, qh))

    def scores(gi, half, j):
        kb, e0, qh = geo[gi]
        if j < 0:
            return jnp.dot(kc_ref[0], qh[half], preferred_element_type=F32)
        bias = jnp.concatenate([quad_ref[half, e0 + NA_CHUNK_ROWS * j + i] for i in range(NA_CHUNK_ROWS)], axis=0)
        bias = bias + mask_refs[gi][0, j * ck:(j + 1) * ck, :]
        return jnp.dot(k_ref[0, pl.ds(kb + j * ck, ck), :], qh[half], preferred_element_type=F32) + bias

    def values(gi, j):
        return vc_ref[0] if j < 0 else v_ref[0, pl.ds(geo[gi][0] + j * ck, ck), :]

    def finish(gi, o0, l0, o1, l1):
        ot_ref[0, :, gi * tq:(gi + 1) * tq] = jnp.where(upper, o0 * (1.0 / l0), o1 * (1.0 / l1)).astype(BF)

    heads = [(gi, half) for gi in range(per_step) for half in range(2)]
    units = [(gi, half, j) for j in range(-1, n_nb) for gi, half in heads]
    lookahead = max(NA_LOOKAHEAD, len(heads))
    pending = [scores(*u) for u in units[:lookahead]]
    m0 = {u[:2]: jnp.max(s, axis=0, keepdims=True) for u, s in zip(units[:len(heads)], pending)}
    acc = {}
    den = {}
    for n, (gi, half, j) in enumerate(units):
        if n + lookahead < len(units):
            pending.append(scores(*units[n + lookahead]))
        p = jnp.exp2(pending.pop(0) - m0[gi, half])
        pv = lax.dot_general(values(gi, j), p.astype(BF), tn, preferred_element_type=F32)
        ps = jnp.sum(p, axis=0, keepdims=True)
        acc[gi, half] = pv if j < 0 else acc[gi, half] + pv
        den[gi, half] = ps if j < 0 else den[gi, half] + ps
    bad = jnp.zeros((1, tq), F32)
    for key in heads:
        bad = jnp.where(jnp.isfinite(den[key]), bad, 1.0)
    overflowed = jnp.max(bad) > 0.0

    @pl.when(jnp.logical_not(overflowed))
    def _():
        for gi in range(per_step):
            finish(gi, acc[gi, 0], den[gi, 0], acc[gi, 1], den[gi, 1])

    @pl.when(overflowed)
    def _():
        for gi in range(per_step):
            outs = []
            for half in range(2):
                s_all = [scores(gi, half, j) for j in range(-1, n_nb)]
                m = s_all[0].max(axis=0, keepdims=True)
                for s in s_all[1:]:
                    m = jnp.maximum(m, jnp.max(s, axis=0, keepdims=True))
                o = None
                l = None
                for j, s in zip(range(-1, n_nb), s_all):
                    p = jnp.exp2(s - m)
                    pv = lax.dot_general(values(gi, j), p.astype(BF), tn, preferred_element_type=F32)
                    ps = jnp.sum(p, axis=0, keepdims=True)
                    o = pv if o is None else o + pv
                    l = ps if l is None else l + ps
                outs += [o, l]
            finish(gi, *outs)


def _na_bias_tables(rpb, n_rows):
    n_groups = n_rows // NA_QROWS
    n_h, n_dr, n_dc = rpb.shape
    m = n_dc + 2 * GRID_W
    c0 = GRID_W + NA_WIN_COLS - 1
    flat = jnp.tile(jnp.pad(rpb, ((0, 0), (0, 0), (GRID_W, GRID_W))), (1, 1, GRID_W + 1))
    t = flat[:, :, c0:c0 + GRID_W * (m - 1)].reshape(n_h, n_dr, GRID_W, m - 1)[..., :GRID_W]
    qc = np.arange(GRID_W)[None, :]
    kc = np.arange(GRID_W)[:, None]
    cs = np.clip(qc - NA_WIN_COLS // 2, 0, GRID_W - NA_WIN_COLS)
    col_ok = (kc >= cs) & (kc < cs + NA_WIN_COLS)
    tc = jnp.where(col_ok, jnp.swapaxes(t, -1, -2) * LOG2E, NEG_BIG)
    back = NA_EROWS - 1 - n_dr
    tcp = jnp.pad(tc, ((0, 0), (NA_QROWS, back), (0, 0), (0, 0)))
    quad = jnp.concatenate([tcp[:, NA_QROWS - 1 - a:NA_QROWS - 1 - a + NA_EROWS] for a in range(NA_QROWS)], axis=-1)
    masks = []
    for g in (0, 1, n_groups - 1):
        r = NA_QROWS * g + np.arange(NA_QROWS)[None, None, :, None]
        kb = int(np.clip(NA_QROWS * g - NA_WIN_ROWS // 2, 0, n_rows - NA_KROWS))
        krow = kb + np.arange(NA_KROWS)[:, None, None, None]
        rs = np.clip(r - NA_WIN_ROWS // 2, 0, n_rows - NA_WIN_ROWS)
        ok = np.broadcast_to((krow >= rs) & (krow < rs + NA_WIN_ROWS), (NA_KROWS, GRID_W, NA_QROWS, GRID_W))
        masks.append(np.where(ok, 0.0, NEG_BIG).reshape(NA_KROWS * GRID_W, NA_QROWS * GRID_W))
    return quad.astype(F32), jnp.asarray(np.stack(masks), F32)


def _na_attend(qt, k, v, kc, vc, quad, rowmask, n_rows, *, per_step=NA_GROUPS_PER_STEP):
    b, w, s = qt.shape
    c = kc.shape[1]
    n_groups = n_rows // NA_QROWS
    tq = NA_QROWS * GRID_W
    pair = 2 * HEAD_DIM

    def mask_spec(gi):
        def index_map(bi, hp, st):
            g = st * per_step + gi
            return (jnp.where(g == 0, 0, jnp.where(g == n_groups - 1, 2, 1)), 0, 0)
        return pl.BlockSpec((1, NA_KROWS * GRID_W, tq), index_map)

    return pl.pallas_call(
        functools.partial(_na_kernel, n_groups=n_groups, per_step=per_step),
        grid=(b, w // pair, n_groups // per_step),
        in_specs=[
            pl.BlockSpec((1, pair, per_step * tq), lambda bi, hp, st: (bi, hp, st)),
            pl.BlockSpec((1, s, pair), lambda bi, hp, st: (bi, 0, hp)),
            pl.BlockSpec((1, s, pair), lambda bi, hp, st: (bi, 0, hp)),
            pl.BlockSpec((1, c, pair), lambda bi, hp, st: (bi, 0, hp)),
            pl.BlockSpec((1, c, pair), lambda bi, hp, st: (bi, 0, hp)),
            pl.BlockSpec((2, NA_EROWS, GRID_W, tq), lambda bi, hp, st: (hp, 0, 0, 0)),
        ] + [mask_spec(gi) for gi in range(per_step)],
        out_specs=pl.BlockSpec((1, pair, per_step * tq), lambda bi, hp, st: (bi, hp, st)),
        out_shape=jax.ShapeDtypeStruct((b, w, s), BF),
        compiler_params=_cparams(("parallel", "parallel", "arbitrary"), 48),
        name="na_attend",
    )(qt, k, v, kc, vc, quad, *([rowmask] * per_step))


def _conv_pw1_kernel(x_ref, mod_ref, g1_ref, w_ref, b_ref, u_ref):
    h = _norm_mod(x_ref[0], g1_ref[...], mod_ref[0, 0:1, :], mod_ref[0, 1:2, :]).astype(BF)
    ag = jnp.dot(h, w_ref[...], preferred_element_type=F32) + b_ref[...]
    d = u_ref.shape[2]
    u_ref[0] = ag[:, :d] * jax.nn.sigmoid(ag[:, d:])


def _conv_pw1(x, mod, g1, w, b, *, tm=512):
    bx, t, d = x.shape
    return pl.pallas_call(
        _conv_pw1_kernel,
        grid=(bx, t // tm),
        in_specs=[
            pl.BlockSpec((1, tm, d), lambda bi, i: (bi, i, 0)),
            pl.BlockSpec((1, N_MOD, d), _mod_map(mod.shape[0])),
            pl.BlockSpec((1, d), lambda bi, i: (0, 0)),
            pl.BlockSpec(w.shape, lambda bi, i: (0, 0)),
            pl.BlockSpec((1, 2 * d), lambda bi, i: (0, 0)),
        ],
        out_specs=pl.BlockSpec((1, tm, d), lambda bi, i: (bi, i, 0)),
        out_shape=jax.ShapeDtypeStruct((bx, t, d), F32),
        compiler_params=_cparams(("parallel", "parallel"), 48),
        name="conv_pw1_glu",
    )(x, mod, g1, w, b)


def _conv_tail_kernel(u_ref, up_ref, un_ref, wdw_ref, bdw_ref, lg_ref, lb_ref, w2_ref, b2_ref, x_ref, mod_ref,
                      o_ref, buf_sc, cv_sc, *, tm):
    i = pl.program_id(1)
    last = pl.num_programs(1) - 1
    buf_sc[0:CONV_HALO, :] = jnp.where(i > 0, up_ref[0], 0.0)
    buf_sc[CONV_HALO:CONV_HALO + tm, :] = u_ref[0]
    buf_sc[CONV_HALO + tm:, :] = jnp.where(i < last, un_ref[0], 0.0)
    assert CONV_HALO - CONV_WIDTH // 2 == 1
    d = buf_sc.shape[1]
    for lb in range(d // CONV_LANES):
        ls = slice(lb * CONV_LANES, (lb + 1) * CONV_LANES)
        for rc in range(tm // CONV_ROWS):
            r0 = rc * CONV_ROWS
            acc = None
            for r in range(SUBLANES):
                part = None
                for j in range(r, CONV_WIDTH + 1, SUBLANES):
                    if j == 0:
                        continue
                    rows = slice(r0 + j - r, r0 + j - r + CONV_ROWS + SUBLANES)
                    term = buf_sc[rows, ls] * wdw_ref[j - 1:j, ls]
                    part = term if part is None else part + term
                part = part[r:r + CONV_ROWS]
                acc = part if acc is None else acc + part
            cv_sc[r0:r0 + CONV_ROWS, ls] = acc
    u = cv_sc[...] + bdw_ref[...]
    mu = jnp.mean(u, axis=-1, keepdims=True)
    uc = u - mu
    var = jnp.mean(uc * uc, axis=-1, keepdims=True)
    y = uc * lax.rsqrt(var + EPS) * lg_ref[...] + lb_ref[...]
    y = (y * jax.nn.sigmoid(y)).astype(BF)
    z = jnp.dot(y, w2_ref[...], preferred_element_type=F32) + b2_ref[...]
    o_ref[0] = x_ref[0] + mod_ref[0, 2:3, :] * z


def _conv_tail(u, wdw, bdw, lg, lb, w2, b2, x, mod, *, tm=256):
    bx, t, d = x.shape
    hb = tm // CONV_HALO
    n_halo = t // CONV_HALO
    vec = lambda: pl.BlockSpec((1, d), lambda bi, i: (0, 0))
    return pl.pallas_call(
        functools.partial(_conv_tail_kernel, tm=tm),
        grid=(bx, t // tm),
        in_specs=[
            pl.BlockSpec((1, tm, d), lambda bi, i: (bi, i, 0)),
            pl.BlockSpec((1, CONV_HALO, d), lambda bi, i: (bi, jnp.maximum(i * hb - 1, 0), 0)),
            pl.BlockSpec((1, CONV_HALO, d), lambda bi, i: (bi, jnp.minimum((i + 1) * hb, n_halo - 1), 0)),
            pl.BlockSpec(wdw.shape, lambda bi, i: (0, 0)),
            vec(), vec(), vec(),
            pl.BlockSpec(w2.shape, lambda bi, i: (0, 0)),
            vec(),
            pl.BlockSpec((1, tm, d), lambda bi, i: (bi, i, 0)),
            pl.BlockSpec((1, N_MOD, d), _mod_map(mod.shape[0])),
        ],
        out_specs=pl.BlockSpec((1, tm, d), lambda bi, i: (bi, i, 0)),
        out_shape=jax.ShapeDtypeStruct((bx, t, d), F32),
        scratch_shapes=[pltpu.VMEM((tm + 2 * CONV_HALO, d), F32), pltpu.VMEM((tm, d), F32)],
        compiler_params=_cparams(("parallel", "parallel"), 48),
        name="conv_tail",
    )(u, u, u, wdw, bdw, lg, lb, w2, b2, x, mod)


def _ft_tables(n_seq, gw):
    nb = n_seq // FT_NA
    ka = np.arange(FT_NA)[None, :, None]
    na = np.arange(FT_NA)[None, None, :]
    jb = np.arange(nb)[:, None, None]
    ph = 2 * np.pi * ((ka * (nb * na + jb)) % n_seq) / n_seq
    t1 = np.concatenate([np.cos(ph), -np.sin(ph)], axis=1)
    kb = np.arange(nb)
    ph = 2 * np.pi * ((kb[:, None] * kb[None, :]) % nb) / nb
    c3, s3 = np.cos(ph), np.sin(ph)
    t2 = np.block([[c3, s3], [-s3, c3]]) / np.sqrt(n_seq)
    m = np.arange(gw)
    ph = 2 * np.pi * ((m[:, None] * m[None, :]) % gw) / gw
    tw = np.stack([np.cos(ph), np.sin(ph)]) / np.sqrt(gw)
    return tuple(jnp.asarray(t, F32).astype(BF) for t in (t1, t2, tw))


def _ft_prep_kernel(x_ref, mod_ref, g1_ref, h_ref, *, nb):
    h = _norm_mod(x_ref[0], g1_ref[...], mod_ref[0, 0:1, :], mod_ref[0, 1:2, :])
    for a in range(h.shape[0] // nb):
        h_ref[0, :, a, :] = h[a * nb:(a + 1) * nb, :]


def _ft_prep(x, mod, g1, *, tm=1024):
    bx, t, d = x.shape
    nb = t // FT_NA
    return pl.pallas_call(
        functools.partial(_ft_prep_kernel, nb=nb),
        grid=(bx, t // tm),
        in_specs=[
            pl.BlockSpec((1, tm, d), lambda bi, i: (bi, i, 0)),
            pl.BlockSpec((1, N_MOD, d), _mod_map(mod.shape[0])),
            pl.BlockSpec((1, d), lambda bi, i: (0, 0)),
        ],
        out_specs=pl.BlockSpec((1, nb, tm // nb, d), lambda bi, i: (bi, 0, i, 0)),
        out_shape=jax.ShapeDtypeStruct((bx, nb, FT_NA, d), F32),
        compiler_params=_cparams(("parallel", "parallel"), 48),
        name="ft_prep",
    )(x, mod, g1)


def _ft_seq1_kernel(h_ref, t1_ref, y_ref, *, tb, d):
    for j in range(tb):
        y = jnp.dot(t1_ref[j], h_ref[0, j].astype(BF), preferred_element_type=F32)
        y_ref[0, :, j, :d] = y[:FT_NA]
        y_ref[0, :, j, d:] = y[FT_NA:]


def _ft_seq1(hp, t1, *, tb=8):
    bx, nb, _, d = hp.shape
    return pl.pallas_call(
        functools.partial(_ft_seq1_kernel, tb=tb, d=d),
        grid=(bx, nb // tb),
        in_specs=[
            pl.BlockSpec((1, tb, FT_NA, d), lambda bi, i: (bi, i, 0, 0)),
            pl.BlockSpec((tb, 2 * FT_NA, FT_NA), lambda bi, i: (i, 0, 0)),
        ],
        out_specs=pl.BlockSpec((1, FT_NA, tb, 2 * d), lambda bi, i: (bi, 0, i, 0)),
        out_shape=jax.ShapeDtypeStruct((bx, FT_NA, nb, 2 * d), F32),
        compiler_params=_cparams(("parallel", "parallel"), 48),
        name="ft_seq1",
    )(hp, t1)


def _ft_seq2_kernel(y_ref, t2_ref, tw_ref, w_ref, b_ref, mod_ref, o_ref, *, ta):
    d = w_ref.shape[0]
    nb = y_ref.shape[2]
    gw = tw_ref.shape[1]
    for a in range(ta):
        yb = y_ref[0, a]
        rhs = jnp.concatenate([yb[:, :d], yb[:, d:]], axis=0).astype(BF)
        aa = jnp.dot(t2_ref[...], rhs, preferred_element_type=F32).astype(BF)
        z = [jnp.dot(aa[:nb, g * gw:(g + 1) * gw], tw_ref[0], preferred_element_type=F32)
             + jnp.dot(aa[nb:, g * gw:(g + 1) * gw], tw_ref[1], preferred_element_type=F32)
             for g in range(d // gw)]
        z = jnp.concatenate(z, axis=1).astype(BF)
        yl = jnp.dot(z, w_ref[...], preferred_element_type=F32) + b_ref[...]
        o_ref[0, :, a, :] = mod_ref[0, 2:3, :] * yl


def _ft_seq2(y, t2, tw, w, b, mod, *, ta=8):
    bx, _, nb, d2 = y.shape
    d = d2 // 2
    out = pl.pallas_call(
        functools.partial(_ft_seq2_kernel, ta=ta),
        grid=(bx, FT_NA // ta),
        in_specs=[
            pl.BlockSpec((1, ta, nb, d2), lambda bi, i: (bi, i, 0, 0)),
            pl.BlockSpec(t2.shape, lambda bi, i: (0, 0)),
            pl.BlockSpec(tw.shape, lambda bi, i: (0, 0, 0)),
            pl.BlockSpec(w.shape, lambda bi, i: (0, 0)),
            pl.BlockSpec((1, d), lambda bi, i: (0, 0)),
            pl.BlockSpec((1, N_MOD, d), _mod_map(mod.shape[0])),
        ],
        out_specs=pl.BlockSpec((1, nb, ta, d), lambda bi, i: (bi, 0, i, 0)),
        out_shape=jax.ShapeDtypeStruct((bx, nb, FT_NA, d), F32),
        compiler_params=_cparams(("parallel", "parallel"), 48),
        name="ft_seq2_mix",
    )(y, t2, tw, w, b, mod)
    return out.reshape(bx, nb * FT_NA, d)


def _rope_tables(n_tok):
    t = jnp.arange(n_tok)
    row = (t // GRID_W).astype(F32)
    col = (t % GRID_W).astype(F32)
    n_axis = HEAD_DIM // 4
    inv = ROPE_THETA ** (-jnp.arange(n_axis, dtype=F32) / n_axis)
    ang = jnp.concatenate([row[:, None] * inv, col[:, None] * inv], axis=-1)
    ang = jnp.tile(jnp.repeat(ang, 2, axis=-1), (1, LANES // HEAD_DIM))
    sign = jnp.where(jnp.arange(LANES) % 2 == 0, -1.0, 1.0).astype(F32)
    return jnp.cos(ang), jnp.sin(ang) * sign


def _row(v):
    return v.reshape(1, -1)


def kernel(x, c, ctx, c_ctx, ada_w, ada_b, norm1_g, norm2_g, mlp_w1, mlp_w2, final_g, at_w_qkv, at_q_g, at_k_g, at_w_o, na_w_qkv, na_rpb, na_w_o, cv_w_pw1, cv_b_pw1, cv_w_dw, cv_b_dw, cv_ln_g, cv_ln_b, cv_w_pw2, cv_b_pw2, ft_w, ft_b):
    bsz, n_lat, d = x.shape
    depth = ada_w.shape[0]
    n_rows = n_lat // GRID_W
    n_ctx = ctx.shape[1]

    vec8 = jnp.zeros((8, d), F32).at[:bsz].set(c).at[bsz].set(c_ctx)
    mods = _ada_mods(vec8, ada_w, ada_b)

    w1m = mlp_w1.astype(BF)
    w2m = mlp_w2.astype(BF)
    hm = jnp.asarray(np.kron(np.eye(AT_HEADS), np.full((HEAD_DIM, HEAD_DIM), 1.0 / HEAD_DIM)), BF)
    h_ctx = ctx
    for i in range(depth):
        kind = i % N_MIXERS
        occ = i // N_MIXERS
        ctx_later = any((j % N_MIXERS) in (0, 1) for j in range(i + 1, depth))
        ml = mods[i, :bsz].reshape(bsz, N_MOD, d)
        mc = mods[i, bsz:bsz + 1].reshape(1, N_MOD, d)
        g1 = _row(norm1_g[i])
        delta = attn = attn_ctx = None
        if kind == 0:
            w = at_w_qkv[occ].astype(BF)
            wo = at_w_o[occ].astype(BF)
            qg = _row(jnp.tile(at_q_g[occ], AT_HEADS))
            kg = _row(jnp.tile(at_k_g[occ], AT_KV_HEADS))
            cos, sin = _rope_tables(n_lat)
            qt_l, k_l, vt_l = _gqa_proj(x, ml, g1, w, hm, qg, kg, cos, sin)
            qt_c, k_c, vt_c = _gqa_proj(h_ctx, mc, g1, w, hm, qg, kg, None, None)
            attn = (_flash(qt_l, k_c, vt_c, k_l, vt_l, tq=512), wo)
            if ctx_later:
                attn_ctx = (_flash(qt_c, k_c, vt_c, tq=n_ctx), wo)
        elif kind == 1:
            w = na_w_qkv[occ].astype(BF)
            wo = na_w_o[occ].astype(BF)
            qt_l, k_l, v_l = _na_proj(x, ml, g1, w)
            qt_c, k_c, v_c = _na_proj(h_ctx, mc, g1, w)
            quad, rowmask = _na_bias_tables(na_rpb[occ], n_rows)
            attn = (_na_attend(qt_l, k_l, v_l, k_c, v_c, quad, rowmask, n_rows), wo)
            if ctx_later:
                raise NotImplementedError("context output of a neighbourhood layer is not needed at this depth")
        elif kind == 2:
            w1 = cv_w_pw1[occ].astype(BF)
            w2 = cv_w_pw2[occ].astype(BF)
            wdw = jnp.zeros((CONV_WIDTH + 1, d), F32).at[:CONV_WIDTH].set(cv_w_dw[occ])
            cv = (wdw, _row(cv_b_dw[occ]), _row(cv_ln_g[occ]), _row(cv_ln_b[occ]), w2, _row(cv_b_pw2[occ]))
            u = _conv_pw1(x, ml, g1, w1, _row(cv_b_pw1[occ]))
            x = _conv_tail(u, *cv, x, ml)
            if ctx_later:
                raise NotImplementedError("context output of a convolution layer is not needed at this depth")
        else:
            t1, t2, tw = _ft_tables(n_lat, d // FT_GROUPS)
            y1 = _ft_seq1(_ft_prep(x, ml, g1), t1)
            delta = _ft_seq2(y1, t2, tw, ft_w[occ].astype(BF), _row(ft_b[occ]), ml)
            if ctx_later:
                raise NotImplementedError("context output of a Fourier layer is not needed at this depth")
        g2 = _row(norm2_g[i])
        is_last = i == depth - 1
        x = _mlp(x, ml, g2, w1m, w2m, i, _row(final_g) if is_last else None, delta=delta, attn=attn, tm=1024, tf=1024)
        if ctx_later:
            h_ctx = _mlp(h_ctx, mc, g2, w1m, w2m, i, attn=attn_ctx, tm=n_ctx)
    return x
```

```python
import functools

import jax
import jax.numpy as jnp
import numpy as np
from jax import lax
from jax.experimental import pallas as pl
from jax.experimental.pallas import tpu as pltpu

GRID_W = 64
N_MIXERS = 4
HEAD_DIM = 64
AT_HEADS = 16
AT_KV_HEADS = 4
AT_WIDTH = AT_HEADS * HEAD_DIM
AT_KV_WIDTH = AT_KV_HEADS * HEAD_DIM
ROPE_THETA = 10000.0
NA_HEADS = 16
NA_WIDTH = NA_HEADS * HEAD_DIM
NA_WIN_ROWS = 8
NA_WIN_COLS = 16
CONV_WIDTH = 31
FT_GROUPS = 4
N_MOD = 6
EPS = 1e-6
QK_SCALE = HEAD_DIM ** -0.5
LOG2E = float(np.log2(np.e))

LANES = 128
NA_QROWS = 4
NA_KROWS = 12
NA_CHUNK_ROWS = 4
NA_GROUPS_PER_STEP = 4
NA_LOOKAHEAD = 6
NA_EROWS = NA_KROWS + 2 * NA_QROWS
CONV_HALO = 16
CONV_ROWS = 64
CONV_LANES = 256
SUBLANES = 8
FT_NA = 64
NEG_BIG = -1e30

BF = jnp.bfloat16
F32 = jnp.float32


def _cparams(sem, vmem_mib):
    return pltpu.CompilerParams(dimension_semantics=sem, vmem_limit_bytes=vmem_mib << 20)


def _norm_mod(x, g, shift, scale):
    ms = jnp.mean(x * x, axis=-1, keepdims=True)
    return (x * lax.rsqrt(ms + EPS) * g) * (1.0 + scale) + shift


def _ada_kernel(v_ref, w_ref, b_ref, o_ref):
    v = v_ref[...]
    sv = v * jax.nn.sigmoid(v)
    o_ref[0] = jnp.dot(sv, w_ref[0], preferred_element_type=F32, precision=lax.Precision.HIGHEST) + b_ref[0]


def _ada_mods(vec8, ada_w, ada_b):
    depth, d, n = ada_w.shape
    tn = n // 4
    return pl.pallas_call(
        _ada_kernel,
        grid=(depth, n // tn),
        in_specs=[
            pl.BlockSpec((8, d), lambda l, j: (0, 0)),
            pl.BlockSpec((1, d, tn), lambda l, j: (l, 0, j)),
            pl.BlockSpec((1, 1, tn), lambda l, j: (l, 0, j)),
        ],
        out_specs=pl.BlockSpec((1, 8, tn), lambda l, j: (l, 0, j)),
        out_shape=jax.ShapeDtypeStruct((depth, 8, n), F32),
        compiler_params=_cparams(("arbitrary", "arbitrary"), 40),
        name="ada_mods",
    )(vec8, ada_w, ada_b.reshape(depth, 1, n))


def _mod_map(n_mod_rows):
    if n_mod_rows == 1:
        return lambda b, *_: (0, 0, 0)
    return lambda b, *_: (b, 0, 0)


def _gqa_proj_kernel(*refs, rope, tm):
    if rope:
        x_ref, mod_ref, g1_ref, w_ref, hm_ref, qg_ref, kg_ref, cos_ref, sin_ref, qt_ref, k_ref, vt_ref = refs
    else:
        x_ref, mod_ref, g1_ref, w_ref, hm_ref, qg_ref, kg_ref, qt_ref, k_ref, vt_ref = refs
    h = _norm_mod(x_ref[0], g1_ref[...], mod_ref[0, 0:1, :], mod_ref[0, 1:2, :]).astype(BF)
    qkv = jnp.dot(h, w_ref[...], preferred_element_type=F32)
    q = qkv[:, :AT_WIDTH]
    k = qkv[:, AT_WIDTH:AT_WIDTH + AT_KV_WIDTH]
    v = qkv[:, AT_WIDTH + AT_KV_WIDTH:]
    hm = hm_ref[...]
    q = q * lax.rsqrt(jnp.dot((q * q).astype(BF), hm, preferred_element_type=F32) + EPS) * qg_ref[...]
    hk = hm[:AT_KV_WIDTH, :AT_KV_WIDTH]
    k = k * lax.rsqrt(jnp.dot((k * k).astype(BF), hk, preferred_element_type=F32) + EPS) * kg_ref[...]
    if rope:
        cos = cos_ref[...]
        sin = sin_ref[...]
        even = (lax.broadcasted_iota(jnp.int32, (tm, LANES), 1) % 2) == 0

        def rot(t):
            partner = jnp.where(even, pltpu.roll(t, LANES - 1, 1), pltpu.roll(t, 1, 1))
            return t * cos + partner * sin
    else:
        def rot(t):
            return t
    for c in range(AT_WIDTH // LANES):
        qc = rot(q[:, c * LANES:(c + 1) * LANES]) * (QK_SCALE * LOG2E)
        qt_ref[0, c * LANES:(c + 1) * LANES, :] = qc.T.astype(BF)
    for c in range(AT_KV_WIDTH // LANES):
        kc = rot(k[:, c * LANES:(c + 1) * LANES]).astype(BF)
        k_ref[0, 2 * c] = kc[:, :HEAD_DIM]
        k_ref[0, 2 * c + 1] = kc[:, HEAD_DIM:]
    vt = v.T.astype(BF)
    for h in range(AT_KV_HEADS):
        vt_ref[0, h] = vt[h * HEAD_DIM:(h + 1) * HEAD_DIM]


def _gqa_proj(x, mod, g1, w, hm, qg, kg, cos, sin, *, tm=256):
    bx, t, d = x.shape
    rope = cos is not None
    n = w.shape[1]
    const = lambda b, i: (0, 0)
    in_specs = [
        pl.BlockSpec((1, tm, d), lambda b, i: (b, i, 0)),
        pl.BlockSpec((1, N_MOD, d), _mod_map(mod.shape[0])),
        pl.BlockSpec((1, d), const),
        pl.BlockSpec((d, n), const),
        pl.BlockSpec(hm.shape, const),
        pl.BlockSpec((1, AT_WIDTH), const),
        pl.BlockSpec((1, AT_KV_WIDTH), const),
    ]
    args = [x, mod, g1, w, hm, qg, kg]
    if rope:
        in_specs += [pl.BlockSpec((tm, LANES), lambda b, i: (i, 0))] * 2
        args += [cos, sin]
    return pl.pallas_call(
        functools.partial(_gqa_proj_kernel, rope=rope, tm=tm),
        grid=(bx, t // tm),
        in_specs=in_specs,
        out_specs=[
            pl.BlockSpec((1, AT_WIDTH, tm), lambda b, i: (b, 0, i)),
            pl.BlockSpec((1, AT_KV_HEADS, tm, HEAD_DIM), lambda b, i: (b, 0, i, 0)),
            pl.BlockSpec((1, AT_KV_HEADS, HEAD_DIM, tm), lambda b, i: (b, 0, 0, i)),
        ],
        out_shape=[
            jax.ShapeDtypeStruct((bx, AT_WIDTH, t), BF),
            jax.ShapeDtypeStruct((bx, AT_KV_HEADS, t, HEAD_DIM), BF),
            jax.ShapeDtypeStruct((bx, AT_KV_HEADS, HEAD_DIM, t), BF),
        ],
        compiler_params=_cparams(("parallel", "parallel"), 48),
        name="gqa_proj_rope" if rope else "gqa_proj_ctx",
    )(*args)


def _flash_kernel(*refs, tk, lookahead, with_latents):
    if with_latents:
        qt_ref, kc_ref, vtc_ref, kl_ref, vtl_ref, ot_ref, m_sc, l_sc, acc_sc = refs
        n_lat = kl_ref.shape[2] // tk
    else:
        qt_ref, kc_ref, vtc_ref, ot_ref, m_sc, l_sc, acc_sc = refs
        n_lat = 0
    n_heads = qt_ref.shape[1] // HEAD_DIM
    qs = [qt_ref[0, hh * HEAD_DIM:(hh + 1) * HEAD_DIM, :] for hh in range(n_heads)]
    chunks = [(kc_ref.at[0, 0], vtc_ref.at[0, 0])]
    chunks += [(kl_ref.at[0, 0, c * tk:(c + 1) * tk, :], vtl_ref.at[0, 0, :, c * tk:(c + 1) * tk]) for c in range(n_lat)]

    def scores(hh, c):
        return jnp.dot(chunks[c][0][...], qs[hh], preferred_element_type=F32)

    def weighted(c, s, m):
        p = jnp.exp2(s - m)
        psum = p.reshape(p.shape[0] // SUBLANES, SUBLANES, p.shape[1]).sum(axis=0)
        return jnp.dot(chunks[c][1][...], p.astype(BF), preferred_element_type=F32), psum

    units = [(hh, c) for hh in range(n_heads) for c in range(len(chunks))]
    pending = [scores(*u) for u in units[:lookahead]]
    m0, acc, den8 = {}, {}, {}
    for n, (hh, c) in enumerate(units):
        if n + lookahead < len(units):
            pending.append(scores(*units[n + lookahead]))
        s = pending.pop(0)
        if c == 0:
            m0[hh] = jnp.max(s, axis=0, keepdims=True)
        pv, psum = weighted(c, s, m0[hh])
        acc[hh] = pv if c == 0 else acc[hh] + pv
        den8[hh] = psum if c == 0 else den8[hh] + psum
    denom = [jnp.sum(den8[hh], axis=0, keepdims=True) for hh in range(n_heads)]
    bad = jnp.zeros_like(denom[0])
    for dn in denom:
        bad = jnp.where(jnp.isfinite(dn), bad, 1.0)
    overflowed = jnp.max(bad) > 0.0

    @pl.when(jnp.logical_not(overflowed))
    def _():
        for hh in range(n_heads):
            ot_ref[0, hh * HEAD_DIM:(hh + 1) * HEAD_DIM, :] = (acc[hh] * (1.0 / denom[hh])).astype(BF)

    @pl.when(overflowed)
    def _():
        for hh in range(n_heads):
            m_sc[...] = jnp.full(m_sc.shape, NEG_BIG, F32)
            l_sc[...] = jnp.zeros(l_sc.shape, F32)
            acc_sc[...] = jnp.zeros(acc_sc.shape, F32)

            def update(k, vt, q=qs[hh]):
                s = jnp.dot(k, q, preferred_element_type=F32)
                m_prev = m_sc[...]
                m_new = jnp.maximum(m_prev, jnp.max(s, axis=0, keepdims=True))
                alpha = jnp.exp2(m_prev - m_new)
                p = jnp.exp2(s - m_new)
                l_sc[...] = alpha * l_sc[...] + jnp.sum(p, axis=0, keepdims=True)
                acc_sc[...] = alpha * acc_sc[...] + jnp.dot(vt, p.astype(BF), preferred_element_type=F32)
                m_sc[...] = m_new

            update(kc_ref[0, 0], vtc_ref[0, 0])

            def body(c, carry, update=update):
                off = pl.multiple_of(c * tk, tk)
                update(kl_ref[0, 0, pl.ds(off, tk), :], vtl_ref[0, 0, :, pl.ds(off, tk)])
                return carry

            if with_latents:
                lax.fori_loop(0, n_lat, body, 0)
            ot_ref[0, hh * HEAD_DIM:(hh + 1) * HEAD_DIM, :] = (acc_sc[...] * (1.0 / l_sc[...])).astype(BF)


def _flash(qt, kc, vtc, kl=None, vtl=None, *, tq, tk=256, lookahead=3, heads_per_step=2):
    b, w, t = qt.shape
    kv = kc.shape[1]
    heads = w // HEAD_DIM
    group = heads // kv
    hps = heads_per_step
    assert group % hps == 0
    kv_map = lambda bi, h, i: (bi, h * hps // group, 0, 0)
    in_specs = [
        pl.BlockSpec((1, hps * HEAD_DIM, tq), lambda bi, h, i: (bi, h, i)),
        pl.BlockSpec((1, 1) + kc.shape[2:], kv_map),
        pl.BlockSpec((1, 1) + vtc.shape[2:], kv_map),
    ]
    args = [qt, kc, vtc]
    if kl is not None:
        in_specs += [pl.BlockSpec((1, 1) + kl.shape[2:], kv_map), pl.BlockSpec((1, 1) + vtl.shape[2:], kv_map)]
        args += [kl, vtl]
    return pl.pallas_call(
        functools.partial(_flash_kernel, tk=tk, lookahead=lookahead, with_latents=kl is not None),
        grid=(b, heads // hps, t // tq),
        in_specs=in_specs,
        out_specs=pl.BlockSpec((1, hps * HEAD_DIM, tq), lambda bi, h, i: (bi, h, i)),
        out_shape=jax.ShapeDtypeStruct((b, w, t), BF),
        scratch_shapes=[pltpu.VMEM((1, tq), F32), pltpu.VMEM((1, tq), F32), pltpu.VMEM((HEAD_DIM, tq), F32)],
        compiler_params=_cparams(("parallel", "parallel", "parallel"), 48),
        name="gqa_flash" if kl is not None else "gqa_flash_ctx",
    )(*args)


def _mlp_kernel(*refs, final, pending):
    refs = list(refs)
    x_ref = refs.pop(0)
    d_ref = refs.pop(0) if pending == "delta" else None
    ot_ref, wo_ref = (refs.pop(0), refs.pop(0)) if pending == "attn" else (None, None)
    mod_ref, g2_ref, w1_ref, w2_ref = refs[:4]
    fg_ref = refs[4] if final else None
    o_ref, xs_sc, h_sc, acc_sc = refs[-4:]
    j = pl.program_id(2)

    @pl.when(j == 0)
    def _():
        xs = x_ref[0]
        if pending == "delta":
            xs = xs + d_ref[0]
        elif pending == "attn":
            y = lax.dot_general(ot_ref[0], wo_ref[...], (((0,), (0,)), ((), ())), preferred_element_type=F32)
            xs = xs + mod_ref[0, 2:3, :] * y
        xs_sc[...] = xs
        h_sc[...] = _norm_mod(xs, g2_ref[...], mod_ref[0, 3:4, :], mod_ref[0, 4:5, :]).astype(BF)
        acc_sc[...] = jnp.zeros(acc_sc.shape, F32)

    a = jnp.dot(h_sc[...], w1_ref[...], preferred_element_type=F32)
    a = jnp.square(jnp.maximum(a, 0.0)).astype(BF)
    acc_sc[...] += jnp.dot(a, w2_ref[...], preferred_element_type=F32)

    @pl.when(j == pl.num_programs(2) - 1)
    def _():
        y = xs_sc[...] + mod_ref[0, 5:6, :] * acc_sc[...]
        if final:
            ms = jnp.mean(y * y, axis=-1, keepdims=True)
            y = y * lax.rsqrt(ms + EPS) * fg_ref[...]
        o_ref[0] = y


def _mlp(x, mod, g2, w1, w2, layer, final_g=None, *, delta=None, attn=None, tm, tf=512):
    b, t, d = x.shape
    f = w1.shape[2]
    final = final_g is not None
    row_spec = pl.BlockSpec((1, tm, d), lambda bi, i, j: (bi, i, 0))
    in_specs, args, pending = [row_spec], [x], None
    if delta is not None:
        in_specs, args, pending = in_specs + [row_spec], args + [delta], "delta"
    if attn is not None:
        ot, wo = attn
        in_specs += [pl.BlockSpec((1, ot.shape[1], tm), lambda bi, i, j: (bi, 0, i)),
                     pl.BlockSpec(wo.shape, lambda bi, i, j: (0, 0))]
        args, pending = args + [ot, wo], "attn"
    in_specs += [
        pl.BlockSpec((1, N_MOD, d), _mod_map(mod.shape[0])),
        pl.BlockSpec((1, d), lambda bi, i, j: (0, 0)),
        pl.BlockSpec((None, d, tf), lambda bi, i, j: (layer, 0, j)),
        pl.BlockSpec((None, tf, d), lambda bi, i, j: (layer, j, 0)),
    ]
    args += [mod, g2, w1, w2]
    if final:
        in_specs.append(pl.BlockSpec((1, d), lambda bi, i, j: (0, 0)))
        args.append(final_g)
    return pl.pallas_call(
        functools.partial(_mlp_kernel, final=final, pending=pending),
        grid=(b, t // tm, f // tf),
        in_specs=in_specs,
        out_specs=pl.BlockSpec((1, tm, d), lambda bi, i, j: (bi, i, 0)),
        out_shape=jax.ShapeDtypeStruct((b, t, d), F32),
        scratch_shapes=[pltpu.VMEM((tm, d), F32), pltpu.VMEM((tm, d), BF), pltpu.VMEM((tm, d), F32)],
        compiler_params=_cparams(("parallel", "parallel", "arbitrary"), 56),
        name="mlp_final" if final else "mlp",
    )(*args)


def _na_proj_kernel(x_ref, mod_ref, g1_ref, w_ref, qt_ref, k_ref, v_ref):
    h = _norm_mod(x_ref[0], g1_ref[...], mod_ref[0, 0:1, :], mod_ref[0, 1:2, :]).astype(BF)
    qkv = jnp.dot(h, w_ref[...], preferred_element_type=F32)
    for c in range(NA_WIDTH // LANES):
        qt_ref[0, c * LANES:(c + 1) * LANES, :] = (qkv[:, c * LANES:(c + 1) * LANES] * (QK_SCALE * LOG2E)).T.astype(BF)
    k_ref[0] = qkv[:, NA_WIDTH:2 * NA_WIDTH].astype(BF)
    v_ref[0] = qkv[:, 2 * NA_WIDTH:].astype(BF)


def _na_proj(x, mod, g1, w, *, tm=256):
    bx, t, d = x.shape
    return pl.pallas_call(
        _na_proj_kernel,
        grid=(bx, t // tm),
        in_specs=[
            pl.BlockSpec((1, tm, d), lambda b, i: (b, i, 0)),
            pl.BlockSpec((1, N_MOD, d), _mod_map(mod.shape[0])),
            pl.BlockSpec((1, d), lambda b, i: (0, 0)),
            pl.BlockSpec(w.shape, lambda b, i: (0, 0)),
        ],
        out_specs=[
            pl.BlockSpec((1, NA_WIDTH, tm), lambda b, i: (b, 0, i)),
            pl.BlockSpec((1, tm, NA_WIDTH), lambda b, i: (b, i, 0)),
            pl.BlockSpec((1, tm, NA_WIDTH), lambda b, i: (b, i, 0)),
        ],
        out_shape=[
            jax.ShapeDtypeStruct((bx, NA_WIDTH, t), BF),
            jax.ShapeDtypeStruct((bx, t, NA_WIDTH), BF),
            jax.ShapeDtypeStruct((bx, t, NA_WIDTH), BF),
        ],
        compiler_params=_cparams(("parallel", "parallel"), 48),
        name="na_proj",
    )(x, mod, g1, w)


def _na_key_base(g, n_groups):
    return jnp.clip(NA_QROWS * g - NA_WIN_ROWS // 2, 0, NA_QROWS * n_groups - NA_KROWS)


def _na_kernel(*refs, n_groups, per_step):
    qt_ref, k_ref, v_ref, kc_ref, vc_ref, quad_ref = refs[:6]
    mask_refs = refs[6:6 + per_step]
    ot_ref = refs[6 + per_step]
    tq = NA_QROWS * GRID_W
    ck = NA_CHUNK_ROWS * GRID_W
    n_nb = NA_KROWS // NA_CHUNK_ROWS
    tn = (((0,), (0,)), ((), ()))
    upper = lax.broadcasted_iota(jnp.int32, (2 * HEAD_DIM, tq), 0) < HEAD_DIM
    geo = []
    for gi in range(per_step):
        g = pl.program_id(2) * per_step + gi
        kb_row = _na_key_base(g, n_groups)
        q2 = qt_ref[0, :, gi * tq:(gi + 1) * tq]
        qh = (jnp.where(upper, q2, jnp.zeros_like(q2)), jnp.where(upper, jnp.zeros_like(q2), q2))
        geo.append((pl.multiple_of(kb_row * GRID_W, GRID_W), kb_row - NA_QROWS * g + NA_WIN_ROWS, qh))

    def scores(gi, half, j):
        kb, e0, qh = geo[gi]
        if j < 0:
            return jnp.dot(kc_ref[0], qh[half], preferred_element_type=F32)
        bias = jnp.concatenate([quad_ref[half, e0 + NA_CHUNK_ROWS * j + i] for i in range(NA_CHUNK_ROWS)], axis=0)
        bias = bias + mask_refs[gi][0, j * ck:(j + 1) * ck, :]
        return jnp.dot(k_ref[0, pl.ds(kb + j * ck, ck), :], qh[half], preferred_element_type=F32) + bias

    def values(gi, j):
        return vc_ref[0] if j < 0 else v_ref[0, pl.ds(geo[gi][0] + j * ck, ck), :]

    def finish(gi, o0, l0, o1, l1):
        ot_ref[0, :, gi * tq:(gi + 1) * tq] = jnp.where(upper, o0 * (1.0 / l0), o1 * (1.0 / l1)).astype(BF)

    heads = [(gi, half) for gi in range(per_step) for half in range(2)]
    units = [(gi, half, j) for j in range(-1, n_nb) for gi, half in heads]
    lookahead = max(NA_LOOKAHEAD, len(heads))
    pending = [scores(*u) for u in units[:lookahead]]
    m0 = {u[:2]: jnp.max(s, axis=0, keepdims=True) for u, s in zip(units[:len(heads)], pending)}
    acc = {}
    den = {}
    for n, (gi, half, j) in enumerate(units):
        if n + lookahead < len(units):
            pending.append(scores(*units[n + lookahead]))
        p = jnp.exp2(pending.pop(0) - m0[gi, half])
        pv = lax.dot_general(values(gi, j), p.astype(BF), tn, preferred_element_type=F32)
        ps = jnp.sum(p, axis=0, keepdims=True)
        acc[gi, half] = pv if j < 0 else acc[gi, half] + pv
        den[gi, half] = ps if j < 0 else den[gi, half] + ps
    bad = jnp.zeros((1, tq), F32)
    for key in heads:
        bad = jnp.where(jnp.isfinite(den[key]), bad, 1.0)
    overflowed = jnp.max(bad) > 0.0

    @pl.when(jnp.logical_not(overflowed))
    def _():
        for gi in range(per_step):
            finish(gi, acc[gi, 0], den[gi, 0], acc[gi, 1], den[gi, 1])

    @pl.when(overflowed)
    def _():
        for gi in range(per_step):
            outs = []
            for half in range(2):
                s_all = [scores(gi, half, j) for j in range(-1, n_nb)]
                m = s_all[0].max(axis=0, keepdims=True)
                for s in s_all[1:]:
                    m = jnp.maximum(m, jnp.max(s, axis=0, keepdims=True))
                o = None
                l = None
                for j, s in zip(range(-1, n_nb), s_all):
                    p = jnp.exp2(s - m)
                    pv = lax.dot_general(values(gi, j), p.astype(BF), tn, preferred_element_type=F32)
                    ps = jnp.sum(p, axis=0, keepdims=True)
                    o = pv if o is None else o + pv
                    l = ps if l is None else l + ps
                outs += [o, l]
            finish(gi, *outs)


def _na_quad_kernel(rpb_ref, quad_ref):
    h = pl.program_id(0)
    n_dr, n_dc = rpb_ref.shape[1:]
    kc = lax.broadcasted_iota(jnp.int32, (GRID_W, GRID_W), 0)
    qc = lax.broadcasted_iota(jnp.int32, (GRID_W, GRID_W), 1)
    dc = kc - qc + NA_WIN_COLS - 1
    cs = jnp.clip(qc - NA_WIN_COLS // 2, 0, GRID_W - NA_WIN_COLS)
    col_ok = (kc >= cs) & (kc < cs + NA_WIN_COLS)
    toeplitz = []
    for dr in range(n_dr):
        t = jnp.zeros((GRID_W, GRID_W), F32)
        for c in range(n_dc):
            t = jnp.where(dc == c, rpb_ref[h, dr, c], t)
        toeplitz.append(jnp.where(col_ok, t * LOG2E, NEG_BIG))
    for e in range(NA_EROWS):
        for a in range(NA_QROWS):
            dr = e - a - 1
            blk = toeplitz[dr] if 0 <= dr < n_dr else jnp.zeros((GRID_W, GRID_W), F32)
            quad_ref[0, e, :, a * GRID_W:(a + 1) * GRID_W] = blk


def _na_bias_tables(rpb, n_rows):
    n_groups = n_rows // NA_QROWS
    n_h = rpb.shape[0]
    quad = pl.pallas_call(
        _na_quad_kernel,
        grid=(n_h,),
        in_specs=[pl.BlockSpec(memory_space=pltpu.SMEM)],
        out_specs=pl.BlockSpec((1, NA_EROWS, GRID_W, NA_QROWS * GRID_W), lambda h: (h, 0, 0, 0)),
        out_shape=jax.ShapeDtypeStruct((n_h, NA_EROWS, GRID_W, NA_QROWS * GRID_W), F32),
        compiler_params=_cparams(("parallel",), 32),
        name="na_bias_quad",
    )(rpb)
    masks = []
    for g in (0, 1, n_groups - 1):
        r = NA_QROWS * g + np.arange(NA_QROWS)[None, None, :, None]
        kb = int(np.clip(NA_QROWS * g - NA_WIN_ROWS // 2, 0, n_rows - NA_KROWS))
        krow = kb + np.arange(NA_KROWS)[:, None, None, None]
        rs = np.clip(r - NA_WIN_ROWS // 2, 0, n_rows - NA_WIN_ROWS)
        ok = np.broadcast_to((krow >= rs) & (krow < rs + NA_WIN_ROWS), (NA_KROWS, GRID_W, NA_QROWS, GRID_W))
        masks.append(np.where(ok, 0.0, NEG_BIG).reshape(NA_KROWS * GRID_W, NA_QROWS * GRID_W))
    return quad, jnp.asarray(np.stack(masks), F32)


def _na_attend(qt, k, v, kc, vc, quad, rowmask, n_rows, *, per_step=NA_GROUPS_PER_STEP):
    b, w, s = qt.shape
    c = kc.shape[1]
    n_groups = n_rows // NA_QROWS
    tq = NA_QROWS * GRID_W
    pair = 2 * HEAD_DIM

    def mask_spec(gi):
        def index_map(bi, hp, st):
            g = st * per_step + gi
            return (jnp.where(g == 0, 0, jnp.where(g == n_groups - 1, 2, 1)), 0, 0)
        return pl.BlockSpec((1, NA_KROWS * GRID_W, tq), index_map)

    return pl.pallas_call(
        functools.partial(_na_kernel, n_groups=n_groups, per_step=per_step),
        grid=(b, w // pair, n_groups // per_step),
        in_specs=[
            pl.BlockSpec((1, pair, per_step * tq), lambda bi, hp, st: (bi, hp, st)),
            pl.BlockSpec((1, s, pair), lambda bi, hp, st: (bi, 0, hp)),
            pl.BlockSpec((1, s, pair), lambda bi, hp, st: (bi, 0, hp)),
            pl.BlockSpec((1, c, pair), lambda bi, hp, st: (bi, 0, hp)),
            pl.BlockSpec((1, c, pair), lambda bi, hp, st: (bi, 0, hp)),
            pl.BlockSpec((2, NA_EROWS, GRID_W, tq), lambda bi, hp, st: (hp, 0, 0, 0)),
        ] + [mask_spec(gi) for gi in range(per_step)],
        out_specs=pl.BlockSpec((1, pair, per_step * tq), lambda bi, hp, st: (bi, hp, st)),
        out_shape=jax.ShapeDtypeStruct((b, w, s), BF),
        compiler_params=_cparams(("parallel", "parallel", "arbitrary"), 48),
        name="na_attend",
    )(qt, k, v, kc, vc, quad, *([rowmask] * per_step))


def _conv_pw1_kernel(x_ref, mod_ref, g1_ref, w_ref, b_ref, u_ref):
    h = _norm_mod(x_ref[0], g1_ref[...], mod_ref[0, 0:1, :], mod_ref[0, 1:2, :]).astype(BF)
    ag = jnp.dot(h, w_ref[...], preferred_element_type=F32) + b_ref[...]
    d = u_ref.shape[2]
    u_ref[0] = ag[:, :d] * jax.nn.sigmoid(ag[:, d:])


def _conv_pw1(x, mod, g1, w, b, *, tm=512):
    bx, t, d = x.shape
    return pl.pallas_call(
        _conv_pw1_kernel,
        grid=(bx, t // tm),
        in_specs=[
            pl.BlockSpec((1, tm, d), lambda bi, i: (bi, i, 0)),
            pl.BlockSpec((1, N_MOD, d), _mod_map(mod.shape[0])),
            pl.BlockSpec((1, d), lambda bi, i: (0, 0)),
            pl.BlockSpec(w.shape, lambda bi, i: (0, 0)),
            pl.BlockSpec((1, 2 * d), lambda bi, i: (0, 0)),
        ],
        out_specs=pl.BlockSpec((1, tm, d), lambda bi, i: (bi, i, 0)),
        out_shape=jax.ShapeDtypeStruct((bx, t, d), F32),
        compiler_params=_cparams(("parallel", "parallel"), 48),
        name="conv_pw1_glu",
    )(x, mod, g1, w, b)


def _conv_tail_kernel(u_ref, up_ref, un_ref, wdw_ref, bdw_ref, lg_ref, lb_ref, w2_ref, b2_ref, x_ref, mod_ref,
                      o_ref, buf_sc, cv_sc, *, tm):
    i = pl.program_id(1)
    last = pl.num_programs(1) - 1
    buf_sc[0:CONV_HALO, :] = jnp.where(i > 0, up_ref[0], 0.0)
    buf_sc[CONV_HALO:CONV_HALO + tm, :] = u_ref[0]
    buf_sc[CONV_HALO + tm:, :] = jnp.where(i < last, un_ref[0], 0.0)
    assert CONV_HALO - CONV_WIDTH // 2 == 1
    d = buf_sc.shape[1]
    for lb in range(d // CONV_LANES):
        ls = slice(lb * CONV_LANES, (lb + 1) * CONV_LANES)
        for rc in range(tm // CONV_ROWS):
            r0 = rc * CONV_ROWS
            acc = None
            for r in range(SUBLANES):
                part = None
                for j in range(r, CONV_WIDTH + 1, SUBLANES):
                    if j == 0:
                        continue
                    rows = slice(r0 + j - r, r0 + j - r + CONV_ROWS + SUBLANES)
                    term = buf_sc[rows, ls] * wdw_ref[j - 1:j, ls]
                    part = term if part is None else part + term
                part = part[r:r + CONV_ROWS]
                acc = part if acc is None else acc + part
            cv_sc[r0:r0 + CONV_ROWS, ls] = acc
    u = cv_sc[...] + bdw_ref[...]
    mu = jnp.mean(u, axis=-1, keepdims=True)
    uc = u - mu
    var = jnp.mean(uc * uc, axis=-1, keepdims=True)
    y = uc * lax.rsqrt(var + EPS) * lg_ref[...] + lb_ref[...]
    y = (y * jax.nn.sigmoid(y)).astype(BF)
    z = jnp.dot(y, w2_ref[...], preferred_element_type=F32) + b2_ref[...]
    o_ref[0] = x_ref[0] + mod_ref[0, 2:3, :] * z


def _conv_tail(u, wdw, bdw, lg, lb, w2, b2, x, mod, *, tm=256):
    bx, t, d = x.shape
    hb = tm // CONV_HALO
    n_halo = t // CONV_HALO
    vec = lambda: pl.BlockSpec((1, d), lambda bi, i: (0, 0))
    return pl.pallas_call(
        functools.partial(_conv_tail_kernel, tm=tm),
        grid=(bx, t // tm),
        in_specs=[
            pl.BlockSpec((1, tm, d), lambda bi, i: (bi, i, 0)),
            pl.BlockSpec((1, CONV_HALO, d), lambda bi, i: (bi, jnp.maximum(i * hb - 1, 0), 0)),
            pl.BlockSpec((1, CONV_HALO, d), lambda bi, i: (bi, jnp.minimum((i + 1) * hb, n_halo - 1), 0)),
            pl.BlockSpec(wdw.shape, lambda bi, i: (0, 0)),
            vec(), vec(), vec(),
            pl.BlockSpec(w2.shape, lambda bi, i: (0, 0)),
            vec(),
            pl.BlockSpec((1, tm, d), lambda bi, i: (bi, i, 0)),
            pl.BlockSpec((1, N_MOD, d), _mod_map(mod.shape[0])),
        ],
        out_specs=pl.BlockSpec((1, tm, d), lambda bi, i: (bi, i, 0)),
        out_shape=jax.ShapeDtypeStruct((bx, t, d), F32),
        scratch_shapes=[pltpu.VMEM((tm + 2 * CONV_HALO, d), F32), pltpu.VMEM((tm, d), F32)],
        compiler_params=_cparams(("parallel", "parallel"), 48),
        name="conv_tail",
    )(u, u, u, wdw, bdw, lg, lb, w2, b2, x, mod)


def _ft_tables(n_seq, gw):
    nb = n_seq // FT_NA
    ka = np.arange(FT_NA)[None, :, None]
    na = np.arange(FT_NA)[None, None, :]
    jb = np.arange(nb)[:, None, None]
    ph = 2 * np.pi * ((ka * (nb * na + jb)) % n_seq) / n_seq
    t1 = np.concatenate([np.cos(ph), -np.sin(ph)], axis=1)
    kb = np.arange(nb)
    ph = 2 * np.pi * ((kb[:, None] * kb[None, :]) % nb) / nb
    c3, s3 = np.cos(ph), np.sin(ph)
    t2 = np.block([[c3, s3], [-s3, c3]]) / np.sqrt(n_seq)
    m = np.arange(gw)
    ph = 2 * np.pi * ((m[:, None] * m[None, :]) % gw) / gw
    tw = np.stack([np.cos(ph), np.sin(ph)]) / np.sqrt(gw)
    return tuple(jnp.asarray(t, F32).astype(BF) for t in (t1, t2, tw))


def _ft_prep_kernel(x_ref, mod_ref, g1_ref, h_ref, *, nb):
    h = _norm_mod(x_ref[0], g1_ref[...], mod_ref[0, 0:1, :], mod_ref[0, 1:2, :])
    for a in range(h.shape[0] // nb):
        h_ref[0, :, a, :] = h[a * nb:(a + 1) * nb, :]


def _ft_prep(x, mod, g1, *, tm=1024):
    bx, t, d = x.shape
    nb = t // FT_NA
    return pl.pallas_call(
        functools.partial(_ft_prep_kernel, nb=nb),
        grid=(bx, t // tm),
        in_specs=[
            pl.BlockSpec((1, tm, d), lambda bi, i: (bi, i, 0)),
            pl.BlockSpec((1, N_MOD, d), _mod_map(mod.shape[0])),
            pl.BlockSpec((1, d), lambda bi, i: (0, 0)),
        ],
        out_specs=pl.BlockSpec((1, nb, tm // nb, d), lambda bi, i: (bi, 0, i, 0)),
        out_shape=jax.ShapeDtypeStruct((bx, nb, FT_NA, d), F32),
        compiler_params=_cparams(("parallel", "parallel"), 48),
        name="ft_prep",
    )(x, mod, g1)


def _ft_seq1_kernel(h_ref, t1_ref, y_ref, *, tb, d):
    for j in range(tb):
        y = jnp.dot(t1_ref[j], h_ref[0, j].astype(BF), preferred_element_type=F32)
        y_ref[0, :, j, :d] = y[:FT_NA]
        y_ref[0, :, j, d:] = y[FT_NA:]


def _ft_seq1(hp, t1, *, tb=8):
    bx, nb, _, d = hp.shape
    return pl.pallas_call(
        functools.partial(_ft_seq1_kernel, tb=tb, d=d),
        grid=(bx, nb // tb),
        in_specs=[
            pl.BlockSpec((1, tb, FT_NA, d), lambda bi, i: (bi, i, 0, 0)),
            pl.BlockSpec((tb, 2 * FT_NA, FT_NA), lambda bi, i: (i, 0, 0)),
        ],
        out_specs=pl.BlockSpec((1, FT_NA, tb, 2 * d), lambda bi, i: (bi, 0, i, 0)),
        out_shape=jax.ShapeDtypeStruct((bx, FT_NA, nb, 2 * d), F32),
        compiler_params=_cparams(("parallel", "parallel"), 48),
        name="ft_seq1",
    )(hp, t1)


def _ft_seq2_kernel(y_ref, t2_ref, tw_ref, w_ref, b_ref, mod_ref, o_ref, *, ta):
    d = w_ref.shape[0]
    nb = y_ref.shape[2]
    gw = tw_ref.shape[1]
    for a in range(ta):
        yb = y_ref[0, a]
        rhs = jnp.concatenate([yb[:, :d], yb[:, d:]], axis=0).astype(BF)
        aa = jnp.dot(t2_ref[...], rhs, preferred_element_type=F32).astype(BF)
        z = [jnp.dot(aa[:nb, g * gw:(g + 1) * gw], tw_ref[0], preferred_element_type=F32)
             + jnp.dot(aa[nb:, g * gw:(g + 1) * gw], tw_ref[1], preferred_element_type=F32)
             for g in range(d // gw)]
        z = jnp.concatenate(z, axis=1).astype(BF)
        yl = jnp.dot(z, w_ref[...], preferred_element_type=F32) + b_ref[...]
        o_ref[0, :, a, :] = mod_ref[0, 2:3, :] * yl


def _ft_seq2(y, t2, tw, w, b, mod, *, ta=8):
    bx, _, nb, d2 = y.shape
    d = d2 // 2
    out = pl.pallas_call(
        functools.partial(_ft_seq2_kernel, ta=ta),
        grid=(bx, FT_NA // ta),
        in_specs=[
            pl.BlockSpec((1, ta, nb, d2), lambda bi, i: (bi, i, 0, 0)),
            pl.BlockSpec(t2.shape, lambda bi, i: (0, 0)),
            pl.BlockSpec(tw.shape, lambda bi, i: (0, 0, 0)),
            pl.BlockSpec(w.shape, lambda bi, i: (0, 0)),
            pl.BlockSpec((1, d), lambda bi, i: (0, 0)),
            pl.BlockSpec((1, N_MOD, d), _mod_map(mod.shape[0])),
        ],
        out_specs=pl.BlockSpec((1, nb, ta, d), lambda bi, i: (bi, 0, i, 0)),
        out_shape=jax.ShapeDtypeStruct((bx, nb, FT_NA, d), F32),
        compiler_params=_cparams(("parallel", "parallel"), 48),
        name="ft_seq2_mix",
    )(y, t2, tw, w, b, mod)
    return out.reshape(bx, nb * FT_NA, d)


def _rope_tables(n_tok):
    t = jnp.arange(n_tok)
    row = (t // GRID_W).astype(F32)
    col = (t % GRID_W).astype(F32)
    n_axis = HEAD_DIM // 4
    inv = ROPE_THETA ** (-jnp.arange(n_axis, dtype=F32) / n_axis)
    ang = jnp.concatenate([row[:, None] * inv, col[:, None] * inv], axis=-1)
    ang = jnp.tile(jnp.repeat(ang, 2, axis=-1), (1, LANES // HEAD_DIM))
    sign = jnp.where(jnp.arange(LANES) % 2 == 0, -1.0, 1.0).astype(F32)
    return jnp.cos(ang), jnp.sin(ang) * sign


def _row(v):
    return v.reshape(1, -1)


def kernel(x, c, ctx, c_ctx, ada_w, ada_b, norm1_g, norm2_g, mlp_w1, mlp_w2, final_g, at_w_qkv, at_q_g, at_k_g, at_w_o, na_w_qkv, na_rpb, na_w_o, cv_w_pw1, cv_b_pw1, cv_w_dw, cv_b_dw, cv_ln_g, cv_ln_b, cv_w_pw2, cv_b_pw2, ft_w, ft_b):
    bsz, n_lat, d = x.shape
    depth = ada_w.shape[0]
    n_rows = n_lat // GRID_W
    n_ctx = ctx.shape[1]

    vec8 = jnp.zeros((8, d), F32).at[:bsz].set(c).at[bsz].set(c_ctx)
    mods = _ada_mods(vec8, ada_w, ada_b)

    w1m = mlp_w1.astype(BF)
    w2m = mlp_w2.astype(BF)
    hm = jnp.asarray(np.kron(np.eye(AT_HEADS), np.full((HEAD_DIM, HEAD_DIM), 1.0 / HEAD_DIM)), BF)
    h_ctx = ctx
    for i in range(depth):
        kind = i % N_MIXERS
        occ = i // N_MIXERS
        ctx_later = any((j % N_MIXERS) in (0, 1) for j in range(i + 1, depth))
        ml = mods[i, :bsz].reshape(bsz, N_MOD, d)
        mc = mods[i, bsz:bsz + 1].reshape(1, N_MOD, d)
        g1 = _row(norm1_g[i])
        delta = attn = attn_ctx = None
        if kind == 0:
            w = at_w_qkv[occ].astype(BF)
            wo = at_w_o[occ].astype(BF)
            qg = _row(jnp.tile(at_q_g[occ], AT_HEADS))
            kg = _row(jnp.tile(at_k_g[occ], AT_KV_HEADS))
            cos, sin = _rope_tables(n_lat)
            qt_l, k_l, vt_l = _gqa_proj(x, ml, g1, w, hm, qg, kg, cos, sin)
            qt_c, k_c, vt_c = _gqa_proj(h_ctx, mc, g1, w, hm, qg, kg, None, None)
            attn = (_flash(qt_l, k_c, vt_c, k_l, vt_l, tq=512), wo)
            if ctx_later:
                attn_ctx = (_flash(qt_c, k_c, vt_c, tq=n_ctx), wo)
        elif kind == 1:
            w = na_w_qkv[occ].astype(BF)
            wo = na_w_o[occ].astype(BF)
            qt_l, k_l, v_l = _na_proj(x, ml, g1, w)
            qt_c, k_c, v_c = _na_proj(h_ctx, mc, g1, w)
            quad, rowmask = _na_bias_tables(na_rpb[occ], n_rows)
            attn = (_na_attend(qt_l, k_l, v_l, k_c, v_c, quad, rowmask, n_rows), wo)
            if ctx_later:
                raise NotImplementedError("context output of a neighbourhood layer is not needed at this depth")
        elif kind == 2:
            w1 = cv_w_pw1[occ].astype(BF)
            w2 = cv_w_pw2[occ].astype(BF)
            wdw = jnp.zeros((CONV_WIDTH + 1, d), F32).at[:CONV_WIDTH].set(cv_w_dw[occ])
            cv = (wdw, _row(cv_b_dw[occ]), _row(cv_ln_g[occ]), _row(cv_ln_b[occ]), w2, _row(cv_b_pw2[occ]))
            u = _conv_pw1(x, ml, g1, w1, _row(cv_b_pw1[occ]))
            x = _conv_tail(u, *cv, x, ml)
            if ctx_later:
                raise NotImplementedError("context output of a convolution layer is not needed at this depth")
        else:
            t1, t2, tw = _ft_tables(n_lat, d // FT_GROUPS)
            y1 = _ft_seq1(_ft_prep(x, ml, g1), t1)
            delta = _ft_seq2(y1, t2, tw, ft_w[occ].astype(BF), _row(ft_b[occ]), ml)
            if ctx_later:
                raise NotImplementedError("context output of a Fourier layer is not needed at this depth")
        g2 = _row(norm2_g[i])
        is_last = i == depth - 1
        x = _mlp(x, ml, g2, w1m, w2m, i, _row(final_g) if is_last else None, delta=delta, attn=attn, tm=1024, tf=1024)
        if ctx_later:
            h_ctx = _mlp(h_ctx, mc, g2, w1m, w2m, i, attn=attn_ctx, tm=n_ctx)
    return x
```

```python
import functools

import jax
import jax.numpy as jnp
import numpy as np
from jax import lax
from jax.experimental import pallas as pl
from jax.experimental.pallas import tpu as pltpu

GRID_W = 64
N_MIXERS = 4
HEAD_DIM = 64
AT_HEADS = 16
AT_KV_HEADS = 4
AT_WIDTH = AT_HEADS * HEAD_DIM
AT_KV_WIDTH = AT_KV_HEADS * HEAD_DIM
ROPE_THETA = 10000.0
NA_HEADS = 16
NA_WIDTH = NA_HEADS * HEAD_DIM
NA_WIN_ROWS = 8
NA_WIN_COLS = 16
CONV_WIDTH = 31
FT_GROUPS = 4
N_MOD = 6
EPS = 1e-6
QK_SCALE = HEAD_DIM ** -0.5
LOG2E = float(np.log2(np.e))

LANES = 128
NA_QROWS = 4
NA_KROWS = 12
NA_CHUNK_ROWS = 4
NA_GROUPS_PER_STEP = 4
NA_LOOKAHEAD = 6
NA_EROWS = NA_KROWS + 2 * NA_QROWS
CONV_HALO = 16
CONV_ROWS = 64
CONV_LANES = 256
SUBLANES = 8
FT_NA = 64
NEG_BIG = -1e30

BF = jnp.bfloat16
F32 = jnp.float32


def _cparams(sem, vmem_mib):
    return pltpu.CompilerParams(dimension_semantics=sem, vmem_limit_bytes=vmem_mib << 20)


def _norm_mod(x, g, shift, scale):
    ms = jnp.mean(x * x, axis=-1, keepdims=True)
    return (x * lax.rsqrt(ms + EPS) * g) * (1.0 + scale) + shift


def _ada_kernel(v_ref, w_ref, b_ref, o_ref):
    v = v_ref[...]
    sv = v * jax.nn.sigmoid(v)
    o_ref[0] = jnp.dot(sv, w_ref[0], preferred_element_type=F32, precision=lax.Precision.HIGHEST) + b_ref[0]


def _ada_mods(vec8, ada_w, ada_b):
    depth, d, n = ada_w.shape
    tn = n // 2
    return pl.pallas_call(
        _ada_kernel,
        grid=(depth, n // tn),
        in_specs=[
            pl.BlockSpec((8, d), lambda l, j: (0, 0)),
            pl.BlockSpec((1, d, tn), lambda l, j: (l, 0, j)),
            pl.BlockSpec((1, 1, tn), lambda l, j: (l, 0, j)),
        ],
        out_specs=pl.BlockSpec((1, 8, tn), lambda l, j: (l, 0, j)),
        out_shape=jax.ShapeDtypeStruct((depth, 8, n), F32),
        compiler_params=_cparams(("arbitrary", "arbitrary"), 40),
        name="ada_mods",
    )(vec8, ada_w, ada_b.reshape(depth, 1, n))


def _mod_map(n_mod_rows):
    if n_mod_rows == 1:
        return lambda b, *_: (0, 0, 0)
    return lambda b, *_: (b, 0, 0)


def _gqa_proj_kernel(*refs, rope, tm):
    if rope:
        x_ref, mod_ref, g1_ref, w_ref, hm_ref, qg_ref, kg_ref, cos_ref, sin_ref, qt_ref, k_ref, vt_ref = refs
    else:
        x_ref, mod_ref, g1_ref, w_ref, hm_ref, qg_ref, kg_ref, qt_ref, k_ref, vt_ref = refs
    h = _norm_mod(x_ref[0], g1_ref[...], mod_ref[0, 0:1, :], mod_ref[0, 1:2, :]).astype(BF)
    qkv = jnp.dot(h, w_ref[...], preferred_element_type=F32)
    q = qkv[:, :AT_WIDTH]
    k = qkv[:, AT_WIDTH:AT_WIDTH + AT_KV_WIDTH]
    v = qkv[:, AT_WIDTH + AT_KV_WIDTH:]
    hm = hm_ref[...]
    q = q * lax.rsqrt(jnp.dot((q * q).astype(BF), hm, preferred_element_type=F32) + EPS) * qg_ref[...]
    hk = hm[:AT_KV_WIDTH, :AT_KV_WIDTH]
    k = k * lax.rsqrt(jnp.dot((k * k).astype(BF), hk, preferred_element_type=F32) + EPS) * kg_ref[...]
    if rope:
        cos = cos_ref[...]
        sin = sin_ref[...]
        even = (lax.broadcasted_iota(jnp.int32, (tm, LANES), 1) % 2) == 0

        def rot(t):
            partner = jnp.where(even, pltpu.roll(t, LANES - 1, 1), pltpu.roll(t, 1, 1))
            return t * cos + partner * sin
    else:
        def rot(t):
            return t
    for c in range(AT_WIDTH // LANES):
        qc = rot(q[:, c * LANES:(c + 1) * LANES]) * (QK_SCALE * LOG2E)
        qt_ref[0, c * LANES:(c + 1) * LANES, :] = qc.T.astype(BF)
    for c in range(AT_KV_WIDTH // LANES):
        kc = rot(k[:, c * LANES:(c + 1) * LANES]).astype(BF)
        k_ref[0, 2 * c] = kc[:, :HEAD_DIM]
        k_ref[0, 2 * c + 1] = kc[:, HEAD_DIM:]
    vt = v.T.astype(BF)
    for h in range(AT_KV_HEADS):
        vt_ref[0, h] = vt[h * HEAD_DIM:(h + 1) * HEAD_DIM]


def _gqa_proj(x, mod, g1, w, hm, qg, kg, cos, sin, *, tm=256):
    bx, t, d = x.shape
    rope = cos is not None
    n = w.shape[1]
    const = lambda b, i: (0, 0)
    in_specs = [
        pl.BlockSpec((1, tm, d), lambda b, i: (b, i, 0)),
        pl.BlockSpec((1, N_MOD, d), _mod_map(mod.shape[0])),
        pl.BlockSpec((1, d), const),
        pl.BlockSpec((d, n), const),
        pl.BlockSpec(hm.shape, const),
        pl.BlockSpec((1, AT_WIDTH), const),
        pl.BlockSpec((1, AT_KV_WIDTH), const),
    ]
    args = [x, mod, g1, w, hm, qg, kg]
    if rope:
        in_specs += [pl.BlockSpec((tm, LANES), lambda b, i: (i, 0))] * 2
        args += [cos, sin]
    return pl.pallas_call(
        functools.partial(_gqa_proj_kernel, rope=rope, tm=tm),
        grid=(bx, t // tm),
        in_specs=in_specs,
        out_specs=[
            pl.BlockSpec((1, AT_WIDTH, tm), lambda b, i: (b, 0, i)),
            pl.BlockSpec((1, AT_KV_HEADS, tm, HEAD_DIM), lambda b, i: (b, 0, i, 0)),
            pl.BlockSpec((1, AT_KV_HEADS, HEAD_DIM, tm), lambda b, i: (b, 0, 0, i)),
        ],
        out_shape=[
            jax.ShapeDtypeStruct((bx, AT_WIDTH, t), BF),
            jax.ShapeDtypeStruct((bx, AT_KV_HEADS, t, HEAD_DIM), BF),
            jax.ShapeDtypeStruct((bx, AT_KV_HEADS, HEAD_DIM, t), BF),
        ],
        compiler_params=_cparams(("parallel", "parallel"), 48),
        name="gqa_proj_rope" if rope else "gqa_proj_ctx",
    )(*args)


def _flash_kernel(*refs, tk, lookahead, with_latents):
    if with_latents:
        qt_ref, kc_ref, vtc_ref, kl_ref, vtl_ref, ot_ref, m_sc, l_sc, acc_sc = refs
        n_lat = kl_ref.shape[2] // tk
    else:
        qt_ref, kc_ref, vtc_ref, ot_ref, m_sc, l_sc, acc_sc = refs
        n_lat = 0
    n_heads = qt_ref.shape[1] // HEAD_DIM
    qs = [qt_ref[0, hh * HEAD_DIM:(hh + 1) * HEAD_DIM, :] for hh in range(n_heads)]
    chunks = [(kc_ref.at[0, 0], vtc_ref.at[0, 0])]
    chunks += [(kl_ref.at[0, 0, c * tk:(c + 1) * tk, :], vtl_ref.at[0, 0, :, c * tk:(c + 1) * tk]) for c in range(n_lat)]

    def scores(hh, c):
        return jnp.dot(chunks[c][0][...], qs[hh], preferred_element_type=F32)

    def weighted(c, s, m):
        p = jnp.exp2(s - m)
        psum = p.reshape(p.shape[0] // SUBLANES, SUBLANES, p.shape[1]).sum(axis=0)
        return jnp.dot(chunks[c][1][...], p.astype(BF), preferred_element_type=F32), psum

    units = [(hh, c) for hh in range(n_heads) for c in range(len(chunks))]
    pending = [scores(*u) for u in units[:lookahead]]
    m0, acc, den8 = {}, {}, {}
    for n, (hh, c) in enumerate(units):
        if n + lookahead < len(units):
            pending.append(scores(*units[n + lookahead]))
        s = pending.pop(0)
        if c == 0:
            m0[hh] = jnp.max(s, axis=0, keepdims=True)
        pv, psum = weighted(c, s, m0[hh])
        acc[hh] = pv if c == 0 else acc[hh] + pv
        den8[hh] = psum if c == 0 else den8[hh] + psum
    denom = [jnp.sum(den8[hh], axis=0, keepdims=True) for hh in range(n_heads)]
    bad = jnp.zeros_like(denom[0])
    for dn in denom:
        bad = jnp.where(jnp.isfinite(dn), bad, 1.0)
    overflowed = jnp.max(bad) > 0.0

    @pl.when(jnp.logical_not(overflowed))
    def _():
        for hh in range(n_heads):
            ot_ref[0, hh * HEAD_DIM:(hh + 1) * HEAD_DIM, :] = (acc[hh] * (1.0 / denom[hh])).astype(BF)

    @pl.when(overflowed)
    def _():
        for hh in range(n_heads):
            m_sc[...] = jnp.full(m_sc.shape, NEG_BIG, F32)
            l_sc[...] = jnp.zeros(l_sc.shape, F32)
            acc_sc[...] = jnp.zeros(acc_sc.shape, F32)

            def update(k, vt, q=qs[hh]):
                s = jnp.dot(k, q, preferred_element_type=F32)
                m_prev = m_sc[...]
                m_new = jnp.maximum(m_prev, jnp.max(s, axis=0, keepdims=True))
                alpha = jnp.exp2(m_prev - m_new)
                p = jnp.exp2(s - m_new)
                l_sc[...] = alpha * l_sc[...] + jnp.sum(p, axis=0, keepdims=True)
                acc_sc[...] = alpha * acc_sc[...] + jnp.dot(vt, p.astype(BF), preferred_element_type=F32)
                m_sc[...] = m_new

            update(kc_ref[0, 0], vtc_ref[0, 0])

            def body(c, carry, update=update):
                off = pl.multiple_of(c * tk, tk)
                update(kl_ref[0, 0, pl.ds(off, tk), :], vtl_ref[0, 0, :, pl.ds(off, tk)])
                return carry

            if with_latents:
                lax.fori_loop(0, n_lat, body, 0)
            ot_ref[0, hh * HEAD_DIM:(hh + 1) * HEAD_DIM, :] = (acc_sc[...] * (1.0 / l_sc[...])).astype(BF)


def _flash(qt, kc, vtc, kl=None, vtl=None, *, tq, tk=256, lookahead=3, heads_per_step=4):
    b, w, t = qt.shape
    kv = kc.shape[1]
    heads = w // HEAD_DIM
    group = heads // kv
    hps = heads_per_step
    assert group % hps == 0
    kv_map = lambda bi, h, i: (bi, h * hps // group, 0, 0)
    in_specs = [
        pl.BlockSpec((1, hps * HEAD_DIM, tq), lambda bi, h, i: (bi, h, i)),
        pl.BlockSpec((1, 1) + kc.shape[2:], kv_map),
        pl.BlockSpec((1, 1) + vtc.shape[2:], kv_map),
    ]
    args = [qt, kc, vtc]
    if kl is not None:
        in_specs += [pl.BlockSpec((1, 1) + kl.shape[2:], kv_map), pl.BlockSpec((1, 1) + vtl.shape[2:], kv_map)]
        args += [kl, vtl]
    return pl.pallas_call(
        functools.partial(_flash_kernel, tk=tk, lookahead=lookahead, with_latents=kl is not None),
        grid=(b, heads // hps, t // tq),
        in_specs=in_specs,
        out_specs=pl.BlockSpec((1, hps * HEAD_DIM, tq), lambda bi, h, i: (bi, h, i)),
        out_shape=jax.ShapeDtypeStruct((b, w, t), BF),
        scratch_shapes=[pltpu.VMEM((1, tq), F32), pltpu.VMEM((1, tq), F32), pltpu.VMEM((HEAD_DIM, tq), F32)],
        compiler_params=_cparams(("parallel", "parallel", "parallel"), 48),
        name="gqa_flash" if kl is not None else "gqa_flash_ctx",
    )(*args)


def _mlp_kernel(*refs, final, pending):
    refs = list(refs)
    x_ref = refs.pop(0)
    d_ref = refs.pop(0) if pending == "delta" else None
    ot_ref, wo_ref = (refs.pop(0), refs.pop(0)) if pending == "attn" else (None, None)
    mod_ref, g2_ref, w1_ref, w2_ref = refs[:4]
    fg_ref = refs[4] if final else None
    o_ref, xs_sc, h_sc, acc_sc = refs[-4:]
    j = pl.program_id(2)

    @pl.when(j == 0)
    def _():
        xs = x_ref[0]
        if pending == "delta":
            xs = xs + d_ref[0]
        elif pending == "attn":
            y = lax.dot_general(ot_ref[0], wo_ref[...], (((0,), (0,)), ((), ())), preferred_element_type=F32)
            xs = xs + mod_ref[0, 2:3, :] * y
        xs_sc[...] = xs
        h_sc[...] = _norm_mod(xs, g2_ref[...], mod_ref[0, 3:4, :], mod_ref[0, 4:5, :]).astype(BF)
        acc_sc[...] = jnp.zeros(acc_sc.shape, F32)

    a = jnp.dot(h_sc[...], w1_ref[...], preferred_element_type=F32)
    a = jnp.square(jnp.maximum(a, 0.0)).astype(BF)
    acc_sc[...] += jnp.dot(a, w2_ref[...], preferred_element_type=F32)

    @pl.when(j == pl.num_programs(2) - 1)
    def _():
        y = xs_sc[...] + mod_ref[0, 5:6, :] * acc_sc[...]
        if final:
            ms = jnp.mean(y * y, axis=-1, keepdims=True)
            y = y * lax.rsqrt(ms + EPS) * fg_ref[...]
        o_ref[0] = y


def _mlp(x, mod, g2, w1, w2, layer, final_g=None, *, delta=None, attn=None, tm, tf=512):
    b, t, d = x.shape
    f = w1.shape[2]
    final = final_g is not None
    row_spec = pl.BlockSpec((1, tm, d), lambda bi, i, j: (bi, i, 0))
    in_specs, args, pending = [row_spec], [x], None
    if delta is not None:
        in_specs, args, pending = in_specs + [row_spec], args + [delta], "delta"
    if attn is not None:
        ot, wo = attn
        in_specs += [pl.BlockSpec((1, ot.shape[1], tm), lambda bi, i, j: (bi, 0, i)),
                     pl.BlockSpec(wo.shape, lambda bi, i, j: (0, 0))]
        args, pending = args + [ot, wo], "attn"
    in_specs += [
        pl.BlockSpec((1, N_MOD, d), _mod_map(mod.shape[0])),
        pl.BlockSpec((1, d), lambda bi, i, j: (0, 0)),
        pl.BlockSpec((None, d, tf), lambda bi, i, j: (layer, 0, j)),
        pl.BlockSpec((None, tf, d), lambda bi, i, j: (layer, j, 0)),
    ]
    args += [mod, g2, w1, w2]
    if final:
        in_specs.append(pl.BlockSpec((1, d), lambda bi, i, j: (0, 0)))
        args.append(final_g)
    return pl.pallas_call(
        functools.partial(_mlp_kernel, final=final, pending=pending),
        grid=(b, t // tm, f // tf),
        in_specs=in_specs,
        out_specs=pl.BlockSpec((1, tm, d), lambda bi, i, j: (bi, i, 0)),
        out_shape=jax.ShapeDtypeStruct((b, t, d), F32),
        scratch_shapes=[pltpu.VMEM((tm, d), F32), pltpu.VMEM((tm, d), BF), pltpu.VMEM((tm, d), F32)],
        compiler_params=_cparams(("parallel", "parallel", "arbitrary"), 56),
        name="mlp_final" if final else "mlp",
    )(*args)


def _na_proj_kernel(x_ref, mod_ref, g1_ref, w_ref, qt_ref, k_ref, v_ref):
    h = _norm_mod(x_ref[0], g1_ref[...], mod_ref[0, 0:1, :], mod_ref[0, 1:2, :]).astype(BF)
    qkv = jnp.dot(h, w_ref[...], preferred_element_type=F32)
    for c in range(NA_WIDTH // LANES):
        qt_ref[0, c * LANES:(c + 1) * LANES, :] = (qkv[:, c * LANES:(c + 1) * LANES] * (QK_SCALE * LOG2E)).T.astype(BF)
    k_ref[0] = qkv[:, NA_WIDTH:2 * NA_WIDTH].astype(BF)
    v_ref[0] = qkv[:, 2 * NA_WIDTH:].astype(BF)


def _na_proj(x, mod, g1, w, *, tm=256):
    bx, t, d = x.shape
    return pl.pallas_call(
        _na_proj_kernel,
        grid=(bx, t // tm),
        in_specs=[
            pl.BlockSpec((1, tm, d), lambda b, i: (b, i, 0)),
            pl.BlockSpec((1, N_MOD, d), _mod_map(mod.shape[0])),
            pl.BlockSpec((1, d), lambda b, i: (0, 0)),
            pl.BlockSpec(w.shape, lambda b, i: (0, 0)),
        ],
        out_specs=[
            pl.BlockSpec((1, NA_WIDTH, tm), lambda b, i: (b, 0, i)),
            pl.BlockSpec((1, tm, NA_WIDTH), lambda b, i: (b, i, 0)),
            pl.BlockSpec((1, tm, NA_WIDTH), lambda b, i: (b, i, 0)),
        ],
        out_shape=[
            jax.ShapeDtypeStruct((bx, NA_WIDTH, t), BF),
            jax.ShapeDtypeStruct((bx, t, NA_WIDTH), BF),
            jax.ShapeDtypeStruct((bx, t, NA_WIDTH), BF),
        ],
        compiler_params=_cparams(("parallel", "parallel"), 48),
        name="na_proj",
    )(x, mod, g1, w)


def _na_key_base(g, n_groups):
    return jnp.clip(NA_QROWS * g - NA_WIN_ROWS // 2, 0, NA_QROWS * n_groups - NA_KROWS)


def _na_kernel(*refs, n_groups, per_step):
    qt_ref, k_ref, v_ref, kc_ref, vc_ref, quad_ref = refs[:6]
    mask_refs = refs[6:6 + per_step]
    ot_ref = refs[6 + per_step]
    tq = NA_QROWS * GRID_W
    ck = NA_CHUNK_ROWS * GRID_W
    n_nb = NA_KROWS // NA_CHUNK_ROWS
    tn = (((0,), (0,)), ((), ()))
    upper = lax.broadcasted_iota(jnp.int32, (2 * HEAD_DIM, tq), 0) < HEAD_DIM
    geo = []
    for gi in range(per_step):
        g = pl.program_id(2) * per_step + gi
        kb_row = _na_key_base(g, n_groups)
        q2 = qt_ref[0, :, gi * tq:(gi + 1) * tq]
        qh = (jnp.where(upper, q2, jnp.zeros_like(q2)), jnp.where(upper, jnp.zeros_like(q2), q2))
        geo.append((pl.multiple_of(kb_row * GRID_W, GRID_W), kb_row - NA_QROWS * g + NA_WIN_ROWS, qh))

    def scores(gi, half, j):
        kb, e0, qh = geo[gi]
        if j < 0:
            return jnp.dot(kc_ref[0], qh[half], preferred_element_type=F32)
        bias = jnp.concatenate([quad_ref[half, e0 + NA_CHUNK_ROWS * j + i] for i in range(NA_CHUNK_ROWS)], axis=0)
        bias = bias + mask_refs[gi][0, j * ck:(j + 1) * ck, :]
        return jnp.dot(k_ref[0, pl.ds(kb + j * ck, ck), :], qh[half], preferred_element_type=F32) + bias

    def values(gi, j):
        return vc_ref[0] if j < 0 else v_ref[0, pl.ds(geo[gi][0] + j * ck, ck), :]

    def finish(gi, o0, l0, o1, l1):
        ot_ref[0, :, gi * tq:(gi + 1) * tq] = jnp.where(upper, o0 * (1.0 / l0), o1 * (1.0 / l1)).astype(BF)

    heads = [(gi, half) for gi in range(per_step) for half in range(2)]
    units = [(gi, half, j) for j in range(-1, n_nb) for gi, half in heads]
    lookahead = max(NA_LOOKAHEAD, len(heads))
    pending = [scores(*u) for u in units[:lookahead]]
    m0 = {u[:2]: jnp.max(s, axis=0, keepdims=True) for u, s in zip(units[:len(heads)], pending)}
    acc = {}
    den = {}
    for n, (gi, half, j) in enumerate(units):
        if n + lookahead < len(units):
            pending.append(scores(*units[n + lookahead]))
        p = jnp.exp2(pending.pop(0) - m0[gi, half])
        pv = lax.dot_general(values(gi, j), p.astype(BF), tn, preferred_element_type=F32)
        ps = jnp.sum(p, axis=0, keepdims=True)
        acc[gi, half] = pv if j < 0 else acc[gi, half] + pv
        den[gi, half] = ps if j < 0 else den[gi, half] + ps
    bad = jnp.zeros((1, tq), F32)
    for key in heads:
        bad = jnp.where(jnp.isfinite(den[key]), bad, 1.0)
    overflowed = jnp.max(bad) > 0.0

    @pl.when(jnp.logical_not(overflowed))
    def _():
        for gi in range(per_step):
            finish(gi, acc[gi, 0], den[gi, 0], acc[gi, 1], den[gi, 1])

    @pl.when(overflowed)
    def _():
        for gi in range(per_step):
            outs = []
            for half in range(2):
                s_all = [scores(gi, half, j) for j in range(-1, n_nb)]
                m = s_all[0].max(axis=0, keepdims=True)
                for s in s_all[1:]:
                    m = jnp.maximum(m, jnp.max(s, axis=0, keepdims=True))
                o = None
                l = None
                for j, s in zip(range(-1, n_nb), s_all):
                    p = jnp.exp2(s - m)
                    pv = lax.dot_general(values(gi, j), p.astype(BF), tn, preferred_element_type=F32)
                    ps = jnp.sum(p, axis=0, keepdims=True)
                    o = pv if o is None else o + pv
                    l = ps if l is None else l + ps
                outs += [o, l]
            finish(gi, *outs)


def _na_quad_kernel(rpb_ref, quad_ref):
    h = pl.program_id(0)
    n_dr, n_dc = rpb_ref.shape[1:]
    kc = lax.broadcasted_iota(jnp.int32, (GRID_W, GRID_W), 0)
    qc = lax.broadcasted_iota(jnp.int32, (GRID_W, GRID_W), 1)
    dc = kc - qc + NA_WIN_COLS - 1
    cs = jnp.clip(qc - NA_WIN_COLS // 2, 0, GRID_W - NA_WIN_COLS)
    col_ok = (kc >= cs) & (kc < cs + NA_WIN_COLS)
    toeplitz = []
    for dr in range(n_dr):
        t = jnp.zeros((GRID_W, GRID_W), F32)
        for c in range(n_dc):
            t = jnp.where(dc == c, rpb_ref[h, dr, c], t)
        toeplitz.append(jnp.where(col_ok, t * LOG2E, NEG_BIG))
    for e in range(NA_EROWS):
        for a in range(NA_QROWS):
            dr = e - a - 1
            blk = toeplitz[dr] if 0 <= dr < n_dr else jnp.zeros((GRID_W, GRID_W), F32)
            quad_ref[0, e, :, a * GRID_W:(a + 1) * GRID_W] = blk


def _na_bias_tables(rpb, n_rows):
    n_groups = n_rows // NA_QROWS
    n_h = rpb.shape[0]
    quad = pl.pallas_call(
        _na_quad_kernel,
        grid=(n_h,),
        in_specs=[pl.BlockSpec(memory_space=pltpu.SMEM)],
        out_specs=pl.BlockSpec((1, NA_EROWS, GRID_W, NA_QROWS * GRID_W), lambda h: (h, 0, 0, 0)),
        out_shape=jax.ShapeDtypeStruct((n_h, NA_EROWS, GRID_W, NA_QROWS * GRID_W), F32),
        compiler_params=_cparams(("parallel",), 32),
        name="na_bias_quad",
    )(rpb)
    masks = []
    for g in (0, 1, n_groups - 1):
        r = NA_QROWS * g + np.arange(NA_QROWS)[None, None, :, None]
        kb = int(np.clip(NA_QROWS * g - NA_WIN_ROWS // 2, 0, n_rows - NA_KROWS))
        krow = kb + np.arange(NA_KROWS)[:, None, None, None]
        rs = np.clip(r - NA_WIN_ROWS // 2, 0, n_rows - NA_WIN_ROWS)
        ok = np.broadcast_to((krow >= rs) & (krow < rs + NA_WIN_ROWS), (NA_KROWS, GRID_W, NA_QROWS, GRID_W))
        masks.append(np.where(ok, 0.0, NEG_BIG).reshape(NA_KROWS * GRID_W, NA_QROWS * GRID_W))
    return quad, jnp.asarray(np.stack(masks), F32)


def _na_attend(qt, k, v, kc, vc, quad, rowmask, n_rows, *, per_step=NA_GROUPS_PER_STEP):
    b, w, s = qt.shape
    c = kc.shape[1]
    n_groups = n_rows // NA_QROWS
    tq = NA_QROWS * GRID_W
    pair = 2 * HEAD_DIM

    def mask_spec(gi):
        def index_map(bi, hp, st):
            g = st * per_step + gi
            return (jnp.where(g == 0, 0, jnp.where(g == n_groups - 1, 2, 1)), 0, 0)
        return pl.BlockSpec((1, NA_KROWS * GRID_W, tq), index_map)

    return pl.pallas_call(
        functools.partial(_na_kernel, n_groups=n_groups, per_step=per_step),
        grid=(b, w // pair, n_groups // per_step),
        in_specs=[
            pl.BlockSpec((1, pair, per_step * tq), lambda bi, hp, st: (bi, hp, st)),
            pl.BlockSpec((1, s, pair), lambda bi, hp, st: (bi, 0, hp)),
            pl.BlockSpec((1, s, pair), lambda bi, hp, st: (bi, 0, hp)),
            pl.BlockSpec((1, c, pair), lambda bi, hp, st: (bi, 0, hp)),
            pl.BlockSpec((1, c, pair), lambda bi, hp, st: (bi, 0, hp)),
            pl.BlockSpec((2, NA_EROWS, GRID_W, tq), lambda bi, hp, st: (hp, 0, 0, 0)),
        ] + [mask_spec(gi) for gi in range(per_step)],
        out_specs=pl.BlockSpec((1, pair, per_step * tq), lambda bi, hp, st: (bi, hp, st)),
        out_shape=jax.ShapeDtypeStruct((b, w, s), BF),
        compiler_params=_cparams(("parallel", "parallel", "arbitrary"), 48),
        name="na_attend",
    )(qt, k, v, kc, vc, quad, *([rowmask] * per_step))


def _conv_pw1_kernel(x_ref, mod_ref, g1_ref, w_ref, b_ref, u_ref):
    h = _norm_mod(x_ref[0], g1_ref[...], mod_ref[0, 0:1, :], mod_ref[0, 1:2, :]).astype(BF)
    ag = jnp.dot(h, w_ref[...], preferred_element_type=F32) + b_ref[...]
    d = u_ref.shape[2]
    u_ref[0] = ag[:, :d] * jax.nn.sigmoid(ag[:, d:])


def _conv_pw1(x, mod, g1, w, b, *, tm=512):
    bx, t, d = x.shape
    return pl.pallas_call(
        _conv_pw1_kernel,
        grid=(bx, t // tm),
        in_specs=[
            pl.BlockSpec((1, tm, d), lambda bi, i: (bi, i, 0)),
            pl.BlockSpec((1, N_MOD, d), _mod_map(mod.shape[0])),
            pl.BlockSpec((1, d), lambda bi, i: (0, 0)),
            pl.BlockSpec(w.shape, lambda bi, i: (0, 0)),
            pl.BlockSpec((1, 2 * d), lambda bi, i: (0, 0)),
        ],
        out_specs=pl.BlockSpec((1, tm, d), lambda bi, i: (bi, i, 0)),
        out_shape=jax.ShapeDtypeStruct((bx, t, d), F32),
        compiler_params=_cparams(("parallel", "parallel"), 48),
        name="conv_pw1_glu",
    )(x, mod, g1, w, b)


def _conv_tail_kernel(u_ref, up_ref, un_ref, wdw_ref, bdw_ref, lg_ref, lb_ref, w2_ref, b2_ref, x_ref, mod_ref,
                      o_ref, buf_sc, cv_sc, *, tm):
    i = pl.program_id(1)
    last = pl.num_programs(1) - 1
    buf_sc[0:CONV_HALO, :] = jnp.where(i > 0, up_ref[0], 0.0)
    buf_sc[CONV_HALO:CONV_HALO + tm, :] = u_ref[0]
    buf_sc[CONV_HALO + tm:, :] = jnp.where(i < last, un_ref[0], 0.0)
    assert CONV_HALO - CONV_WIDTH // 2 == 1
    d = buf_sc.shape[1]
    for lb in range(d // CONV_LANES):
        ls = slice(lb * CONV_LANES, (lb + 1) * CONV_LANES)
        for rc in range(tm // CONV_ROWS):
            r0 = rc * CONV_ROWS
            acc = None
            for r in range(SUBLANES):
                part = None
                for j in range(r, CONV_WIDTH + 1, SUBLANES):
                    if j == 0:
                        continue
                    rows = slice(r0 + j - r, r0 + j - r + CONV_ROWS + SUBLANES)
                    term = buf_sc[rows, ls] * wdw_ref[j - 1:j, ls]
                    part = term if part is None else part + term
                part = part[r:r + CONV_ROWS]
                acc = part if acc is None else acc + part
            cv_sc[r0:r0 + CONV_ROWS, ls] = acc
    u = cv_sc[...] + bdw_ref[...]
    mu = jnp.mean(u, axis=-1, keepdims=True)
    uc = u - mu
    var = jnp.mean(uc * uc, axis=-1, keepdims=True)
    y = uc * lax.rsqrt(var + EPS) * lg_ref[...] + lb_ref[...]
    y = (y * jax.nn.sigmoid(y)).astype(BF)
    z = jnp.dot(y, w2_ref[...], preferred_element_type=F32) + b2_ref[...]
    o_ref[0] = x_ref[0] + mod_ref[0, 2:3, :] * z


def _conv_tail(u, wdw, bdw, lg, lb, w2, b2, x, mod, *, tm=256):
    bx, t, d = x.shape
    hb = tm // CONV_HALO
    n_halo = t // CONV_HALO
    vec = lambda: pl.BlockSpec((1, d), lambda bi, i: (0, 0))
    return pl.pallas_call(
        functools.partial(_conv_tail_kernel, tm=tm),
        grid=(bx, t // tm),
        in_specs=[
            pl.BlockSpec((1, tm, d), lambda bi, i: (bi, i, 0)),
            pl.BlockSpec((1, CONV_HALO, d), lambda bi, i: (bi, jnp.maximum(i * hb - 1, 0), 0)),
            pl.BlockSpec((1, CONV_HALO, d), lambda bi, i: (bi, jnp.minimum((i + 1) * hb, n_halo - 1), 0)),
            pl.BlockSpec(wdw.shape, lambda bi, i: (0, 0)),
            vec(), vec(), vec(),
            pl.BlockSpec(w2.shape, lambda bi, i: (0, 0)),
            vec(),
            pl.BlockSpec((1, tm, d), lambda bi, i: (bi, i, 0)),
            pl.BlockSpec((1, N_MOD, d), _mod_map(mod.shape[0])),
        ],
        out_specs=pl.BlockSpec((1, tm, d), lambda bi, i: (bi, i, 0)),
        out_shape=jax.ShapeDtypeStruct((bx, t, d), F32),
        scratch_shapes=[pltpu.VMEM((tm + 2 * CONV_HALO, d), F32), pltpu.VMEM((tm, d), F32)],
        compiler_params=_cparams(("parallel", "parallel"), 48),
        name="conv_tail",
    )(u, u, u, wdw, bdw, lg, lb, w2, b2, x, mod)


def _ft_tables(n_seq, gw):
    nb = n_seq // FT_NA
    ka = np.arange(FT_NA)[None, :, None]
    na = np.arange(FT_NA)[None, None, :]
    jb = np.arange(nb)[:, None, None]
    ph = 2 * np.pi * ((ka * (nb * na + jb)) % n_seq) / n_seq
    t1 = np.concatenate([np.cos(ph), -np.sin(ph)], axis=1)
    kb = np.arange(nb)
    ph = 2 * np.pi * ((kb[:, None] * kb[None, :]) % nb) / nb
    c3, s3 = np.cos(ph), np.sin(ph)
    t2 = np.block([[c3, s3], [-s3, c3]]) / np.sqrt(n_seq)
    m = np.arange(gw)
    ph = 2 * np.pi * ((m[:, None] * m[None, :]) % gw) / gw
    tw = np.stack([np.cos(ph), np.sin(ph)]) / np.sqrt(gw)
    return tuple(jnp.asarray(t, F32).astype(BF) for t in (t1, t2, tw))


def _ft_prep_kernel(x_ref, mod_ref, g1_ref, h_ref, *, nb):
    h = _norm_mod(x_ref[0], g1_ref[...], mod_ref[0, 0:1, :], mod_ref[0, 1:2, :])
    for a in range(h.shape[0] // nb):
        h_ref[0, :, a, :] = h[a * nb:(a + 1) * nb, :]


def _ft_prep(x, mod, g1, *, tm=1024):
    bx, t, d = x.shape
    nb = t // FT_NA
    return pl.pallas_call(
        functools.partial(_ft_prep_kernel, nb=nb),
        grid=(bx, t // tm),
        in_specs=[
            pl.BlockSpec((1, tm, d), lambda bi, i: (bi, i, 0)),
            pl.BlockSpec((1, N_MOD, d), _mod_map(mod.shape[0])),
            pl.BlockSpec((1, d), lambda bi, i: (0, 0)),
        ],
        out_specs=pl.BlockSpec((1, nb, tm // nb, d), lambda bi, i: (bi, 0, i, 0)),
        out_shape=jax.ShapeDtypeStruct((bx, nb, FT_NA, d), F32),
        compiler_params=_cparams(("parallel", "parallel"), 48),
        name="ft_prep",
    )(x, mod, g1)


def _ft_seq1_kernel(h_ref, t1_ref, y_ref, *, tb, d):
    for j in range(tb):
        y = jnp.dot(t1_ref[j], h_ref[0, j].astype(BF), preferred_element_type=F32)
        y_ref[0, :, j, :d] = y[:FT_NA]
        y_ref[0, :, j, d:] = y[FT_NA:]


def _ft_seq1(hp, t1, *, tb=8):
    bx, nb, _, d = hp.shape
    return pl.pallas_call(
        functools.partial(_ft_seq1_kernel, tb=tb, d=d),
        grid=(bx, nb // tb),
        in_specs=[
            pl.BlockSpec((1, tb, FT_NA, d), lambda bi, i: (bi, i, 0, 0)),
            pl.BlockSpec((tb, 2 * FT_NA, FT_NA), lambda bi, i: (i, 0, 0)),
        ],
        out_specs=pl.BlockSpec((1, FT_NA, tb, 2 * d), lambda bi, i: (bi, 0, i, 0)),
        out_shape=jax.ShapeDtypeStruct((bx, FT_NA, nb, 2 * d), F32),
        compiler_params=_cparams(("parallel", "parallel"), 48),
        name="ft_seq1",
    )(hp, t1)


def _ft_seq2_kernel(y_ref, t2_ref, tw_ref, w_ref, b_ref, mod_ref, o_ref, *, ta):
    d = w_ref.shape[0]
    nb = y_ref.shape[2]
    gw = tw_ref.shape[1]
    for a in range(ta):
        yb = y_ref[0, a]
        rhs = jnp.concatenate([yb[:, :d], yb[:, d:]], axis=0).astype(BF)
        aa = jnp.dot(t2_ref[...], rhs, preferred_element_type=F32).astype(BF)
        z = [jnp.dot(aa[:nb, g * gw:(g + 1) * gw], tw_ref[0], preferred_element_type=F32)
             + jnp.dot(aa[nb:, g * gw:(g + 1) * gw], tw_ref[1], preferred_element_type=F32)
             for g in range(d // gw)]
        z = jnp.concatenate(z, axis=1).astype(BF)
        yl = jnp.dot(z, w_ref[...], preferred_element_type=F32) + b_ref[...]
        o_ref[0, :, a, :] = mod_ref[0, 2:3, :] * yl


def _ft_seq2(y, t2, tw, w, b, mod, *, ta=8):
    bx, _, nb, d2 = y.shape
    d = d2 // 2
    out = pl.pallas_call(
        functools.partial(_ft_seq2_kernel, ta=ta),
        grid=(bx, FT_NA // ta),
        in_specs=[
            pl.BlockSpec((1, ta, nb, d2), lambda bi, i: (bi, i, 0, 0)),
            pl.BlockSpec(t2.shape, lambda bi, i: (0, 0)),
            pl.BlockSpec(tw.shape, lambda bi, i: (0, 0, 0)),
            pl.BlockSpec(w.shape, lambda bi, i: (0, 0)),
            pl.BlockSpec((1, d), lambda bi, i: (0, 0)),
            pl.BlockSpec((1, N_MOD, d), _mod_map(mod.shape[0])),
        ],
        out_specs=pl.BlockSpec((1, nb, ta, d), lambda bi, i: (bi, 0, i, 0)),
        out_shape=jax.ShapeDtypeStruct((bx, nb, FT_NA, d), F32),
        compiler_params=_cparams(("parallel", "parallel"), 48),
        name="ft_seq2_mix",
    )(y, t2, tw, w, b, mod)
    return out.reshape(bx, nb * FT_NA, d)


def _rope_tables(n_tok):
    t = jnp.arange(n_tok)
    row = (t // GRID_W).astype(F32)
    col = (t % GRID_W).astype(F32)
    n_axis = HEAD_DIM // 4
    inv = ROPE_THETA ** (-jnp.arange(n_axis, dtype=F32) / n_axis)
    ang = jnp.concatenate([row[:, None] * inv, col[:, None] * inv], axis=-1)
    ang = jnp.tile(jnp.repeat(ang, 2, axis=-1), (1, LANES // HEAD_DIM))
    sign = jnp.where(jnp.arange(LANES) % 2 == 0, -1.0, 1.0).astype(F32)
    return jnp.cos(ang), jnp.sin(ang) * sign


def _row(v):
    return v.reshape(1, -1)


def kernel(x, c, ctx, c_ctx, ada_w, ada_b, norm1_g, norm2_g, mlp_w1, mlp_w2, final_g, at_w_qkv, at_q_g, at_k_g, at_w_o, na_w_qkv, na_rpb, na_w_o, cv_w_pw1, cv_b_pw1, cv_w_dw, cv_b_dw, cv_ln_g, cv_ln_b, cv_w_pw2, cv_b_pw2, ft_w, ft_b):
    bsz, n_lat, d = x.shape
    depth = ada_w.shape[0]
    n_rows = n_lat // GRID_W
    n_ctx = ctx.shape[1]

    vec8 = jnp.zeros((8, d), F32).at[:bsz].set(c).at[bsz].set(c_ctx)
    mods = _ada_mods(vec8, ada_w, ada_b)

    w1m = mlp_w1.astype(BF)
    w2m = mlp_w2.astype(BF)
    hm = jnp.asarray(np.kron(np.eye(AT_HEADS), np.full((HEAD_DIM, HEAD_DIM), 1.0 / HEAD_DIM)), BF)
    h_ctx = ctx
    for i in range(depth):
        kind = i % N_MIXERS
        occ = i // N_MIXERS
        ctx_later = any((j % N_MIXERS) in (0, 1) for j in range(i + 1, depth))
        ml = mods[i, :bsz].reshape(bsz, N_MOD, d)
        mc = mods[i, bsz:bsz + 1].reshape(1, N_MOD, d)
        g1 = _row(norm1_g[i])
        delta = attn = attn_ctx = None
        if kind == 0:
            w = at_w_qkv[occ].astype(BF)
            wo = at_w_o[occ].astype(BF)
            qg = _row(jnp.tile(at_q_g[occ], AT_HEADS))
            kg = _row(jnp.tile(at_k_g[occ], AT_KV_HEADS))
            cos, sin = _rope_tables(n_lat)
            qt_l, k_l, vt_l = _gqa_proj(x, ml, g1, w, hm, qg, kg, cos, sin)
            qt_c, k_c, vt_c = _gqa_proj(h_ctx, mc, g1, w, hm, qg, kg, None, None)
            attn = (_flash(qt_l, k_c, vt_c, k_l, vt_l, tq=512), wo)
            if ctx_later:
                attn_ctx = (_flash(qt_c, k_c, vt_c, tq=n_ctx), wo)
        elif kind == 1:
            w = na_w_qkv[occ].astype(BF)
            wo = na_w_o[occ].astype(BF)
            qt_l, k_l, v_l = _na_proj(x, ml, g1, w)
            qt_c, k_c, v_c = _na_proj(h_ctx, mc, g1, w)
            quad, rowmask = _na_bias_tables(na_rpb[occ], n_rows)
            attn = (_na_attend(qt_l, k_l, v_l, k_c, v_c, quad, rowmask, n_rows), wo)
            if ctx_later:
                raise NotImplementedError("context output of a neighbourhood layer is not needed at this depth")
        elif kind == 2:
            w1 = cv_w_pw1[occ].astype(BF)
            w2 = cv_w_pw2[occ].astype(BF)
            wdw = jnp.zeros((CONV_WIDTH + 1, d), F32).at[:CONV_WIDTH].set(cv_w_dw[occ])
            cv = (wdw, _row(cv_b_dw[occ]), _row(cv_ln_g[occ]), _row(cv_ln_b[occ]), w2, _row(cv_b_pw2[occ]))
            u = _conv_pw1(x, ml, g1, w1, _row(cv_b_pw1[occ]))
            x = _conv_tail(u, *cv, x, ml)
            if ctx_later:
                raise NotImplementedError("context output of a convolution layer is not needed at this depth")
        else:
            t1, t2, tw = _ft_tables(n_lat, d // FT_GROUPS)
            y1 = _ft_seq1(_ft_prep(x, ml, g1), t1)
            delta = _ft_seq2(y1, t2, tw, ft_w[occ].astype(BF), _row(ft_b[occ]), ml)
            if ctx_later:
                raise NotImplementedError("context output of a Fourier layer is not needed at this depth")
        g2 = _row(norm2_g[i])
        is_last = i == depth - 1
        x = _mlp(x, ml, g2, w1m, w2m, i, _row(final_g) if is_last else None, delta=delta, attn=attn, tm=1024, tf=1024)
        if ctx_later:
            h_ctx = _mlp(h_ctx, mc, g2, w1m, w2m, i, attn=attn_ctx, tm=n_ctx)
    return x
```

```python
import functools

import jax
import jax.numpy as jnp
import numpy as np
from jax import lax
from jax.experimental import pallas as pl
from jax.experimental.pallas import tpu as pltpu

GRID_W = 64
N_MIXERS = 4
HEAD_DIM = 64
AT_HEADS = 16
AT_KV_HEADS = 4
AT_WIDTH = AT_HEADS * HEAD_DIM
AT_KV_WIDTH = AT_KV_HEADS * HEAD_DIM
ROPE_THETA = 10000.0
NA_HEADS = 16
NA_WIDTH = NA_HEADS * HEAD_DIM
NA_WIN_ROWS = 8
NA_WIN_COLS = 16
CONV_WIDTH = 31
FT_GROUPS = 4
N_MOD = 6
EPS = 1e-6
QK_SCALE = HEAD_DIM ** -0.5
LOG2E = float(np.log2(np.e))

LANES = 128
NA_QROWS = 4
NA_KROWS = 12
NA_CHUNK_ROWS = 4
NA_GROUPS_PER_STEP = 4
NA_LOOKAHEAD = 6
NA_EROWS = NA_KROWS + 2 * NA_QROWS
CONV_HALO = 16
CONV_ROWS = 64
CONV_LANES = 256
SUBLANES = 8
FT_NA = 64
NEG_BIG = -1e30

BF = jnp.bfloat16
F32 = jnp.float32


def _cparams(sem, vmem_mib):
    return pltpu.CompilerParams(dimension_semantics=sem, vmem_limit_bytes=vmem_mib << 20)


def _norm_mod(x, g, shift, scale):
    ms = jnp.mean(x * x, axis=-1, keepdims=True)
    return (x * lax.rsqrt(ms + EPS) * g) * (1.0 + scale) + shift


def _ada_kernel(v_ref, w_ref, b_ref, o_ref):
    v = v_ref[...]
    sv = v * jax.nn.sigmoid(v)
    o_ref[0] = jnp.dot(sv, w_ref[0], preferred_element_type=F32, precision=lax.Precision.HIGHEST) + b_ref[0]


def _ada_mods(vec8, ada_w, ada_b):
    depth, d, n = ada_w.shape
    tn = n // 2
    return pl.pallas_call(
        _ada_kernel,
        grid=(depth, n // tn),
        in_specs=[
            pl.BlockSpec((8, d), lambda l, j: (0, 0)),
            pl.BlockSpec((1, d, tn), lambda l, j: (l, 0, j)),
            pl.BlockSpec((1, 1, tn), lambda l, j: (l, 0, j)),
        ],
        out_specs=pl.BlockSpec((1, 8, tn), lambda l, j: (l, 0, j)),
        out_shape=jax.ShapeDtypeStruct((depth, 8, n), F32),
        compiler_params=_cparams(("arbitrary", "arbitrary"), 40),
        name="ada_mods",
    )(vec8, ada_w, ada_b.reshape(depth, 1, n))


def _mod_map(n_mod_rows):
    if n_mod_rows == 1:
        return lambda b, *_: (0, 0, 0)
    return lambda b, *_: (b, 0, 0)


def _gqa_proj_kernel(*refs, rope, tm):
    if rope:
        x_ref, mod_ref, g1_ref, w_ref, hm_ref, qg_ref, kg_ref, cos_ref, sin_ref, qt_ref, k_ref, vt_ref = refs
    else:
        x_ref, mod_ref, g1_ref, w_ref, hm_ref, qg_ref, kg_ref, qt_ref, k_ref, vt_ref = refs
    h = _norm_mod(x_ref[0], g1_ref[...], mod_ref[0, 0:1, :], mod_ref[0, 1:2, :]).astype(BF)
    qkv = jnp.dot(h, w_ref[...], preferred_element_type=F32)
    q = qkv[:, :AT_WIDTH]
    k = qkv[:, AT_WIDTH:AT_WIDTH + AT_KV_WIDTH]
    v = qkv[:, AT_WIDTH + AT_KV_WIDTH:]
    hm = hm_ref[...]
    q = q * lax.rsqrt(jnp.dot((q * q).astype(BF), hm, preferred_element_type=F32) + EPS) * qg_ref[...]
    hk = hm[:AT_KV_WIDTH, :AT_KV_WIDTH]
    k = k * lax.rsqrt(jnp.dot((k * k).astype(BF), hk, preferred_element_type=F32) + EPS) * kg_ref[...]
    if rope:
        cos = cos_ref[...]
        sin = sin_ref[...]
        even = (lax.broadcasted_iota(jnp.int32, (tm, LANES), 1) % 2) == 0

        def rot(t):
            partner = jnp.where(even, pltpu.roll(t, LANES - 1, 1), pltpu.roll(t, 1, 1))
            return t * cos + partner * sin
    else:
        def rot(t):
            return t
    for c in range(AT_WIDTH // LANES):
        qc = rot(q[:, c * LANES:(c + 1) * LANES]) * (QK_SCALE * LOG2E)
        qt_ref[0, c * LANES:(c + 1) * LANES, :] = qc.T.astype(BF)
    for c in range(AT_KV_WIDTH // LANES):
        kc = rot(k[:, c * LANES:(c + 1) * LANES]).astype(BF)
        k_ref[0, 2 * c] = kc[:, :HEAD_DIM]
        k_ref[0, 2 * c + 1] = kc[:, HEAD_DIM:]
    vt = v.T.astype(BF)
    for h in range(AT_KV_HEADS):
        vt_ref[0, h] = vt[h * HEAD_DIM:(h + 1) * HEAD_DIM]


def _gqa_proj(x, mod, g1, w, hm, qg, kg, cos, sin, *, tm=256):
    bx, t, d = x.shape
    rope = cos is not None
    n = w.shape[1]
    const = lambda b, i: (0, 0)
    in_specs = [
        pl.BlockSpec((1, tm, d), lambda b, i: (b, i, 0)),
        pl.BlockSpec((1, N_MOD, d), _mod_map(mod.shape[0])),
        pl.BlockSpec((1, d), const),
        pl.BlockSpec((d, n), const),
        pl.BlockSpec(hm.shape, const),
        pl.BlockSpec((1, AT_WIDTH), const),
        pl.BlockSpec((1, AT_KV_WIDTH), const),
    ]
    args = [x, mod, g1, w, hm, qg, kg]
    if rope:
        in_specs += [pl.BlockSpec((tm, LANES), lambda b, i: (i, 0))] * 2
        args += [cos, sin]
    return pl.pallas_call(
        functools.partial(_gqa_proj_kernel, rope=rope, tm=tm),
        grid=(bx, t // tm),
        in_specs=in_specs,
        out_specs=[
            pl.BlockSpec((1, AT_WIDTH, tm), lambda b, i: (b, 0, i)),
            pl.BlockSpec((1, AT_KV_HEADS, tm, HEAD_DIM), lambda b, i: (b, 0, i, 0)),
            pl.BlockSpec((1, AT_KV_HEADS, HEAD_DIM, tm), lambda b, i: (b, 0, 0, i)),
        ],
        out_shape=[
            jax.ShapeDtypeStruct((bx, AT_WIDTH, t), BF),
            jax.ShapeDtypeStruct((bx, AT_KV_HEADS, t, HEAD_DIM), BF),
            jax.ShapeDtypeStruct((bx, AT_KV_HEADS, HEAD_DIM, t), BF),
        ],
        compiler_params=_cparams(("parallel", "parallel"), 48),
        name="gqa_proj_rope" if rope else "gqa_proj_ctx",
    )(*args)


def _flash_kernel(*refs, tk, lookahead, with_latents):
    if with_latents:
        qt_ref, kc_ref, vtc_ref, kl_ref, vtl_ref, ot_ref, m_sc, l_sc, acc_sc = refs
        n_lat = kl_ref.shape[2] // tk
    else:
        qt_ref, kc_ref, vtc_ref, ot_ref, m_sc, l_sc, acc_sc = refs
        n_lat = 0
    n_heads = qt_ref.shape[1] // HEAD_DIM
    qs = [qt_ref[0, hh * HEAD_DIM:(hh + 1) * HEAD_DIM, :] for hh in range(n_heads)]
    chunks = [(kc_ref.at[0, 0], vtc_ref.at[0, 0])]
    chunks += [(kl_ref.at[0, 0, c * tk:(c + 1) * tk, :], vtl_ref.at[0, 0, :, c * tk:(c + 1) * tk]) for c in range(n_lat)]

    def scores(hh, c):
        return jnp.dot(chunks[c][0][...], qs[hh], preferred_element_type=F32)

    def weighted(c, s, m):
        p = jnp.exp2(s - m)
        psum = p.reshape(p.shape[0] // SUBLANES, SUBLANES, p.shape[1]).sum(axis=0)
        return jnp.dot(chunks[c][1][...], p.astype(BF), preferred_element_type=F32), psum

    units = [(hh, c) for hh in range(n_heads) for c in range(len(chunks))]
    pending = [scores(*u) for u in units[:lookahead]]
    m0, acc, den8 = {}, {}, {}
    for n, (hh, c) in enumerate(units):
        if n + lookahead < len(units):
            pending.append(scores(*units[n + lookahead]))
        s = pending.pop(0)
        if c == 0:
            m0[hh] = jnp.max(s, axis=0, keepdims=True)
        pv, psum = weighted(c, s, m0[hh])
        acc[hh] = pv if c == 0 else acc[hh] + pv
        den8[hh] = psum if c == 0 else den8[hh] + psum
    denom = [jnp.sum(den8[hh], axis=0, keepdims=True) for hh in range(n_heads)]
    bad = jnp.zeros_like(denom[0])
    for dn in denom:
        bad = jnp.where(jnp.isfinite(dn), bad, 1.0)
    overflowed = jnp.max(bad) > 0.0

    @pl.when(jnp.logical_not(overflowed))
    def _():
        for hh in range(n_heads):
            ot_ref[0, hh * HEAD_DIM:(hh + 1) * HEAD_DIM, :] = (acc[hh] * (1.0 / denom[hh])).astype(BF)

    @pl.when(overflowed)
    def _():
        for hh in range(n_heads):
            m_sc[...] = jnp.full(m_sc.shape, NEG_BIG, F32)
            l_sc[...] = jnp.zeros(l_sc.shape, F32)
            acc_sc[...] = jnp.zeros(acc_sc.shape, F32)

            def update(k, vt, q=qs[hh]):
                s = jnp.dot(k, q, preferred_element_type=F32)
                m_prev = m_sc[...]
                m_new = jnp.maximum(m_prev, jnp.max(s, axis=0, keepdims=True))
                alpha = jnp.exp2(m_prev - m_new)
                p = jnp.exp2(s - m_new)
                l_sc[...] = alpha * l_sc[...] + jnp.sum(p, axis=0, keepdims=True)
                acc_sc[...] = alpha * acc_sc[...] + jnp.dot(vt, p.astype(BF), preferred_element_type=F32)
                m_sc[...] = m_new

            update(kc_ref[0, 0], vtc_ref[0, 0])

            def body(c, carry, update=update):
                off = pl.multiple_of(c * tk, tk)
                update(kl_ref[0, 0, pl.ds(off, tk), :], vtl_ref[0, 0, :, pl.ds(off, tk)])
                return carry

            if with_latents:
                lax.fori_loop(0, n_lat, body, 0)
            ot_ref[0, hh * HEAD_DIM:(hh + 1) * HEAD_DIM, :] = (acc_sc[...] * (1.0 / l_sc[...])).astype(BF)


def _flash(qt, kc, vtc, kl=None, vtl=None, *, tq, tk=256, lookahead=3, heads_per_step=4):
    b, w, t = qt.shape
    kv = kc.shape[1]
    heads = w // HEAD_DIM
    group = heads // kv
    hps = heads_per_step
    assert group % hps == 0
    kv_map = lambda bi, h, i: (bi, h * hps // group, 0, 0)
    in_specs = [
        pl.BlockSpec((1, hps * HEAD_DIM, tq), lambda bi, h, i: (bi, h, i)),
        pl.BlockSpec((1, 1) + kc.shape[2:], kv_map),
        pl.BlockSpec((1, 1) + vtc.shape[2:], kv_map),
    ]
    args = [qt, kc, vtc]
    if kl is not None:
        in_specs += [pl.BlockSpec((1, 1) + kl.shape[2:], kv_map), pl.BlockSpec((1, 1) + vtl.shape[2:], kv_map)]
        args += [kl, vtl]
    return pl.pallas_call(
        functools.partial(_flash_kernel, tk=tk, lookahead=lookahead, with_latents=kl is not None),
        grid=(b, heads // hps, t // tq),
        in_specs=in_specs,
        out_specs=pl.BlockSpec((1, hps * HEAD_DIM, tq), lambda bi, h, i: (bi, h, i)),
        out_shape=jax.ShapeDtypeStruct((b, w, t), BF),
        scratch_shapes=[pltpu.VMEM((1, tq), F32), pltpu.VMEM((1, tq), F32), pltpu.VMEM((HEAD_DIM, tq), F32)],
        compiler_params=_cparams(("parallel", "parallel", "parallel"), 48),
        name="gqa_flash" if kl is not None else "gqa_flash_ctx",
    )(*args)


def _mlp_kernel(*refs, final, pending):
    refs = list(refs)
    x_ref = refs.pop(0)
    d_ref = refs.pop(0) if pending == "delta" else None
    ot_ref, wo_ref = (refs.pop(0), refs.pop(0)) if pending == "attn" else (None, None)
    mod_ref, g2_ref, w1_ref, w2_ref = refs[:4]
    fg_ref = refs[4] if final else None
    o_ref, xs_sc, h_sc, acc_sc = refs[-4:]
    j = pl.program_id(2)

    @pl.when(j == 0)
    def _():
        xs = x_ref[0]
        if pending == "delta":
            xs = xs + d_ref[0]
        elif pending == "attn":
            y = lax.dot_general(ot_ref[0], wo_ref[...], (((0,), (0,)), ((), ())), preferred_element_type=F32)
            xs = xs + mod_ref[0, 2:3, :] * y
        xs_sc[...] = xs
        h_sc[...] = _norm_mod(xs, g2_ref[...], mod_ref[0, 3:4, :], mod_ref[0, 4:5, :]).astype(BF)
        acc_sc[...] = jnp.zeros(acc_sc.shape, F32)

    a = jnp.dot(h_sc[...], w1_ref[...], preferred_element_type=F32)
    a = jnp.square(jnp.maximum(a, 0.0)).astype(BF)
    acc_sc[...] += jnp.dot(a, w2_ref[...], preferred_element_type=F32)

    @pl.when(j == pl.num_programs(2) - 1)
    def _():
        y = xs_sc[...] + mod_ref[0, 5:6, :] * acc_sc[...]
        if final:
            ms = jnp.mean(y * y, axis=-1, keepdims=True)
            y = y * lax.rsqrt(ms + EPS) * fg_ref[...]
        o_ref[0] = y


def _mlp(x, mod, g2, w1, w2, layer, final_g=None, *, delta=None, attn=None, tm, tf=512):
    b, t, d = x.shape
    f = w1.shape[2]
    final = final_g is not None
    row_spec = pl.BlockSpec((1, tm, d), lambda bi, i, j: (bi, i, 0))
    in_specs, args, pending = [row_spec], [x], None
    if delta is not None:
        in_specs, args, pending = in_specs + [row_spec], args + [delta], "delta"
    if attn is not None:
        ot, wo = attn
        in_specs += [pl.BlockSpec((1, ot.shape[1], tm), lambda bi, i, j: (bi, 0, i)),
                     pl.BlockSpec(wo.shape, lambda bi, i, j: (0, 0))]
        args, pending = args + [ot, wo], "attn"
    in_specs += [
        pl.BlockSpec((1, N_MOD, d), _mod_map(mod.shape[0])),
        pl.BlockSpec((1, d), lambda bi, i, j: (0, 0)),
        pl.BlockSpec((None, d, tf), lambda bi, i, j: (layer, 0, j)),
        pl.BlockSpec((None, tf, d), lambda bi, i, j: (layer, j, 0)),
    ]
    args += [mod, g2, w1, w2]
    if final:
        in_specs.append(pl.BlockSpec((1, d), lambda bi, i, j: (0, 0)))
        args.append(final_g)
    return pl.pallas_call(
        functools.partial(_mlp_kernel, final=final, pending=pending),
        grid=(b, t // tm, f // tf),
        in_specs=in_specs,
        out_specs=pl.BlockSpec((1, tm, d), lambda bi, i, j: (bi, i, 0)),
        out_shape=jax.ShapeDtypeStruct((b, t, d), F32),
        scratch_shapes=[pltpu.VMEM((tm, d), F32), pltpu.VMEM((tm, d), BF), pltpu.VMEM((tm, d), F32)],
        compiler_params=_cparams(("parallel", "parallel", "arbitrary"), 56),
        name="mlp_final" if final else "mlp",
    )(*args)


def _na_proj_kernel(x_ref, mod_ref, g1_ref, w_ref, qt_ref, k_ref, v_ref):
    h = _norm_mod(x_ref[0], g1_ref[...], mod_ref[0, 0:1, :], mod_ref[0, 1:2, :]).astype(BF)
    qkv = jnp.dot(h, w_ref[...], preferred_element_type=F32)
    for c in range(NA_WIDTH // LANES):
        qt_ref[0, c * LANES:(c + 1) * LANES, :] = (qkv[:, c * LANES:(c + 1) * LANES] * (QK_SCALE * LOG2E)).T.astype(BF)
    k_ref[0] = qkv[:, NA_WIDTH:2 * NA_WIDTH].astype(BF)
    v_ref[0] = qkv[:, 2 * NA_WIDTH:].astype(BF)


def _na_proj(x, mod, g1, w, *, tm=256):
    bx, t, d = x.shape
    return pl.pallas_call(
        _na_proj_kernel,
        grid=(bx, t // tm),
        in_specs=[
            pl.BlockSpec((1, tm, d), lambda b, i: (b, i, 0)),
            pl.BlockSpec((1, N_MOD, d), _mod_map(mod.shape[0])),
            pl.BlockSpec((1, d), lambda b, i: (0, 0)),
            pl.BlockSpec(w.shape, lambda b, i: (0, 0)),
        ],
        out_specs=[
            pl.BlockSpec((1, NA_WIDTH, tm), lambda b, i: (b, 0, i)),
            pl.BlockSpec((1, tm, NA_WIDTH), lambda b, i: (b, i, 0)),
            pl.BlockSpec((1, tm, NA_WIDTH), lambda b, i: (b, i, 0)),
        ],
        out_shape=[
            jax.ShapeDtypeStruct((bx, NA_WIDTH, t), BF),
            jax.ShapeDtypeStruct((bx, t, NA_WIDTH), BF),
            jax.ShapeDtypeStruct((bx, t, NA_WIDTH), BF),
        ],
        compiler_params=_cparams(("parallel", "parallel"), 48),
        name="na_proj",
    )(x, mod, g1, w)


def _na_key_base(g, n_groups):
    return jnp.clip(NA_QROWS * g - NA_WIN_ROWS // 2, 0, NA_QROWS * n_groups - NA_KROWS)


def _na_kernel(*refs, n_groups, per_step):
    qt_ref, k_ref, v_ref, kc_ref, vc_ref, quad_ref = refs[:6]
    mask_refs = refs[6:6 + per_step]
    ot_ref = refs[6 + per_step]
    tq = NA_QROWS * GRID_W
    ck = NA_CHUNK_ROWS * GRID_W
    n_nb = NA_KROWS // NA_CHUNK_ROWS
    tn = (((0,), (0,)), ((), ()))
    upper = lax.broadcasted_iota(jnp.int32, (2 * HEAD_DIM, tq), 0) < HEAD_DIM
    geo = []
    for gi in range(per_step):
        g = pl.program_id(2) * per_step + gi
        kb_row = _na_key_base(g, n_groups)
        q2 = qt_ref[0, :, gi * tq:(gi + 1) * tq]
        qh = (jnp.where(upper, q2, jnp.zeros_like(q2)), jnp.where(upper, jnp.zeros_like(q2), q2))
        geo.append((pl.multiple_of(kb_row * GRID_W, GRID_W), kb_row - NA_QROWS * g + NA_WIN_ROWS, qh))

    def scores(gi, half, j):
        kb, e0, qh = geo[gi]
        if j < 0:
            return jnp.dot(kc_ref[0], qh[half], preferred_element_type=F32)
        bias = jnp.concatenate([quad_ref[half, e0 + NA_CHUNK_ROWS * j + i] for i in range(NA_CHUNK_ROWS)], axis=0)
        bias = bias + mask_refs[gi][0, j * ck:(j + 1) * ck, :]
        return jnp.dot(k_ref[0, pl.ds(kb + j * ck, ck), :], qh[half], preferred_element_type=F32) + bias

    def values(gi, j):
        return vc_ref[0] if j < 0 else v_ref[0, pl.ds(geo[gi][0] + j * ck, ck), :]

    def finish(gi, o0, l0, o1, l1):
        ot_ref[0, :, gi * tq:(gi + 1) * tq] = jnp.where(upper, o0 * (1.0 / l0), o1 * (1.0 / l1)).astype(BF)

    heads = [(gi, half) for gi in range(per_step) for half in range(2)]
    units = [(gi, half, j) for j in range(-1, n_nb) for gi, half in heads]
    lookahead = max(NA_LOOKAHEAD, len(heads))
    pending = [scores(*u) for u in units[:lookahead]]
    m0 = {u[:2]: jnp.max(s, axis=0, keepdims=True) for u, s in zip(units[:len(heads)], pending)}
    acc = {}
    den = {}
    for n, (gi, half, j) in enumerate(units):
        if n + lookahead < len(units):
            pending.append(scores(*units[n + lookahead]))
        p = jnp.exp2(pending.pop(0) - m0[gi, half])
        pv = lax.dot_general(values(gi, j), p.astype(BF), tn, preferred_element_type=F32)
        ps = jnp.sum(p, axis=0, keepdims=True)
        acc[gi, half] = pv if j < 0 else acc[gi, half] + pv
        den[gi, half] = ps if j < 0 else den[gi, half] + ps
    bad = jnp.zeros((1, tq), F32)
    for key in heads:
        bad = jnp.where(jnp.isfinite(den[key]), bad, 1.0)
    overflowed = jnp.max(bad) > 0.0

    @pl.when(jnp.logical_not(overflowed))
    def _():
        for gi in range(per_step):
            finish(gi, acc[gi, 0], den[gi, 0], acc[gi, 1], den[gi, 1])

    @pl.when(overflowed)
    def _():
        for gi in range(per_step):
            outs = []
            for half in range(2):
                s_all = [scores(gi, half, j) for j in range(-1, n_nb)]
                m = s_all[0].max(axis=0, keepdims=True)
                for s in s_all[1:]:
                    m = jnp.maximum(m, jnp.max(s, axis=0, keepdims=True))
                o = None
                l = None
                for j, s in zip(range(-1, n_nb), s_all):
                    p = jnp.exp2(s - m)
                    pv = lax.dot_general(values(gi, j), p.astype(BF), tn, preferred_element_type=F32)
                    ps = jnp.sum(p, axis=0, keepdims=True)
                    o = pv if o is None else o + pv
                    l = ps if l is None else l + ps
                outs += [o, l]
            finish(gi, *outs)


def _na_quad_kernel(rpb_ref, quad_ref):
    h = pl.program_id(0)
    n_dr, n_dc = rpb_ref.shape[1:]
    kc = lax.broadcasted_iota(jnp.int32, (GRID_W, GRID_W), 0)
    qc = lax.broadcasted_iota(jnp.int32, (GRID_W, GRID_W), 1)
    dc = kc - qc + NA_WIN_COLS - 1
    cs = jnp.clip(qc - NA_WIN_COLS // 2, 0, GRID_W - NA_WIN_COLS)
    col_ok = (kc >= cs) & (kc < cs + NA_WIN_COLS)
    toeplitz = []
    for dr in range(n_dr):
        t = jnp.zeros((GRID_W, GRID_W), F32)
        for c in range(n_dc):
            t = jnp.where(dc == c, rpb_ref[h, dr, c], t)
        toeplitz.append(jnp.where(col_ok, t * LOG2E, NEG_BIG))
    for e in range(NA_EROWS):
        for a in range(NA_QROWS):
            dr = e - a - 1
            blk = toeplitz[dr] if 0 <= dr < n_dr else jnp.zeros((GRID_W, GRID_W), F32)
            quad_ref[0, e, :, a * GRID_W:(a + 1) * GRID_W] = blk


def _na_bias_tables(rpb, n_rows):
    n_groups = n_rows // NA_QROWS
    n_h = rpb.shape[0]
    quad = pl.pallas_call(
        _na_quad_kernel,
        grid=(n_h,),
        in_specs=[pl.BlockSpec(memory_space=pltpu.SMEM)],
        out_specs=pl.BlockSpec((1, NA_EROWS, GRID_W, NA_QROWS * GRID_W), lambda h: (h, 0, 0, 0)),
        out_shape=jax.ShapeDtypeStruct((n_h, NA_EROWS, GRID_W, NA_QROWS * GRID_W), F32),
        compiler_params=_cparams(("parallel",), 32),
        name="na_bias_quad",
    )(rpb)
    masks = []
    for g in (0, 1, n_groups - 1):
        r = NA_QROWS * g + np.arange(NA_QROWS)[None, None, :, None]
        kb = int(np.clip(NA_QROWS * g - NA_WIN_ROWS // 2, 0, n_rows - NA_KROWS))
        krow = kb + np.arange(NA_KROWS)[:, None, None, None]
        rs = np.clip(r - NA_WIN_ROWS // 2, 0, n_rows - NA_WIN_ROWS)
        ok = np.broadcast_to((krow >= rs) & (krow < rs + NA_WIN_ROWS), (NA_KROWS, GRID_W, NA_QROWS, GRID_W))
        masks.append(np.where(ok, 0.0, NEG_BIG).reshape(NA_KROWS * GRID_W, NA_QROWS * GRID_W))
    return quad, jnp.asarray(np.stack(masks), F32)


def _na_attend(qt, k, v, kc, vc, quad, rowmask, n_rows, *, per_step=NA_GROUPS_PER_STEP):
    b, w, s = qt.shape
    c = kc.shape[1]
    n_groups = n_rows // NA_QROWS
    tq = NA_QROWS * GRID_W
    pair = 2 * HEAD_DIM

    def mask_spec(gi):
        def index_map(bi, hp, st):
            g = st * per_step + gi
            return (jnp.where(g == 0, 0, jnp.where(g == n_groups - 1, 2, 1)), 0, 0)
        return pl.BlockSpec((1, NA_KROWS * GRID_W, tq), index_map)

    return pl.pallas_call(
        functools.partial(_na_kernel, n_groups=n_groups, per_step=per_step),
        grid=(b, w // pair, n_groups // per_step),
        in_specs=[
            pl.BlockSpec((1, pair, per_step * tq), lambda bi, hp, st: (bi, hp, st)),
            pl.BlockSpec((1, s, pair), lambda bi, hp, st: (bi, 0, hp)),
            pl.BlockSpec((1, s, pair), lambda bi, hp, st: (bi, 0, hp)),
            pl.BlockSpec((1, c, pair), lambda bi, hp, st: (bi, 0, hp)),
            pl.BlockSpec((1, c, pair), lambda bi, hp, st: (bi, 0, hp)),
            pl.BlockSpec((2, NA_EROWS, GRID_W, tq), lambda bi, hp, st: (hp, 0, 0, 0)),
        ] + [mask_spec(gi) for gi in range(per_step)],
        out_specs=pl.BlockSpec((1, pair, per_step * tq), lambda bi, hp, st: (bi, hp, st)),
        out_shape=jax.ShapeDtypeStruct((b, w, s), BF),
        compiler_params=_cparams(("parallel", "parallel", "arbitrary"), 48),
        name="na_attend",
    )(qt, k, v, kc, vc, quad, *([rowmask] * per_step))


def _conv_pw1_kernel(x_ref, mod_ref, g1_ref, w_ref, b_ref, u_ref):
    h = _norm_mod(x_ref[0], g1_ref[...], mod_ref[0, 0:1, :], mod_ref[0, 1:2, :]).astype(BF)
    ag = jnp.dot(h, w_ref[...], preferred_element_type=F32) + b_ref[...]
    d = u_ref.shape[2]
    u_ref[0] = ag[:, :d] * jax.nn.sigmoid(ag[:, d:])


def _conv_pw1(x, mod, g1, w, b, *, tm=512):
    bx, t, d = x.shape
    return pl.pallas_call(
        _conv_pw1_kernel,
        grid=(bx, t // tm),
        in_specs=[
            pl.BlockSpec((1, tm, d), lambda bi, i: (bi, i, 0)),
            pl.BlockSpec((1, N_MOD, d), _mod_map(mod.shape[0])),
            pl.BlockSpec((1, d), lambda bi, i: (0, 0)),
            pl.BlockSpec(w.shape, lambda bi, i: (0, 0)),
            pl.BlockSpec((1, 2 * d), lambda bi, i: (0, 0)),
        ],
        out_specs=pl.BlockSpec((1, tm, d), lambda bi, i: (bi, i, 0)),
        out_shape=jax.ShapeDtypeStruct((bx, t, d), F32),
        compiler_params=_cparams(("parallel", "parallel"), 48),
        name="conv_pw1_glu",
    )(x, mod, g1, w, b)


def _conv_tail_kernel(u_ref, up_ref, un_ref, wdw_ref, bdw_ref, lg_ref, lb_ref, w2_ref, b2_ref, x_ref, mod_ref,
                      o_ref, buf_sc, cv_sc, *, tm):
    i = pl.program_id(1)
    last = pl.num_programs(1) - 1
    buf_sc[0:CONV_HALO, :] = jnp.where(i > 0, up_ref[0], 0.0)
    buf_sc[CONV_HALO:CONV_HALO + tm, :] = u_ref[0]
    buf_sc[CONV_HALO + tm:, :] = jnp.where(i < last, un_ref[0], 0.0)
    assert CONV_HALO - CONV_WIDTH // 2 == 1
    d = buf_sc.shape[1]
    for lb in range(d // CONV_LANES):
        ls = slice(lb * CONV_LANES, (lb + 1) * CONV_LANES)
        for rc in range(tm // CONV_ROWS):
            r0 = rc * CONV_ROWS
            acc = None
            for r in range(SUBLANES):
                part = None
                for j in range(r, CONV_WIDTH + 1, SUBLANES):
                    if j == 0:
                        continue
                    rows = slice(r0 + j - r, r0 + j - r + CONV_ROWS + SUBLANES)
                    term = buf_sc[rows, ls] * wdw_ref[j - 1:j, ls]
                    part = term if part is None else part + term
                part = part[r:r + CONV_ROWS]
                acc = part if acc is None else acc + part
            cv_sc[r0:r0 + CONV_ROWS, ls] = acc
    u = cv_sc[...] + bdw_ref[...]
    mu = jnp.mean(u, axis=-1, keepdims=True)
    uc = u - mu
    var = jnp.mean(uc * uc, axis=-1, keepdims=True)
    y = uc * lax.rsqrt(var + EPS) * lg_ref[...] + lb_ref[...]
    y = (y * jax.nn.sigmoid(y)).astype(BF)
    z = jnp.dot(y, w2_ref[...], preferred_element_type=F32) + b2_ref[...]
    o_ref[0] = x_ref[0] + mod_ref[0, 2:3, :] * z


def _conv_tail(u, wdw, bdw, lg, lb, w2, b2, x, mod, *, tm=256):
    bx, t, d = x.shape
    hb = tm // CONV_HALO
    n_halo = t // CONV_HALO
    vec = lambda: pl.BlockSpec((1, d), lambda bi, i: (0, 0))
    return pl.pallas_call(
        functools.partial(_conv_tail_kernel, tm=tm),
        grid=(bx, t // tm),
        in_specs=[
            pl.BlockSpec((1, tm, d), lambda bi, i: (bi, i, 0)),
            pl.BlockSpec((1, CONV_HALO, d), lambda bi, i: (bi, jnp.maximum(i * hb - 1, 0), 0)),
            pl.BlockSpec((1, CONV_HALO, d), lambda bi, i: (bi, jnp.minimum((i + 1) * hb, n_halo - 1), 0)),
            pl.BlockSpec(wdw.shape, lambda bi, i: (0, 0)),
            vec(), vec(), vec(),
            pl.BlockSpec(w2.shape, lambda bi, i: (0, 0)),
            vec(),
            pl.BlockSpec((1, tm, d), lambda bi, i: (bi, i, 0)),
            pl.BlockSpec((1, N_MOD, d), _mod_map(mod.shape[0])),
        ],
        out_specs=pl.BlockSpec((1, tm, d), lambda bi, i: (bi, i, 0)),
        out_shape=jax.ShapeDtypeStruct((bx, t, d), F32),
        scratch_shapes=[pltpu.VMEM((tm + 2 * CONV_HALO, d), F32), pltpu.VMEM((tm, d), F32)],
        compiler_params=_cparams(("parallel", "parallel"), 48),
        name="conv_tail",
    )(u, u, u, wdw, bdw, lg, lb, w2, b2, x, mod)


def _ft_tables(n_seq, gw):
    nb = n_seq // FT_NA
    ka = np.arange(FT_NA)[None, :, None]
    na = np.arange(FT_NA)[None, None, :]
    jb = np.arange(nb)[:, None, None]
    ph = 2 * np.pi * ((ka * (nb * na + jb)) % n_seq) / n_seq
    t1 = np.concatenate([np.cos(ph), -np.sin(ph)], axis=1)
    kb = np.arange(nb)
    ph = 2 * np.pi * ((kb[:, None] * kb[None, :]) % nb) / nb
    c3, s3 = np.cos(ph), np.sin(ph)
    t2 = np.block([[c3, s3], [-s3, c3]]) / np.sqrt(n_seq)
    m = np.arange(gw)
    ph = 2 * np.pi * ((m[:, None] * m[None, :]) % gw) / gw
    tw = np.stack([np.cos(ph), np.sin(ph)]) / np.sqrt(gw)
    return tuple(jnp.asarray(t, F32).astype(BF) for t in (t1, t2, tw))


def _ft_prep_kernel(x_ref, mod_ref, g1_ref, h_ref, *, nb):
    h = _norm_mod(x_ref[0], g1_ref[...], mod_ref[0, 0:1, :], mod_ref[0, 1:2, :])
    for a in range(h.shape[0] // nb):
        h_ref[0, :, a, :] = h[a * nb:(a + 1) * nb, :]


def _ft_prep(x, mod, g1, *, tm=1024):
    bx, t, d = x.shape
    nb = t // FT_NA
    return pl.pallas_call(
        functools.partial(_ft_prep_kernel, nb=nb),
        grid=(bx, t // tm),
        in_specs=[
            pl.BlockSpec((1, tm, d), lambda bi, i: (bi, i, 0)),
            pl.BlockSpec((1, N_MOD, d), _mod_map(mod.shape[0])),
            pl.BlockSpec((1, d), lambda bi, i: (0, 0)),
        ],
        out_specs=pl.BlockSpec((1, nb, tm // nb, d), lambda bi, i: (bi, 0, i, 0)),
        out_shape=jax.ShapeDtypeStruct((bx, nb, FT_NA, d), F32),
        compiler_params=_cparams(("parallel", "parallel"), 48),
        name="ft_prep",
    )(x, mod, g1)


def _ft_seq1_kernel(h_ref, t1_ref, y_ref, *, tb, d):
    for j in range(tb):
        y = jnp.dot(t1_ref[j], h_ref[0, j].astype(BF), preferred_element_type=F32)
        y_ref[0, :, j, :d] = y[:FT_NA]
        y_ref[0, :, j, d:] = y[FT_NA:]


def _ft_seq1(hp, t1, *, tb=8):
    bx, nb, _, d = hp.shape
    return pl.pallas_call(
        functools.partial(_ft_seq1_kernel, tb=tb, d=d),
        grid=(bx, nb // tb),
        in_specs=[
            pl.BlockSpec((1, tb, FT_NA, d), lambda bi, i: (bi, i, 0, 0)),
            pl.BlockSpec((tb, 2 * FT_NA, FT_NA), lambda bi, i: (i, 0, 0)),
        ],
        out_specs=pl.BlockSpec((1, FT_NA, tb, 2 * d), lambda bi, i: (bi, 0, i, 0)),
        out_shape=jax.ShapeDtypeStruct((bx, FT_NA, nb, 2 * d), F32),
        compiler_params=_cparams(("parallel", "parallel"), 48),
        name="ft_seq1",
    )(hp, t1)


def _ft_seq2_kernel(y_ref, t2_ref, tw_ref, w_ref, b_ref, mod_ref, o_ref, *, ta):
    d = w_ref.shape[0]
    nb = y_ref.shape[2]
    gw = tw_ref.shape[1]
    for a in range(ta):
        yb = y_ref[0, a]
        rhs = jnp.concatenate([yb[:, :d], yb[:, d:]], axis=0).astype(BF)
        aa = jnp.dot(t2_ref[...], rhs, preferred_element_type=F32).astype(BF)
        z = [jnp.dot(aa[:nb, g * gw:(g + 1) * gw], tw_ref[0], preferred_element_type=F32)
             + jnp.dot(aa[nb:, g * gw:(g + 1) * gw], tw_ref[1], preferred_element_type=F32)
             for g in range(d // gw)]
        z = jnp.concatenate(z, axis=1).astype(BF)
        yl = jnp.dot(z, w_ref[...], preferred_element_type=F32) + b_ref[...]
        o_ref[0, :, a, :] = mod_ref[0, 2:3, :] * yl


def _ft_seq2(y, t2, tw, w, b, mod, *, ta=8):
    bx, _, nb, d2 = y.shape
    d = d2 // 2
    out = pl.pallas_call(
        functools.partial(_ft_seq2_kernel, ta=ta),
        grid=(bx, FT_NA // ta),
        in_specs=[
            pl.BlockSpec((1, ta, nb, d2), lambda bi, i: (bi, i, 0, 0)),
            pl.BlockSpec(t2.shape, lambda bi, i: (0, 0)),
            pl.BlockSpec(tw.shape, lambda bi, i: (0, 0, 0)),
            pl.BlockSpec(w.shape, lambda bi, i: (0, 0)),
            pl.BlockSpec((1, d), lambda bi, i: (0, 0)),
            pl.BlockSpec((1, N_MOD, d), _mod_map(mod.shape[0])),
        ],
        out_specs=pl.BlockSpec((1, nb, ta, d), lambda bi, i: (bi, 0, i, 0)),
        out_shape=jax.ShapeDtypeStruct((bx, nb, FT_NA, d), F32),
        compiler_params=_cparams(("parallel", "parallel"), 48),
        name="ft_seq2_mix",
    )(y, t2, tw, w, b, mod)
    return out.reshape(bx, nb * FT_NA, d)


def _rope_tables(n_tok):
    t = jnp.arange(n_tok)
    row = (t // GRID_W).astype(F32)
    col = (t % GRID_W).astype(F32)
    n_axis = HEAD_DIM // 4
    inv = ROPE_THETA ** (-jnp.arange(n_axis, dtype=F32) / n_axis)
    ang = jnp.concatenate([row[:, None] * inv, col[:, None] * inv], axis=-1)
    ang = jnp.tile(jnp.repeat(ang, 2, axis=-1), (1, LANES // HEAD_DIM))
    sign = jnp.where(jnp.arange(LANES) % 2 == 0, -1.0, 1.0).astype(F32)
    return jnp.cos(ang), jnp.sin(ang) * sign


def _row(v):
    return v.reshape(1, -1)


def kernel(x, c, ctx, c_ctx, ada_w, ada_b, norm1_g, norm2_g, mlp_w1, mlp_w2, final_g, at_w_qkv, at_q_g, at_k_g, at_w_o, na_w_qkv, na_rpb, na_w_o, cv_w_pw1, cv_b_pw1, cv_w_dw, cv_b_dw, cv_ln_g, cv_ln_b, cv_w_pw2, cv_b_pw2, ft_w, ft_b):
    bsz, n_lat, d = x.shape
    depth = ada_w.shape[0]
    n_rows = n_lat // GRID_W
    n_ctx = ctx.shape[1]

    vec8 = jnp.zeros((8, d), F32).at[:bsz].set(c).at[bsz].set(c_ctx)
    mods = _ada_mods(vec8, ada_w, ada_b)

    w1m = mlp_w1.astype(BF)
    w2m = mlp_w2.astype(BF)
    hm = jnp.asarray(np.kron(np.eye(AT_HEADS), np.full((HEAD_DIM, HEAD_DIM), 1.0 / HEAD_DIM)), BF)
    h_ctx = ctx
    for i in range(depth):
        kind = i % N_MIXERS
        occ = i // N_MIXERS
        ctx_later = any((j % N_MIXERS) in (0, 1) for j in range(i + 1, depth))
        ml = mods[i, :bsz].reshape(bsz, N_MOD, d)
        mc = mods[i, bsz:bsz + 1].reshape(1, N_MOD, d)
        g1 = _row(norm1_g[i])
        delta = attn = attn_ctx = None
        if kind == 0:
            w = at_w_qkv[occ].astype(BF)
            wo = at_w_o[occ].astype(BF)
            qg = _row(jnp.tile(at_q_g[occ], AT_HEADS))
            kg = _row(jnp.tile(at_k_g[occ], AT_KV_HEADS))
            cos, sin = _rope_tables(n_lat)
            qt_l, k_l, vt_l = _gqa_proj(x, ml, g1, w, hm, qg, kg, cos, sin, tm=512)
            qt_c, k_c, vt_c = _gqa_proj(h_ctx, mc, g1, w, hm, qg, kg, None, None, tm=n_ctx)
            attn = (_flash(qt_l, k_c, vt_c, k_l, vt_l, tq=512), wo)
            if ctx_later:
                attn_ctx = (_flash(qt_c, k_c, vt_c, tq=n_ctx), wo)
        elif kind == 1:
            w = na_w_qkv[occ].astype(BF)
            wo = na_w_o[occ].astype(BF)
            qt_l, k_l, v_l = _na_proj(x, ml, g1, w, tm=512)
            qt_c, k_c, v_c = _na_proj(h_ctx, mc, g1, w, tm=n_ctx)
            quad, rowmask = _na_bias_tables(na_rpb[occ], n_rows)
            attn = (_na_attend(qt_l, k_l, v_l, k_c, v_c, quad, rowmask, n_rows), wo)
            if ctx_later:
                raise NotImplementedError("context output of a neighbourhood layer is not needed at this depth")
        elif kind == 2:
            w1 = cv_w_pw1[occ].astype(BF)
            w2 = cv_w_pw2[occ].astype(BF)
            wdw = jnp.zeros((CONV_WIDTH + 1, d), F32).at[:CONV_WIDTH].set(cv_w_dw[occ])
            cv = (wdw, _row(cv_b_dw[occ]), _row(cv_ln_g[occ]), _row(cv_ln_b[occ]), w2, _row(cv_b_pw2[occ]))
            u = _conv_pw1(x, ml, g1, w1, _row(cv_b_pw1[occ]))
            x = _conv_tail(u, *cv, x, ml, tm=512)
            if ctx_later:
                raise NotImplementedError("context output of a convolution layer is not needed at this depth")
        else:
            t1, t2, tw = _ft_tables(n_lat, d // FT_GROUPS)
            y1 = _ft_seq1(_ft_prep(x, ml, g1), t1)
            delta = _ft_seq2(y1, t2, tw, ft_w[occ].astype(BF), _row(ft_b[occ]), ml)
            if ctx_later:
                raise NotImplementedError("context output of a Fourier layer is not needed at this depth")
        g2 = _row(norm2_g[i])
        is_last = i == depth - 1
        x = _mlp(x, ml, g2, w1m, w2m, i, _row(final_g) if is_last else None, delta=delta, attn=attn, tm=1024, tf=1024)
        if ctx_later:
            h_ctx = _mlp(h_ctx, mc, g2, w1m, w2m, i, attn=attn_ctx, tm=n_ctx)
    return x
```

```python
import functools

import jax
import jax.numpy as jnp
import numpy as np
from jax import lax
from jax.experimental import pallas as pl
from jax.experimental.pallas import tpu as pltpu

GRID_W = 64
N_MIXERS = 4
HEAD_DIM = 64
AT_HEADS = 16
AT_KV_HEADS = 4
AT_WIDTH = AT_HEADS * HEAD_DIM
AT_KV_WIDTH = AT_KV_HEADS * HEAD_DIM
ROPE_THETA = 10000.0
NA_HEADS = 16
NA_WIDTH = NA_HEADS * HEAD_DIM
NA_WIN_ROWS = 8
NA_WIN_COLS = 16
CONV_WIDTH = 31
FT_GROUPS = 4
N_MOD = 6
EPS = 1e-6
QK_SCALE = HEAD_DIM ** -0.5
LOG2E = float(np.log2(np.e))

LANES = 128
NA_QROWS = 4
NA_KROWS = 12
NA_CHUNK_ROWS = 4
NA_GROUPS_PER_STEP = 8
NA_LOOKAHEAD = 6
NA_EROWS = NA_KROWS + 2 * NA_QROWS
CONV_HALO = 16
CONV_ROWS = 64
CONV_LANES = 256
SUBLANES = 8
FT_NA = 64
NEG_BIG = -1e30

BF = jnp.bfloat16
F32 = jnp.float32


def _cparams(sem, vmem_mib):
    return pltpu.CompilerParams(dimension_semantics=sem, vmem_limit_bytes=vmem_mib << 20)


def _norm_mod(x, g, shift, scale):
    ms = jnp.mean(x * x, axis=-1, keepdims=True)
    return (x * lax.rsqrt(ms + EPS) * g) * (1.0 + scale) + shift


def _ada_kernel(v_ref, w_ref, b_ref, o_ref):
    v = v_ref[...]
    sv = v * jax.nn.sigmoid(v)
    o_ref[0] = jnp.dot(sv, w_ref[0], preferred_element_type=F32, precision=lax.Precision.HIGHEST) + b_ref[0]


def _ada_mods(vec8, ada_w, ada_b):
    depth, d, n = ada_w.shape
    tn = n // 2
    return pl.pallas_call(
        _ada_kernel,
        grid=(depth, n // tn),
        in_specs=[
            pl.BlockSpec((8, d), lambda l, j: (0, 0)),
            pl.BlockSpec((1, d, tn), lambda l, j: (l, 0, j)),
            pl.BlockSpec((1, 1, tn), lambda l, j: (l, 0, j)),
        ],
        out_specs=pl.BlockSpec((1, 8, tn), lambda l, j: (l, 0, j)),
        out_shape=jax.ShapeDtypeStruct((depth, 8, n), F32),
        compiler_params=_cparams(("arbitrary", "arbitrary"), 40),
        name="ada_mods",
    )(vec8, ada_w, ada_b.reshape(depth, 1, n))


def _mod_map(n_mod_rows):
    if n_mod_rows == 1:
        return lambda b, *_: (0, 0, 0)
    return lambda b, *_: (b, 0, 0)


def _gqa_proj_kernel(*refs, rope, tm):
    if rope:
        x_ref, mod_ref, g1_ref, w_ref, hm_ref, qg_ref, kg_ref, cos_ref, sin_ref, qt_ref, k_ref, vt_ref = refs
    else:
        x_ref, mod_ref, g1_ref, w_ref, hm_ref, qg_ref, kg_ref, qt_ref, k_ref, vt_ref = refs
    h = _norm_mod(x_ref[0], g1_ref[...], mod_ref[0, 0:1, :], mod_ref[0, 1:2, :]).astype(BF)
    qkv = jnp.dot(h, w_ref[...], preferred_element_type=F32)
    q = qkv[:, :AT_WIDTH]
    k = qkv[:, AT_WIDTH:AT_WIDTH + AT_KV_WIDTH]
    v = qkv[:, AT_WIDTH + AT_KV_WIDTH:]
    hm = hm_ref[...]
    q = q * lax.rsqrt(jnp.dot((q * q).astype(BF), hm, preferred_element_type=F32) + EPS) * qg_ref[...]
    hk = hm[:AT_KV_WIDTH, :AT_KV_WIDTH]
    k = k * lax.rsqrt(jnp.dot((k * k).astype(BF), hk, preferred_element_type=F32) + EPS) * kg_ref[...]
    if rope:
        cos = cos_ref[...]
        sin = sin_ref[...]
        even = (lax.broadcasted_iota(jnp.int32, (tm, LANES), 1) % 2) == 0

        def rot(t):
            partner = jnp.where(even, pltpu.roll(t, LANES - 1, 1), pltpu.roll(t, 1, 1))
            return t * cos + partner * sin
    else:
        def rot(t):
            return t
    for c in range(AT_WIDTH // LANES):
        qc = rot(q[:, c * LANES:(c + 1) * LANES]) * (QK_SCALE * LOG2E)
        qt_ref[0, c * LANES:(c + 1) * LANES, :] = qc.T.astype(BF)
    for c in range(AT_KV_WIDTH // LANES):
        kc = rot(k[:, c * LANES:(c + 1) * LANES]).astype(BF)
        k_ref[0, 2 * c] = kc[:, :HEAD_DIM]
        k_ref[0, 2 * c + 1] = kc[:, HEAD_DIM:]
    vt = v.T.astype(BF)
    for h in range(AT_KV_HEADS):
        vt_ref[0, h] = vt[h * HEAD_DIM:(h + 1) * HEAD_DIM]


def _gqa_proj(x, mod, g1, w, hm, qg, kg, cos, sin, *, tm=256):
    bx, t, d = x.shape
    rope = cos is not None
    n = w.shape[1]
    const = lambda b, i: (0, 0)
    in_specs = [
        pl.BlockSpec((1, tm, d), lambda b, i: (b, i, 0)),
        pl.BlockSpec((1, N_MOD, d), _mod_map(mod.shape[0])),
        pl.BlockSpec((1, d), const),
        pl.BlockSpec((d, n), const),
        pl.BlockSpec(hm.shape, const),
        pl.BlockSpec((1, AT_WIDTH), const),
        pl.BlockSpec((1, AT_KV_WIDTH), const),
    ]
    args = [x, mod, g1, w, hm, qg, kg]
    if rope:
        in_specs += [pl.BlockSpec((tm, LANES), lambda b, i: (i, 0))] * 2
        args += [cos, sin]
    return pl.pallas_call(
        functools.partial(_gqa_proj_kernel, rope=rope, tm=tm),
        grid=(bx, t // tm),
        in_specs=in_specs,
        out_specs=[
            pl.BlockSpec((1, AT_WIDTH, tm), lambda b, i: (b, 0, i)),
            pl.BlockSpec((1, AT_KV_HEADS, tm, HEAD_DIM), lambda b, i: (b, 0, i, 0)),
            pl.BlockSpec((1, AT_KV_HEADS, HEAD_DIM, tm), lambda b, i: (b, 0, 0, i)),
        ],
        out_shape=[
            jax.ShapeDtypeStruct((bx, AT_WIDTH, t), BF),
            jax.ShapeDtypeStruct((bx, AT_KV_HEADS, t, HEAD_DIM), BF),
            jax.ShapeDtypeStruct((bx, AT_KV_HEADS, HEAD_DIM, t), BF),
        ],
        compiler_params=_cparams(("parallel", "parallel"), 48),
        name="gqa_proj_rope" if rope else "gqa_proj_ctx",
    )(*args)


def _flash_kernel(*refs, tk, lookahead, with_latents):
    if with_latents:
        qt_ref, kc_ref, vtc_ref, kl_ref, vtl_ref, ot_ref, m_sc, l_sc, acc_sc = refs
        n_lat = kl_ref.shape[2] // tk
    else:
        qt_ref, kc_ref, vtc_ref, ot_ref, m_sc, l_sc, acc_sc = refs
        n_lat = 0
    n_heads = qt_ref.shape[1] // HEAD_DIM
    qs = [qt_ref[0, hh * HEAD_DIM:(hh + 1) * HEAD_DIM, :] for hh in range(n_heads)]
    chunks = [(kc_ref.at[0, 0], vtc_ref.at[0, 0])]
    chunks += [(kl_ref.at[0, 0, c * tk:(c + 1) * tk, :], vtl_ref.at[0, 0, :, c * tk:(c + 1) * tk]) for c in range(n_lat)]

    def scores(hh, c):
        return jnp.dot(chunks[c][0][...], qs[hh], preferred_element_type=F32)

    def weighted(c, s, m):
        p = jnp.exp2(s - m)
        psum = p.reshape(p.shape[0] // SUBLANES, SUBLANES, p.shape[1]).sum(axis=0)
        return jnp.dot(chunks[c][1][...], p.astype(BF), preferred_element_type=F32), psum

    units = [(hh, c) for hh in range(n_heads) for c in range(len(chunks))]
    pending = [scores(*u) for u in units[:lookahead]]
    m0, acc, den8 = {}, {}, {}
    for n, (hh, c) in enumerate(units):
        if n + lookahead < len(units):
            pending.append(scores(*units[n + lookahead]))
        s = pending.pop(0)
        if c == 0:
            m0[hh] = jnp.max(s, axis=0, keepdims=True)
        pv, psum = weighted(c, s, m0[hh])
        acc[hh] = pv if c == 0 else acc[hh] + pv
        den8[hh] = psum if c == 0 else den8[hh] + psum
    denom = [jnp.sum(den8[hh], axis=0, keepdims=True) for hh in range(n_heads)]
    bad = jnp.zeros_like(denom[0])
    for dn in denom:
        bad = jnp.where(jnp.isfinite(dn), bad, 1.0)
    overflowed = jnp.max(bad) > 0.0

    @pl.when(jnp.logical_not(overflowed))
    def _():
        for hh in range(n_heads):
            ot_ref[0, hh * HEAD_DIM:(hh + 1) * HEAD_DIM, :] = (acc[hh] * (1.0 / denom[hh])).astype(BF)

    @pl.when(overflowed)
    def _():
        for hh in range(n_heads):
            m_sc[...] = jnp.full(m_sc.shape, NEG_BIG, F32)
            l_sc[...] = jnp.zeros(l_sc.shape, F32)
            acc_sc[...] = jnp.zeros(acc_sc.shape, F32)

            def update(k, vt, q=qs[hh]):
                s = jnp.dot(k, q, preferred_element_type=F32)
                m_prev = m_sc[...]
                m_new = jnp.maximum(m_prev, jnp.max(s, axis=0, keepdims=True))
                alpha = jnp.exp2(m_prev - m_new)
                p = jnp.exp2(s - m_new)
                l_sc[...] = alpha * l_sc[...] + jnp.sum(p, axis=0, keepdims=True)
                acc_sc[...] = alpha * acc_sc[...] + jnp.dot(vt, p.astype(BF), preferred_element_type=F32)
                m_sc[...] = m_new

            update(kc_ref[0, 0], vtc_ref[0, 0])

            def body(c, carry, update=update):
                off = pl.multiple_of(c * tk, tk)
                update(kl_ref[0, 0, pl.ds(off, tk), :], vtl_ref[0, 0, :, pl.ds(off, tk)])
                return carry

            if with_latents:
                lax.fori_loop(0, n_lat, body, 0)
            ot_ref[0, hh * HEAD_DIM:(hh + 1) * HEAD_DIM, :] = (acc_sc[...] * (1.0 / l_sc[...])).astype(BF)


def _flash(qt, kc, vtc, kl=None, vtl=None, *, tq, tk=256, lookahead=3, heads_per_step=4):
    b, w, t = qt.shape
    kv = kc.shape[1]
    heads = w // HEAD_DIM
    group = heads // kv
    hps = heads_per_step
    assert group % hps == 0
    kv_map = lambda bi, h, i: (bi, h * hps // group, 0, 0)
    in_specs = [
        pl.BlockSpec((1, hps * HEAD_DIM, tq), lambda bi, h, i: (bi, h, i)),
        pl.BlockSpec((1, 1) + kc.shape[2:], kv_map),
        pl.BlockSpec((1, 1) + vtc.shape[2:], kv_map),
    ]
    args = [qt, kc, vtc]
    if kl is not None:
        in_specs += [pl.BlockSpec((1, 1) + kl.shape[2:], kv_map), pl.BlockSpec((1, 1) + vtl.shape[2:], kv_map)]
        args += [kl, vtl]
    return pl.pallas_call(
        functools.partial(_flash_kernel, tk=tk, lookahead=lookahead, with_latents=kl is not None),
        grid=(b, heads // hps, t // tq),
        in_specs=in_specs,
        out_specs=pl.BlockSpec((1, hps * HEAD_DIM, tq), lambda bi, h, i: (bi, h, i)),
        out_shape=jax.ShapeDtypeStruct((b, w, t), BF),
        scratch_shapes=[pltpu.VMEM((1, tq), F32), pltpu.VMEM((1, tq), F32), pltpu.VMEM((HEAD_DIM, tq), F32)],
        compiler_params=_cparams(("parallel", "parallel", "parallel"), 48),
        name="gqa_flash" if kl is not None else "gqa_flash_ctx",
    )(*args)


def _mlp_kernel(*refs, final, pending):
    refs = list(refs)
    x_ref = refs.pop(0)
    d_ref = refs.pop(0) if pending == "delta" else None
    ot_ref, wo_ref = (refs.pop(0), refs.pop(0)) if pending == "attn" else (None, None)
    mod_ref, g2_ref, w1_ref, w2_ref = refs[:4]
    fg_ref = refs[4] if final else None
    o_ref, xs_sc, h_sc, acc_sc = refs[-4:]
    j = pl.program_id(2)

    @pl.when(j == 0)
    def _():
        xs = x_ref[0]
        if pending == "delta":
            xs = xs + d_ref[0]
        elif pending == "attn":
            y = lax.dot_general(ot_ref[0], wo_ref[...], (((0,), (0,)), ((), ())), preferred_element_type=F32)
            xs = xs + mod_ref[0, 2:3, :] * y
        xs_sc[...] = xs
        h_sc[...] = _norm_mod(xs, g2_ref[...], mod_ref[0, 3:4, :], mod_ref[0, 4:5, :]).astype(BF)
        acc_sc[...] = jnp.zeros(acc_sc.shape, F32)

    a = jnp.dot(h_sc[...], w1_ref[...], preferred_element_type=F32)
    a = jnp.square(jnp.maximum(a, 0.0)).astype(BF)
    acc_sc[...] += jnp.dot(a, w2_ref[...], preferred_element_type=F32)

    @pl.when(j == pl.num_programs(2) - 1)
    def _():
        y = xs_sc[...] + mod_ref[0, 5:6, :] * acc_sc[...]
        if final:
            ms = jnp.mean(y * y, axis=-1, keepdims=True)
            y = y * lax.rsqrt(ms + EPS) * fg_ref[...]
        o_ref[0] = y


def _mlp(x, mod, g2, w1, w2, layer, final_g=None, *, delta=None, attn=None, tm, tf=512):
    b, t, d = x.shape
    f = w1.shape[2]
    final = final_g is not None
    row_spec = pl.BlockSpec((1, tm, d), lambda bi, i, j: (bi, i, 0))
    in_specs, args, pending = [row_spec], [x], None
    if delta is not None:
        in_specs, args, pending = in_specs + [row_spec], args + [delta], "delta"
    if attn is not None:
        ot, wo = attn
        in_specs += [pl.BlockSpec((1, ot.shape[1], tm), lambda bi, i, j: (bi, 0, i)),
                     pl.BlockSpec(wo.shape, lambda bi, i, j: (0, 0))]
        args, pending = args + [ot, wo], "attn"
    in_specs += [
        pl.BlockSpec((1, N_MOD, d), _mod_map(mod.shape[0])),
        pl.BlockSpec((1, d), lambda bi, i, j: (0, 0)),
        pl.BlockSpec((None, d, tf), lambda bi, i, j: (layer, 0, j)),
        pl.BlockSpec((None, tf, d), lambda bi, i, j: (layer, j, 0)),
    ]
    args += [mod, g2, w1, w2]
    if final:
        in_specs.append(pl.BlockSpec((1, d), lambda bi, i, j: (0, 0)))
        args.append(final_g)
    return pl.pallas_call(
        functools.partial(_mlp_kernel, final=final, pending=pending),
        grid=(b, t // tm, f // tf),
        in_specs=in_specs,
        out_specs=pl.BlockSpec((1, tm, d), lambda bi, i, j: (bi, i, 0)),
        out_shape=jax.ShapeDtypeStruct((b, t, d), F32),
        scratch_shapes=[pltpu.VMEM((tm, d), F32), pltpu.VMEM((tm, d), BF), pltpu.VMEM((tm, d), F32)],
        compiler_params=_cparams(("parallel", "parallel", "arbitrary"), 56),
        name="mlp_final" if final else "mlp",
    )(*args)


def _na_proj_kernel(x_ref, mod_ref, g1_ref, w_ref, qt_ref, k_ref, v_ref):
    h = _norm_mod(x_ref[0], g1_ref[...], mod_ref[0, 0:1, :], mod_ref[0, 1:2, :]).astype(BF)
    qkv = jnp.dot(h, w_ref[...], preferred_element_type=F32)
    for c in range(NA_WIDTH // LANES):
        qt_ref[0, c * LANES:(c + 1) * LANES, :] = (qkv[:, c * LANES:(c + 1) * LANES] * (QK_SCALE * LOG2E)).T.astype(BF)
    k_ref[0] = qkv[:, NA_WIDTH:2 * NA_WIDTH].astype(BF)
    v_ref[0] = qkv[:, 2 * NA_WIDTH:].astype(BF)


def _na_proj(x, mod, g1, w, *, tm=256):
    bx, t, d = x.shape
    return pl.pallas_call(
        _na_proj_kernel,
        grid=(bx, t // tm),
        in_specs=[
            pl.BlockSpec((1, tm, d), lambda b, i: (b, i, 0)),
            pl.BlockSpec((1, N_MOD, d), _mod_map(mod.shape[0])),
            pl.BlockSpec((1, d), lambda b, i: (0, 0)),
            pl.BlockSpec(w.shape, lambda b, i: (0, 0)),
        ],
        out_specs=[
            pl.BlockSpec((1, NA_WIDTH, tm), lambda b, i: (b, 0, i)),
            pl.BlockSpec((1, tm, NA_WIDTH), lambda b, i: (b, i, 0)),
            pl.BlockSpec((1, tm, NA_WIDTH), lambda b, i: (b, i, 0)),
        ],
        out_shape=[
            jax.ShapeDtypeStruct((bx, NA_WIDTH, t), BF),
            jax.ShapeDtypeStruct((bx, t, NA_WIDTH), BF),
            jax.ShapeDtypeStruct((bx, t, NA_WIDTH), BF),
        ],
        compiler_params=_cparams(("parallel", "parallel"), 48),
        name="na_proj",
    )(x, mod, g1, w)


def _na_key_base(g, n_groups):
    return jnp.clip(NA_QROWS * g - NA_WIN_ROWS // 2, 0, NA_QROWS * n_groups - NA_KROWS)


def _na_kernel(*refs, n_groups, per_step):
    qt_ref, k_ref, v_ref, kc_ref, vc_ref, quad_ref = refs[:6]
    mask_refs = refs[6:6 + per_step]
    ot_ref = refs[6 + per_step]
    tq = NA_QROWS * GRID_W
    ck = NA_CHUNK_ROWS * GRID_W
    n_nb = NA_KROWS // NA_CHUNK_ROWS
    tn = (((0,), (0,)), ((), ()))
    upper = lax.broadcasted_iota(jnp.int32, (2 * HEAD_DIM, tq), 0) < HEAD_DIM
    geo = []
    for gi in range(per_step):
        g = pl.program_id(2) * per_step + gi
        kb_row = _na_key_base(g, n_groups)
        q2 = qt_ref[0, :, gi * tq:(gi + 1) * tq]
        qh = (jnp.where(upper, q2, jnp.zeros_like(q2)), jnp.where(upper, jnp.zeros_like(q2), q2))
        geo.append((pl.multiple_of(kb_row * GRID_W, GRID_W), kb_row - NA_QROWS * g + NA_WIN_ROWS, qh))

    def scores(gi, half, j):
        kb, e0, qh = geo[gi]
        if j < 0:
            return jnp.dot(kc_ref[0], qh[half], preferred_element_type=F32)
        bias = jnp.concatenate([quad_ref[half, e0 + NA_CHUNK_ROWS * j + i] for i in range(NA_CHUNK_ROWS)], axis=0)
        bias = bias + mask_refs[gi][0, j * ck:(j + 1) * ck, :]
        return jnp.dot(k_ref[0, pl.ds(kb + j * ck, ck), :], qh[half], preferred_element_type=F32) + bias

    def values(gi, j):
        return vc_ref[0] if j < 0 else v_ref[0, pl.ds(geo[gi][0] + j * ck, ck), :]

    def finish(gi, o0, l0, o1, l1):
        ot_ref[0, :, gi * tq:(gi + 1) * tq] = jnp.where(upper, o0 * (1.0 / l0), o1 * (1.0 / l1)).astype(BF)

    heads = [(gi, half) for gi in range(per_step) for half in range(2)]
    units = [(gi, half, j) for j in range(-1, n_nb) for gi, half in heads]
    lookahead = max(NA_LOOKAHEAD, len(heads))
    pending = [scores(*u) for u in units[:lookahead]]
    m0 = {u[:2]: jnp.max(s, axis=0, keepdims=True) for u, s in zip(units[:len(heads)], pending)}
    acc = {}
    den = {}
    for n, (gi, half, j) in enumerate(units):
        if n + lookahead < len(units):
            pending.append(scores(*units[n + lookahead]))
        p = jnp.exp2(pending.pop(0) - m0[gi, half])
        pv = lax.dot_general(values(gi, j), p.astype(BF), tn, preferred_element_type=F32)
        ps = jnp.sum(p, axis=0, keepdims=True)
        acc[gi, half] = pv if j < 0 else acc[gi, half] + pv
        den[gi, half] = ps if j < 0 else den[gi, half] + ps
    bad = jnp.zeros((1, tq), F32)
    for key in heads:
        bad = jnp.where(jnp.isfinite(den[key]), bad, 1.0)
    overflowed = jnp.max(bad) > 0.0

    @pl.when(jnp.logical_not(overflowed))
    def _():
        for gi in range(per_step):
            finish(gi, acc[gi, 0], den[gi, 0], acc[gi, 1], den[gi, 1])

    @pl.when(overflowed)
    def _():
        for gi in range(per_step):
            outs = []
            for half in range(2):
                s_all = [scores(gi, half, j) for j in range(-1, n_nb)]
                m = s_all[0].max(axis=0, keepdims=True)
                for s in s_all[1:]:
                    m = jnp.maximum(m, jnp.max(s, axis=0, keepdims=True))
                o = None
                l = None
                for j, s in zip(range(-1, n_nb), s_all):
                    p = jnp.exp2(s - m)
                    pv = lax.dot_general(values(gi, j), p.astype(BF), tn, preferred_element_type=F32)
                    ps = jnp.sum(p, axis=0, keepdims=True)
                    o = pv if o is None else o + pv
                    l = ps if l is None else l + ps
                outs += [o, l]
            finish(gi, *outs)


def _na_quad_kernel(rpb_ref, quad_ref):
    h = pl.program_id(0)
    n_dr, n_dc = rpb_ref.shape[1:]
    kc = lax.broadcasted_iota(jnp.int32, (GRID_W, GRID_W), 0)
    qc = lax.broadcasted_iota(jnp.int32, (GRID_W, GRID_W), 1)
    dc = kc - qc + NA_WIN_COLS - 1
    cs = jnp.clip(qc - NA_WIN_COLS // 2, 0, GRID_W - NA_WIN_COLS)
    col_ok = (kc >= cs) & (kc < cs + NA_WIN_COLS)
    toeplitz = []
    for dr in range(n_dr):
        t = jnp.zeros((GRID_W, GRID_W), F32)
        for c in range(n_dc):
            t = jnp.where(dc == c, rpb_ref[h, dr, c], t)
        toeplitz.append(jnp.where(col_ok, t * LOG2E, NEG_BIG))
    for e in range(NA_EROWS):
        for a in range(NA_QROWS):
            dr = e - a - 1
            blk = toeplitz[dr] if 0 <= dr < n_dr else jnp.zeros((GRID_W, GRID_W), F32)
            quad_ref[0, e, :, a * GRID_W:(a + 1) * GRID_W] = blk


def _na_bias_tables(rpb, n_rows):
    n_groups = n_rows // NA_QROWS
    n_h = rpb.shape[0]
    quad = pl.pallas_call(
        _na_quad_kernel,
        grid=(n_h,),
        in_specs=[pl.BlockSpec(memory_space=pltpu.SMEM)],
        out_specs=pl.BlockSpec((1, NA_EROWS, GRID_W, NA_QROWS * GRID_W), lambda h: (h, 0, 0, 0)),
        out_shape=jax.ShapeDtypeStruct((n_h, NA_EROWS, GRID_W, NA_QROWS * GRID_W), F32),
        compiler_params=_cparams(("parallel",), 32),
        name="na_bias_quad",
    )(rpb)
    masks = []
    for g in (0, 1, n_groups - 1):
        r = NA_QROWS * g + np.arange(NA_QROWS)[None, None, :, None]
        kb = int(np.clip(NA_QROWS * g - NA_WIN_ROWS // 2, 0, n_rows - NA_KROWS))
        krow = kb + np.arange(NA_KROWS)[:, None, None, None]
        rs = np.clip(r - NA_WIN_ROWS // 2, 0, n_rows - NA_WIN_ROWS)
        ok = np.broadcast_to((krow >= rs) & (krow < rs + NA_WIN_ROWS), (NA_KROWS, GRID_W, NA_QROWS, GRID_W))
        masks.append(np.where(ok, 0.0, NEG_BIG).reshape(NA_KROWS * GRID_W, NA_QROWS * GRID_W))
    return quad, jnp.asarray(np.stack(masks), F32)


def _na_attend(qt, k, v, kc, vc, quad, rowmask, n_rows, *, per_step=NA_GROUPS_PER_STEP):
    b, w, s = qt.shape
    c = kc.shape[1]
    n_groups = n_rows // NA_QROWS
    tq = NA_QROWS * GRID_W
    pair = 2 * HEAD_DIM

    def mask_spec(gi):
        def index_map(bi, hp, st):
            g = st * per_step + gi
            return (jnp.where(g == 0, 0, jnp.where(g == n_groups - 1, 2, 1)), 0, 0)
        return pl.BlockSpec((1, NA_KROWS * GRID_W, tq), index_map)

    return pl.pallas_call(
        functools.partial(_na_kernel, n_groups=n_groups, per_step=per_step),
        grid=(b, w // pair, n_groups // per_step),
        in_specs=[
            pl.BlockSpec((1, pair, per_step * tq), lambda bi, hp, st: (bi, hp, st)),
            pl.BlockSpec((1, s, pair), lambda bi, hp, st: (bi, 0, hp)),
            pl.BlockSpec((1, s, pair), lambda bi, hp, st: (bi, 0, hp)),
            pl.BlockSpec((1, c, pair), lambda bi, hp, st: (bi, 0, hp)),
            pl.BlockSpec((1, c, pair), lambda bi, hp, st: (bi, 0, hp)),
            pl.BlockSpec((2, NA_EROWS, GRID_W, tq), lambda bi, hp, st: (hp, 0, 0, 0)),
        ] + [mask_spec(gi) for gi in range(per_step)],
        out_specs=pl.BlockSpec((1, pair, per_step * tq), lambda bi, hp, st: (bi, hp, st)),
        out_shape=jax.ShapeDtypeStruct((b, w, s), BF),
        compiler_params=_cparams(("parallel", "parallel", "arbitrary"), 48),
        name="na_attend",
    )(qt, k, v, kc, vc, quad, *([rowmask] * per_step))


def _conv_pw1_kernel(x_ref, mod_ref, g1_ref, w_ref, b_ref, u_ref):
    h = _norm_mod(x_ref[0], g1_ref[...], mod_ref[0, 0:1, :], mod_ref[0, 1:2, :]).astype(BF)
    ag = jnp.dot(h, w_ref[...], preferred_element_type=F32) + b_ref[...]
    d = u_ref.shape[2]
    u_ref[0] = ag[:, :d] * jax.nn.sigmoid(ag[:, d:])


def _conv_pw1(x, mod, g1, w, b, *, tm=512):
    bx, t, d = x.shape
    return pl.pallas_call(
        _conv_pw1_kernel,
        grid=(bx, t // tm),
        in_specs=[
            pl.BlockSpec((1, tm, d), lambda bi, i: (bi, i, 0)),
            pl.BlockSpec((1, N_MOD, d), _mod_map(mod.shape[0])),
            pl.BlockSpec((1, d), lambda bi, i: (0, 0)),
            pl.BlockSpec(w.shape, lambda bi, i: (0, 0)),
            pl.BlockSpec((1, 2 * d), lambda bi, i: (0, 0)),
        ],
        out_specs=pl.BlockSpec((1, tm, d), lambda bi, i: (bi, i, 0)),
        out_shape=jax.ShapeDtypeStruct((bx, t, d), F32),
        compiler_params=_cparams(("parallel", "parallel"), 48),
        name="conv_pw1_glu",
    )(x, mod, g1, w, b)


def _conv_tail_kernel(u_ref, up_ref, un_ref, wdw_ref, bdw_ref, lg_ref, lb_ref, w2_ref, b2_ref, x_ref, mod_ref,
                      o_ref, buf_sc, cv_sc, *, tm):
    i = pl.program_id(1)
    last = pl.num_programs(1) - 1
    buf_sc[0:CONV_HALO, :] = jnp.where(i > 0, up_ref[0], 0.0)
    buf_sc[CONV_HALO:CONV_HALO + tm, :] = u_ref[0]
    buf_sc[CONV_HALO + tm:, :] = jnp.where(i < last, un_ref[0], 0.0)
    assert CONV_HALO - CONV_WIDTH // 2 == 1
    d = buf_sc.shape[1]
    for lb in range(d // CONV_LANES):
        ls = slice(lb * CONV_LANES, (lb + 1) * CONV_LANES)
        for rc in range(tm // CONV_ROWS):
            r0 = rc * CONV_ROWS
            acc = None
            for r in range(SUBLANES):
                part = None
                for j in range(r, CONV_WIDTH + 1, SUBLANES):
                    if j == 0:
                        continue
                    rows = slice(r0 + j - r, r0 + j - r + CONV_ROWS + SUBLANES)
                    term = buf_sc[rows, ls] * wdw_ref[j - 1:j, ls]
                    part = term if part is None else part + term
                part = part[r:r + CONV_ROWS]
                acc = part if acc is None else acc + part
            cv_sc[r0:r0 + CONV_ROWS, ls] = acc
    u = cv_sc[...] + bdw_ref[...]
    mu = jnp.mean(u, axis=-1, keepdims=True)
    uc = u - mu
    var = jnp.mean(uc * uc, axis=-1, keepdims=True)
    y = uc * lax.rsqrt(var + EPS) * lg_ref[...] + lb_ref[...]
    y = (y * jax.nn.sigmoid(y)).astype(BF)
    z = jnp.dot(y, w2_ref[...], preferred_element_type=F32) + b2_ref[...]
    o_ref[0] = x_ref[0] + mod_ref[0, 2:3, :] * z


def _conv_tail(u, wdw, bdw, lg, lb, w2, b2, x, mod, *, tm=256):
    bx, t, d = x.shape
    hb = tm // CONV_HALO
    n_halo = t // CONV_HALO
    vec = lambda: pl.BlockSpec((1, d), lambda bi, i: (0, 0))
    return pl.pallas_call(
        functools.partial(_conv_tail_kernel, tm=tm),
        grid=(bx, t // tm),
        in_specs=[
            pl.BlockSpec((1, tm, d), lambda bi, i: (bi, i, 0)),
            pl.BlockSpec((1, CONV_HALO, d), lambda bi, i: (bi, jnp.maximum(i * hb - 1, 0), 0)),
            pl.BlockSpec((1, CONV_HALO, d), lambda bi, i: (bi, jnp.minimum((i + 1) * hb, n_halo - 1), 0)),
            pl.BlockSpec(wdw.shape, lambda bi, i: (0, 0)),
            vec(), vec(), vec(),
            pl.BlockSpec(w2.shape, lambda bi, i: (0, 0)),
            vec(),
            pl.BlockSpec((1, tm, d), lambda bi, i: (bi, i, 0)),
            pl.BlockSpec((1, N_MOD, d), _mod_map(mod.shape[0])),
        ],
        out_specs=pl.BlockSpec((1, tm, d), lambda bi, i: (bi, i, 0)),
        out_shape=jax.ShapeDtypeStruct((bx, t, d), F32),
        scratch_shapes=[pltpu.VMEM((tm + 2 * CONV_HALO, d), F32), pltpu.VMEM((tm, d), F32)],
        compiler_params=_cparams(("parallel", "parallel"), 48),
        name="conv_tail",
    )(u, u, u, wdw, bdw, lg, lb, w2, b2, x, mod)


def _ft_tables(n_seq, gw):
    nb = n_seq // FT_NA
    ka = np.arange(FT_NA)[None, :, None]
    na = np.arange(FT_NA)[None, None, :]
    jb = np.arange(nb)[:, None, None]
    ph = 2 * np.pi * ((ka * (nb * na + jb)) % n_seq) / n_seq
    t1 = np.concatenate([np.cos(ph), -np.sin(ph)], axis=1)
    kb = np.arange(nb)
    ph = 2 * np.pi * ((kb[:, None] * kb[None, :]) % nb) / nb
    c3, s3 = np.cos(ph), np.sin(ph)
    t2 = np.block([[c3, s3], [-s3, c3]]) / np.sqrt(n_seq)
    m = np.arange(gw)
    ph = 2 * np.pi * ((m[:, None] * m[None, :]) % gw) / gw
    tw = np.stack([np.cos(ph), np.sin(ph)]) / np.sqrt(gw)
    return tuple(jnp.asarray(t, F32).astype(BF) for t in (t1, t2, tw))


def _ft_prep_kernel(x_ref, mod_ref, g1_ref, h_ref, *, nb):
    h = _norm_mod(x_ref[0], g1_ref[...], mod_ref[0, 0:1, :], mod_ref[0, 1:2, :])
    for a in range(h.shape[0] // nb):
        h_ref[0, :, a, :] = h[a * nb:(a + 1) * nb, :]


def _ft_prep(x, mod, g1, *, tm=1024):
    bx, t, d = x.shape
    nb = t // FT_NA
    return pl.pallas_call(
        functools.partial(_ft_prep_kernel, nb=nb),
        grid=(bx, t // tm),
        in_specs=[
            pl.BlockSpec((1, tm, d), lambda bi, i: (bi, i, 0)),
            pl.BlockSpec((1, N_MOD, d), _mod_map(mod.shape[0])),
            pl.BlockSpec((1, d), lambda bi, i: (0, 0)),
        ],
        out_specs=pl.BlockSpec((1, nb, tm // nb, d), lambda bi, i: (bi, 0, i, 0)),
        out_shape=jax.ShapeDtypeStruct((bx, nb, FT_NA, d), F32),
        compiler_params=_cparams(("parallel", "parallel"), 48),
        name="ft_prep",
    )(x, mod, g1)


def _ft_seq1_kernel(h_ref, t1_ref, y_ref, *, tb, d):
    for j in range(tb):
        y = jnp.dot(t1_ref[j], h_ref[0, j].astype(BF), preferred_element_type=F32)
        y_ref[0, :, j, :d] = y[:FT_NA]
        y_ref[0, :, j, d:] = y[FT_NA:]


def _ft_seq1(hp, t1, *, tb=8):
    bx, nb, _, d = hp.shape
    return pl.pallas_call(
        functools.partial(_ft_seq1_kernel, tb=tb, d=d),
        grid=(bx, nb // tb),
        in_specs=[
            pl.BlockSpec((1, tb, FT_NA, d), lambda bi, i: (bi, i, 0, 0)),
            pl.BlockSpec((tb, 2 * FT_NA, FT_NA), lambda bi, i: (i, 0, 0)),
        ],
        out_specs=pl.BlockSpec((1, FT_NA, tb, 2 * d), lambda bi, i: (bi, 0, i, 0)),
        out_shape=jax.ShapeDtypeStruct((bx, FT_NA, nb, 2 * d), F32),
        compiler_params=_cparams(("parallel", "parallel"), 48),
        name="ft_seq1",
    )(hp, t1)


def _ft_seq2_kernel(y_ref, t2_ref, tw_ref, w_ref, b_ref, mod_ref, o_ref, *, ta):
    d = w_ref.shape[0]
    nb = y_ref.shape[2]
    gw = tw_ref.shape[1]
    for a in range(ta):
        yb = y_ref[0, a]
        rhs = jnp.concatenate([yb[:, :d], yb[:, d:]], axis=0).astype(BF)
        aa = jnp.dot(t2_ref[...], rhs, preferred_element_type=F32).astype(BF)
        z = [jnp.dot(aa[:nb, g * gw:(g + 1) * gw], tw_ref[0], preferred_element_type=F32)
             + jnp.dot(aa[nb:, g * gw:(g + 1) * gw], tw_ref[1], preferred_element_type=F32)
             for g in range(d // gw)]
        z = jnp.concatenate(z, axis=1).astype(BF)
        yl = jnp.dot(z, w_ref[...], preferred_element_type=F32) + b_ref[...]
        o_ref[0, :, a, :] = mod_ref[0, 2:3, :] * yl


def _ft_seq2(y, t2, tw, w, b, mod, *, ta=8):
    bx, _, nb, d2 = y.shape
    d = d2 // 2
    out = pl.pallas_call(
        functools.partial(_ft_seq2_kernel, ta=ta),
        grid=(bx, FT_NA // ta),
        in_specs=[
            pl.BlockSpec((1, ta, nb, d2), lambda bi, i: (bi, i, 0, 0)),
            pl.BlockSpec(t2.shape, lambda bi, i: (0, 0)),
            pl.BlockSpec(tw.shape, lambda bi, i: (0, 0, 0)),
            pl.BlockSpec(w.shape, lambda bi, i: (0, 0)),
            pl.BlockSpec((1, d), lambda bi, i: (0, 0)),
            pl.BlockSpec((1, N_MOD, d), _mod_map(mod.shape[0])),
        ],
        out_specs=pl.BlockSpec((1, nb, ta, d), lambda bi, i: (bi, 0, i, 0)),
        out_shape=jax.ShapeDtypeStruct((bx, nb, FT_NA, d), F32),
        compiler_params=_cparams(("parallel", "parallel"), 48),
        name="ft_seq2_mix",
    )(y, t2, tw, w, b, mod)
    return out.reshape(bx, nb * FT_NA, d)


def _rope_tables(n_tok):
    t = jnp.arange(n_tok)
    row = (t // GRID_W).astype(F32)
    col = (t % GRID_W).astype(F32)
    n_axis = HEAD_DIM // 4
    inv = ROPE_THETA ** (-jnp.arange(n_axis, dtype=F32) / n_axis)
    ang = jnp.concatenate([row[:, None] * inv, col[:, None] * inv], axis=-1)
    ang = jnp.tile(jnp.repeat(ang, 2, axis=-1), (1, LANES // HEAD_DIM))
    sign = jnp.where(jnp.arange(LANES) % 2 == 0, -1.0, 1.0).astype(F32)
    return jnp.cos(ang), jnp.sin(ang) * sign


def _row(v):
    return v.reshape(1, -1)


def kernel(x, c, ctx, c_ctx, ada_w, ada_b, norm1_g, norm2_g, mlp_w1, mlp_w2, final_g, at_w_qkv, at_q_g, at_k_g, at_w_o, na_w_qkv, na_rpb, na_w_o, cv_w_pw1, cv_b_pw1, cv_w_dw, cv_b_dw, cv_ln_g, cv_ln_b, cv_w_pw2, cv_b_pw2, ft_w, ft_b):
    bsz, n_lat, d = x.shape
    depth = ada_w.shape[0]
    n_rows = n_lat // GRID_W
    n_ctx = ctx.shape[1]

    vec8 = jnp.zeros((8, d), F32).at[:bsz].set(c).at[bsz].set(c_ctx)
    mods = _ada_mods(vec8, ada_w, ada_b)

    w1m = mlp_w1.astype(BF)
    w2m = mlp_w2.astype(BF)
    hm = jnp.asarray(np.kron(np.eye(AT_HEADS), np.full((HEAD_DIM, HEAD_DIM), 1.0 / HEAD_DIM)), BF)
    h_ctx = ctx
    for i in range(depth):
        kind = i % N_MIXERS
        occ = i // N_MIXERS
        ctx_later = any((j % N_MIXERS) in (0, 1) for j in range(i + 1, depth))
        ml = mods[i, :bsz].reshape(bsz, N_MOD, d)
        mc = mods[i, bsz:bsz + 1].reshape(1, N_MOD, d)
        g1 = _row(norm1_g[i])
        delta = attn = attn_ctx = None
        if kind == 0:
            w = at_w_qkv[occ].astype(BF)
            wo = at_w_o[occ].astype(BF)
            qg = _row(jnp.tile(at_q_g[occ], AT_HEADS))
            kg = _row(jnp.tile(at_k_g[occ], AT_KV_HEADS))
            cos, sin = _rope_tables(n_lat)
            qt_l, k_l, vt_l = _gqa_proj(x, ml, g1, w, hm, qg, kg, cos, sin, tm=512)
            qt_c, k_c, vt_c = _gqa_proj(h_ctx, mc, g1, w, hm, qg, kg, None, None, tm=n_ctx)
            attn = (_flash(qt_l, k_c, vt_c, k_l, vt_l, tq=512), wo)
            if ctx_later:
                attn_ctx = (_flash(qt_c, k_c, vt_c, tq=n_ctx), wo)
        elif kind == 1:
            w = na_w_qkv[occ].astype(BF)
            wo = na_w_o[occ].astype(BF)
            qt_l, k_l, v_l = _na_proj(x, ml, g1, w, tm=512)
            qt_c, k_c, v_c = _na_proj(h_ctx, mc, g1, w, tm=n_ctx)
            quad, rowmask = _na_bias_tables(na_rpb[occ], n_rows)
            attn = (_na_attend(qt_l, k_l, v_l, k_c, v_c, quad, rowmask, n_rows), wo)
            if ctx_later:
                raise NotImplementedError("context output of a neighbourhood layer is not needed at this depth")
        elif kind == 2:
            w1 = cv_w_pw1[occ].astype(BF)
            w2 = cv_w_pw2[occ].astype(BF)
            wdw = jnp.zeros((CONV_WIDTH + 1, d), F32).at[:CONV_WIDTH].set(cv_w_dw[occ])
            cv = (wdw, _row(cv_b_dw[occ]), _row(cv_ln_g[occ]), _row(cv_ln_b[occ]), w2, _row(cv_b_pw2[occ]))
            u = _conv_pw1(x, ml, g1, w1, _row(cv_b_pw1[occ]), tm=1024)
            x = _conv_tail(u, *cv, x, ml, tm=512)
            if ctx_later:
                raise NotImplementedError("context output of a convolution layer is not needed at this depth")
        else:
            t1, t2, tw = _ft_tables(n_lat, d // FT_GROUPS)
            y1 = _ft_seq1(_ft_prep(x, ml, g1), t1)
            delta = _ft_seq2(y1, t2, tw, ft_w[occ].astype(BF), _row(ft_b[occ]), ml)
            if ctx_later:
                raise NotImplementedError("context output of a Fourier layer is not needed at this depth")
        g2 = _row(norm2_g[i])
        is_last = i == depth - 1
        x = _mlp(x, ml, g2, w1m, w2m, i, _row(final_g) if is_last else None, delta=delta, attn=attn, tm=1024, tf=1024)
        if ctx_later:
            h_ctx = _mlp(h_ctx, mc, g2, w1m, w2m, i, attn=attn_ctx, tm=n_ctx)
    return x
```
